```python
import math
import jax, jax.numpy as jnp
from jax import lax
import numpy as np

D_MODEL = 1024
BATCH = 16
SEQ = 256
DEPTH = 4
DEC_BATCH = 8
DEC_SEQ = 4096
PAST_LEN = 256

F32 = jnp.float32
EPS = 1e-6
NEG_INF = -1e30
GRID_W = 64
N_DIR = 2
N_HEADS = 8
N_KV_HEADS = 2
GQA = N_HEADS // N_KV_HEADS
HEAD_DIM = 64
SCALE = HEAD_DIM ** -0.5
WINDOW = 128
ATT_BLOCK = 128
ROPE_BASE = 10000.0
SSM_GROUP_CH = 16
SSM_GROUPS = 16
SSM_WIDTH = SSM_GROUPS * SSM_GROUP_CH
SSM_STATE = 64
GDN_HEADS = 4
GDN_DK = 64
GDN_DV = 64
GDN_K_W = GDN_HEADS * GDN_DK
GDN_V_W = GDN_HEADS * GDN_DV
GDN_CONV = 3
GDN_CHUNK = 64
ATT_Q_W = N_HEADS * HEAD_DIM
ATT_KV_W = N_KV_HEADS * HEAD_DIM
MIX_W = ATT_Q_W + SSM_WIDTH + GDN_V_W
D_FF = 4 * D_MODEL
IN_SIZES = (ATT_Q_W, ATT_KV_W, ATT_KV_W, SSM_WIDTH, GDN_K_W, GDN_K_W, GDN_V_W, GDN_V_W, N_DIR * GDN_HEADS, N_DIR * GDN_HEADS)
IN_W = sum(IN_SIZES)

kernel_name = 'hybrid_dit_s5_swa_gdn_step'


def rms_norm(x, g):
    xf = x.astype(F32)
    y = xf * lax.rsqrt(jnp.mean(xf * xf, axis=-1, keepdims=True) + EPS)
    return (y * g.astype(F32)).astype(x.dtype)


def l2_normalize(x):
    return x * lax.rsqrt(jnp.sum(x * x, axis=-1, keepdims=True) + EPS)


def split_columns(z):
    outs, start = [], 0
    for size in IN_SIZES:
        outs.append(z[..., start:start + size])
        start += size
    return outs


def axial_rope(x):
    B, L, H, hd = x.shape
    n_rows = L // GRID_W
    rows = jnp.repeat(jnp.arange(n_rows, dtype=F32), GRID_W)
    cols = jnp.tile(jnp.arange(GRID_W, dtype=F32), n_rows)
    n_freq = hd // 4
    inv_freq = jnp.power(ROPE_BASE, -jnp.arange(n_freq, dtype=F32) / n_freq)
    ang = jnp.concatenate([rows[:, None] * inv_freq, cols[:, None] * inv_freq], axis=-1)
    cos = jnp.cos(ang)[None, :, None, :]
    sin = jnp.sin(ang)[None, :, None, :]
    xf = x.astype(F32)
    x1, x2 = xf[..., 0::2], xf[..., 1::2]
    out = jnp.stack([x1 * cos - x2 * sin, x1 * sin + x2 * cos], axis=-1).reshape(B, L, H, hd)
    return out.astype(x.dtype)


def context_attention(q, k, v, sink):
    B, S = q.shape[:2]
    nb = S // ATT_BLOCK
    kf, vf = k.astype(F32), v.astype(F32)
    qb = q.astype(F32).reshape(B, nb, ATT_BLOCK, N_KV_HEADS, GQA, HEAD_DIM).transpose(1, 0, 2, 3, 4, 5)
    sink_col = sink.astype(F32).reshape(1, N_KV_HEADS, GQA, 1, 1)

    def block(q_blk):
        s = jnp.einsum('bqkgd,bskd->bkgqs', q_blk, kf) * SCALE
        logits = jnp.concatenate([s, jnp.broadcast_to(sink_col, s.shape[:-1] + (1,))], axis=-1)
        probs = jax.nn.softmax(logits, axis=-1)[..., :-1]
        return jnp.einsum('bkgqs,bskd->bqkgd', probs, vf)

    o = lax.map(block, qb)
    return o.transpose(1, 0, 2, 3, 4, 5).reshape(B, S, ATT_Q_W)


def latent_attention(q, k, v, ck, cv, sink):
    B, L = q.shape[:2]
    nb = L // ATT_BLOCK
    n_loc = 3 * ATT_BLOCK
    pad = ((0, 0), (ATT_BLOCK, ATT_BLOCK), (0, 0), (0, 0))
    kp = jnp.pad(k.astype(F32), pad)
    vp = jnp.pad(v.astype(F32), pad)
    ckf, cvf = ck.astype(F32), cv.astype(F32)
    qb = q.astype(F32).reshape(B, nb, ATT_BLOCK, N_KV_HEADS, GQA, HEAD_DIM).transpose(1, 0, 2, 3, 4, 5)
    sink_col = sink.astype(F32).reshape(1, N_KV_HEADS, GQA, 1, 1)
    q_off = jnp.arange(ATT_BLOCK)
    k_off = jnp.arange(n_loc) - ATT_BLOCK

    def block(args):
        i, q_blk = args
        start = i * ATT_BLOCK
        k_blk = lax.dynamic_slice_in_dim(kp, start, n_loc, axis=1)
        v_blk = lax.dynamic_slice_in_dim(vp, start, n_loc, axis=1)
        q_pos = start + q_off
        k_pos = start + k_off
        valid = (jnp.abs(q_pos[:, None] - k_pos[None, :]) <= WINDOW) & (k_pos >= 0)[None, :] & (k_pos < L)[None, :]
        s_loc = jnp.where(valid, jnp.einsum('bqkgd,bskd->bkgqs', q_blk, k_blk) * SCALE, NEG_INF)
        s_ctx = jnp.einsum('bqkgd,bskd->bkgqs', q_blk, ckf) * SCALE
        logits = jnp.concatenate([s_loc, s_ctx, jnp.broadcast_to(sink_col, s_loc.shape[:-1] + (1,))], axis=-1)
        probs = jax.nn.softmax(logits, axis=-1)
        return (jnp.einsum('bkgqs,bskd->bqkgd', probs[..., :n_loc], v_blk)
                + jnp.einsum('bkgqs,bskd->bqkgd', probs[..., n_loc:-1], cvf))

    o = lax.map(block, (jnp.arange(nb), qb))
    return o.transpose(1, 0, 2, 3, 4, 5).reshape(B, L, ATT_Q_W)


def s5_direction(u, lam_re, lam_im, log_step, b_re, b_im, h0_re, h0_im, reverse):
    step = jnp.exp(log_step)[:, None]
    mag = jnp.exp(lam_re * step)
    ang = lam_im * step
    ab_re, ab_im = mag * jnp.cos(ang), mag * jnp.sin(ang)
    den = lam_re * lam_re + lam_im * lam_im
    nr, ni = ab_re - 1.0, ab_im
    f_re = (nr * lam_re + ni * lam_im) / den
    f_im = (ni * lam_re - nr * lam_im) / den
    bb_re = f_re[..., None] * b_re - f_im[..., None] * b_im
    bb_im = f_re[..., None] * b_im + f_im[..., None] * b_re
    bu_re = jnp.einsum('blgc,gpc->blgp', u, bb_re)
    bu_im = jnp.einsum('blgc,gpc->blgp', u, bb_im)
    first = -1 if reverse else 0
    bu_re = bu_re.at[:, first].add(ab_re * h0_re - ab_im * h0_im)
    bu_im = bu_im.at[:, first].add(ab_re * h0_im + ab_im * h0_re)
    a_re = jnp.broadcast_to(ab_re, bu_re.shape)
    a_im = jnp.broadcast_to(ab_im, bu_im.shape)

    def combine(e1, e2):
        a1r, a1i, b1r, b1i = e1
        a2r, a2i, b2r, b2i = e2
        return (a2r * a1r - a2i * a1i, a2r * a1i + a2i * a1r,
                a2r * b1r - a2i * b1i + b2r, a2r * b1i + a2i * b1r + b2i)

    _, _, h_re, h_im = lax.associative_scan(combine, (a_re, a_im, bu_re, bu_im), reverse=reverse, axis=1)
    return h_re, h_im


def ssm_branch(u, p, h0, want_state):
    B, L, _ = u.shape
    uf = u.astype(F32).reshape(B, L, SSM_GROUPS, SSM_GROUP_CH)
    h0 = h0.astype(F32)
    y = uf * p['ssm_d'].astype(F32).reshape(SSM_GROUPS, SSM_GROUP_CH)
    finals = []
    for d in range(N_DIR):
        reverse = d == 1
        h_re, h_im = s5_direction(uf, p['ssm_lam_re'][d].astype(F32), p['ssm_lam_im'][d].astype(F32),
                                  p['ssm_log_step'][d].astype(F32), p['ssm_b_re'][d].astype(F32),
                                  p['ssm_b_im'][d].astype(F32), h0[:, d, 0], h0[:, d, 1], reverse)
        y = y + (jnp.einsum('blgp,gcp->blgc', h_re, p['ssm_c_re'][d].astype(F32))
                 - jnp.einsum('blgp,gcp->blgc', h_im, p['ssm_c_im'][d].astype(F32)))
        if want_state:
            last = 0 if reverse else -1
            finals.append(jnp.stack([h_re[:, last], h_im[:, last]], axis=1))
    z = jax.nn.gelu(y.reshape(B, L, SSM_WIDTH))
    out = z * jax.nn.sigmoid(z @ p['ssm_w_glu'].astype(F32) + p['ssm_b_glu'].astype(F32))
    return out, (jnp.stack(finals, axis=1) if want_state else None)


def centred_conv(x, w):
    K, L = w.shape[0], x.shape[1]
    half = K // 2
    xp = jnp.pad(x, ((0, 0), (half, K - 1 - half), (0, 0)))
    return sum(xp[:, j:j + L] * w[j] for j in range(K))


def gated_delta_chunked(q, k, v, g, beta, s0):
    B, L, H, _ = q.shape
    dv = v.shape[-1]
    C = GDN_CHUNK
    n = L // C

    def chunks(t):
        return t.reshape(B, n, C, H, -1).transpose(1, 0, 3, 2, 4)

    qc, kc, vc = chunks(q), chunks(k), chunks(v)
    gc = jnp.cumsum(g.reshape(B, n, C, H).transpose(1, 0, 3, 2), axis=-1)
    bc = beta.reshape(B, n, C, H).transpose(1, 0, 3, 2)[..., None]
    incl = jnp.tril(jnp.ones((C, C), bool))
    strict = jnp.tril(jnp.ones((C, C), bool), -1)
    decay = jnp.exp(jnp.where(incl, gc[..., :, None] - gc[..., None, :], -jnp.inf))
    kb = kc * bc
    lower = jnp.where(strict, jnp.einsum('nbhid,nbhjd->nbhij', kb, kc) * decay, 0.0)
    eye = jnp.broadcast_to(jnp.eye(C, dtype=F32), lower.shape)
    t_inv = lax.linalg.triangular_solve(eye + lower, eye, left_side=True, lower=True)
    u = t_inv @ (vc * bc)
    w = t_inv @ (kb * jnp.exp(gc)[..., None])

    def step(S, xs):
        q_i, k_i, u_i, w_i, g_i, d_i = xs
        v_new = u_i - jnp.einsum('bhck,bhkv->bhcv', w_i, S)
        attn = jnp.einsum('bhik,bhjk->bhij', q_i, k_i) * d_i
        o = jnp.einsum('bhck,bhkv->bhcv', q_i * jnp.exp(g_i)[..., None], S) + attn @ v_new
        g_last = g_i[..., -1]
        S = (S * jnp.exp(g_last)[..., None, None]
             + jnp.einsum('bhck,bhcv->bhkv', k_i * jnp.exp(g_last[..., None] - g_i)[..., None], v_new))
        return S, o

    s_final, o = lax.scan(step, s0, (qc, kc, u, w, gc, decay))
    return o.transpose(1, 0, 3, 2, 4).reshape(B, L, H, dv), s_final


def gdn_branch(gq, gk, gv, gz, ga, gb, p, s0):
    B, L, _ = gq.shape
    qkv = jax.nn.silu(centred_conv(jnp.concatenate([gq, gk, gv], axis=-1).astype(F32), p['gdn_conv_w'].astype(F32)))
    q = l2_normalize(qkv[..., :GDN_K_W].reshape(B, L, GDN_HEADS, GDN_DK)) * (GDN_DK ** -0.5)
    k = l2_normalize(qkv[..., GDN_K_W:2 * GDN_K_W].reshape(B, L, GDN_HEADS, GDN_DK))
    v = qkv[..., 2 * GDN_K_W:].reshape(B, L, GDN_HEADS, GDN_DV)
    a = ga.astype(F32).reshape(B, L, N_DIR, GDN_HEADS)
    b = gb.astype(F32).reshape(B, L, N_DIR, GDN_HEADS)
    a_log = p['gdn_a_log'].astype(F32)
    dt_bias = p['gdn_dt_bias'].astype(F32)
    s0 = s0.astype(F32)
    outs, finals = [], []
    for d in range(N_DIR):
        g = -jnp.exp(a_log[d]) * jax.nn.softplus(a[:, :, d] + dt_bias[d])
        beta = jax.nn.sigmoid(b[:, :, d])
        seqs = (q, k, v, g, beta)
        if d == 1:
            seqs = tuple(jnp.flip(t, axis=1) for t in seqs)
        o_d, s_d = gated_delta_chunked(*seqs, s0[:, d])
        if d == 1:
            o_d = jnp.flip(o_d, axis=1)
        outs.append(o_d)
        finals.append(s_d)
    o = rms_norm(outs[0] + outs[1], p['gdn_norm_g']) * jax.nn.silu(gz.astype(F32).reshape(B, L, GDN_HEADS, GDN_DV))
    return o.reshape(B, L, GDN_V_W), jnp.stack(finals, axis=1)


def trunk_layer(x, cond, p, ctx_kv, ssm_h0, gdn_s0):
    is_context = ctx_kv is None
    B, L, _ = x.shape
    mod = (jax.nn.silu(cond) @ p['w_mod'] + p['b_mod'])[:, None, :]
    sh_a, sc_a, g_a, sh_m, sc_m, g_m = jnp.split(mod, 6, axis=-1)
    h = rms_norm(x, p['norm1_g']) * (1.0 + sc_a) + sh_a
    q, k, v, u, gq, gk, gv, gz, ga, gb = split_columns(h @ p['w_in'])
    q = rms_norm(q.reshape(B, L, N_HEADS, HEAD_DIM), p['q_norm_g'])
    k = rms_norm(k.reshape(B, L, N_KV_HEADS, HEAD_DIM), p['k_norm_g'])
    v = v.reshape(B, L, N_KV_HEADS, HEAD_DIM)
    if is_context:
        attn = context_attention(q, k, v, p['attn_sink'])
    else:
        attn = latent_attention(axial_rope(q), axial_rope(k), v, ctx_kv[0], ctx_kv[1], p['attn_sink'])
    ssm_out, ssm_final = ssm_branch(u, p, ssm_h0, is_context)
    gdn_out, gdn_final = gdn_branch(gq, gk, gv, gz, ga, gb, p, gdn_s0)
    mixed = jnp.concatenate([attn.astype(x.dtype), ssm_out.astype(x.dtype), gdn_out.astype(x.dtype)], axis=-1) @ p['w_out']
    x = x + g_a * mixed
    h2 = rms_norm(x, p['norm2_g']) * (1.0 + sc_m) + sh_m
    x = x + g_m * (jnp.square(jax.nn.relu(h2 @ p['w_ff1'])) @ p['w_ff2'])
    if is_context:
        return x, (k, v, ssm_final, gdn_final)
    return x, None


def setup_inputs(seed: int = 0) -> dict:
    key = jax.random.key(seed)
    keys = iter(jax.random.split(key, 40))

    def normal(shape, scale):
        return scale * jax.random.normal(next(keys), shape, F32)

    def uniform(shape, lo, hi):
        return jax.random.uniform(next(keys), shape, F32, lo, hi)

    D = D_MODEL
    out = {}
    out['x_prompt'] = normal((BATCH, SEQ, D), 1.0)
    out['x_sample'] = normal((DEC_BATCH, DEC_SEQ, D), 1.0)
    out['c'] = normal((DEC_BATCH, D), 1.0)
    out['cache_k'] = normal((DEC_BATCH, DEPTH, PAST_LEN, N_KV_HEADS, HEAD_DIM), 1.0)
    out['cache_v'] = normal((DEC_BATCH, DEPTH, PAST_LEN, N_KV_HEADS, HEAD_DIM), 1.0)
    out['state_ssm'] = normal((DEC_BATCH, DEPTH, N_DIR, 2, SSM_GROUPS, SSM_STATE), 0.5)
    out['state_gdn'] = normal((DEC_BATCH, DEPTH, N_DIR, GDN_HEADS, GDN_DK, GDN_DV), 0.1)
    out['c_ctx'] = normal((D,), 1.0)
    out['norm1_g'] = 1.0 + normal((DEPTH, D), 0.01)
    out['norm2_g'] = 1.0 + normal((DEPTH, D), 0.01)
    out['w_mod'] = normal((DEPTH, D, 6 * D), 0.5 * D ** -0.5)
    out['b_mod'] = normal((DEPTH, 6 * D), 0.01)
    out['w_in'] = normal((DEPTH, D, IN_W), D ** -0.5)
    out['q_norm_g'] = 1.0 + normal((DEPTH, HEAD_DIM), 0.01)
    out['k_norm_g'] = 1.0 + normal((DEPTH, HEAD_DIM), 0.01)
    out['attn_sink'] = normal((DEPTH, N_HEADS), 0.5)
    out['ssm_lam_re'] = -0.5 + normal((DEPTH, N_DIR, SSM_GROUPS, SSM_STATE), 0.01)
    out['ssm_lam_im'] = jnp.pi * jnp.arange(SSM_STATE, dtype=F32) + normal((DEPTH, N_DIR, SSM_GROUPS, SSM_STATE), 0.01)
    out['ssm_log_step'] = uniform((DEPTH, N_DIR, SSM_GROUPS), math.log(1e-3), math.log(1e-1))
    out['ssm_b_re'] = normal((DEPTH, N_DIR, SSM_GROUPS, SSM_STATE, SSM_GROUP_CH), (0.5 / SSM_GROUP_CH) ** 0.5)
    out['ssm_b_im'] = normal((DEPTH, N_DIR, SSM_GROUPS, SSM_STATE, SSM_GROUP_CH), (0.5 / SSM_GROUP_CH) ** 0.5)
    out['ssm_c_re'] = normal((DEPTH, N_DIR, SSM_GROUPS, SSM_GROUP_CH, SSM_STATE), (0.5 / SSM_STATE) ** 0.5)
    out['ssm_c_im'] = normal((DEPTH, N_DIR, SSM_GROUPS, SSM_GROUP_CH, SSM_STATE), (0.5 / SSM_STATE) ** 0.5)
    out['ssm_d'] = normal((DEPTH, SSM_WIDTH), 1.0)
    out['ssm_w_glu'] = normal((DEPTH, SSM_WIDTH, SSM_WIDTH), SSM_WIDTH ** -0.5)
    out['ssm_b_glu'] = normal((DEPTH, SSM_WIDTH), 0.01)
    out['gdn_conv_w'] = normal((DEPTH, GDN_CONV, 2 * GDN_K_W + GDN_V_W), GDN_CONV ** -0.5)
    out['gdn_a_log'] = jnp.log(uniform((DEPTH, N_DIR, GDN_HEADS), 1.0, 16.0))
    dt = jnp.exp(uniform((DEPTH, N_DIR, GDN_HEADS), math.log(1e-3), math.log(1e-1)))
    out['gdn_dt_bias'] = dt + jnp.log(-jnp.expm1(-dt))
    out['gdn_norm_g'] = 1.0 + normal((DEPTH, GDN_DV), 0.01)
    out['w_out'] = normal((DEPTH, MIX_W, D), MIX_W ** -0.5)
    out['w_ff1'] = normal((DEPTH, D, D_FF), D ** -0.5)
    out['w_ff2'] = normal((DEPTH, D_FF, D), D_FF ** -0.5)
    return out


def reference(x_prompt, x_sample, c, cache_k, cache_v, state_ssm, state_gdn, c_ctx,
              norm1_g, norm2_g, w_mod, b_mod, w_in, q_norm_g, k_norm_g, attn_sink,
              ssm_lam_re, ssm_lam_im, ssm_log_step, ssm_b_re, ssm_b_im, ssm_c_re, ssm_c_im,
              ssm_d, ssm_w_glu, ssm_b_glu, gdn_conv_w, gdn_a_log, gdn_dt_bias, gdn_norm_g,
              w_out, w_ff1, w_ff2):
    def layer_params(l):
        return {'norm1_g': norm1_g[l], 'norm2_g': norm2_g[l], 'w_mod': w_mod[l], 'b_mod': b_mod[l],
                'w_in': w_in[l], 'q_norm_g': q_norm_g[l], 'k_norm_g': k_norm_g[l], 'attn_sink': attn_sink[l],
                'ssm_lam_re': ssm_lam_re[l], 'ssm_lam_im': ssm_lam_im[l], 'ssm_log_step': ssm_log_step[l],
                'ssm_b_re': ssm_b_re[l], 'ssm_b_im': ssm_b_im[l], 'ssm_c_re': ssm_c_re[l], 'ssm_c_im': ssm_c_im[l],
                'ssm_d': ssm_d[l], 'ssm_w_glu': ssm_w_glu[l], 'ssm_b_glu': ssm_b_glu[l],
                'gdn_conv_w': gdn_conv_w[l], 'gdn_a_log': gdn_a_log[l], 'gdn_dt_bias': gdn_dt_bias[l],
                'gdn_norm_g': gdn_norm_g[l], 'w_out': w_out[l], 'w_ff1': w_ff1[l], 'w_ff2': w_ff2[l]}

    n_ctx_req = x_prompt.shape[0]
    ssm_zero = jnp.zeros((n_ctx_req, N_DIR, 2, SSM_GROUPS, SSM_STATE), F32)
    gdn_zero = jnp.zeros((n_ctx_req, N_DIR, GDN_HEADS, GDN_DK, GDN_DV), F32)
    cond_ctx = c_ctx[None, :]
    xp = x_prompt
    ks, vs, ss, gs = [], [], [], []
    for l in range(DEPTH):
        xp, (k_l, v_l, s_l, g_l) = trunk_layer(xp, cond_ctx, layer_params(l), None, ssm_zero, gdn_zero)
        ks.append(k_l)
        vs.append(v_l)
        ss.append(s_l)
        gs.append(g_l)
    y_prompt = xp
    new_cache_k = jnp.stack(ks, axis=1)
    new_cache_v = jnp.stack(vs, axis=1)
    new_state_ssm = jnp.stack(ss, axis=1)
    new_state_gdn = jnp.stack(gs, axis=1)

    xs = x_sample
    for l in range(DEPTH):
        xs, _ = trunk_layer(xs, c, layer_params(l), (cache_k[:, l], cache_v[:, l]), state_ssm[:, l], state_gdn[:, l])
    y_sample = xs
    return (y_prompt, y_sample, new_cache_k, new_cache_v, new_state_ssm, new_state_gdn)
```

```python
import functools
import math

import jax
import jax.numpy as jnp
from jax import lax
from jax.experimental import pallas as pl
from jax.experimental.pallas import tpu as pltpu

F32 = jnp.float32
BF16 = jnp.bfloat16
EPS = 1e-6
NEG_INF = -1e30

D_MODEL = 1024
DEPTH = 4
GRID_W = 64
N_DIR = 2
N_HEADS = 8
N_KV_HEADS = 2
GQA = N_HEADS // N_KV_HEADS
HEAD_DIM = 64
ATT_BLOCK = 128
ROPE_BASE = 10000.0
SSM_GROUP_CH = 16
SSM_GROUPS = 16
SSM_WIDTH = SSM_GROUPS * SSM_GROUP_CH
SSM_STATE = 64
GDN_HEADS = 4
GDN_DK = 64
GDN_DV = 64
GDN_K_W = GDN_HEADS * GDN_DK
GDN_V_W = GDN_HEADS * GDN_DV
GDN_QKV_W = 2 * GDN_K_W + GDN_V_W
GDN_CHUNK = 64
ATT_Q_W = N_HEADS * HEAD_DIM
ATT_KV_W = N_KV_HEADS * HEAD_DIM
D_FF = 4 * D_MODEL
N_MOD = 6
GATE_W = 2 * N_DIR * GDN_HEADS

LANES = 128
SUBLANES = 8
SSM_T = LANES
VMEM_LIMIT = 56 * 1024 * 1024

_NT = (((1,), (1,)), ((), ()))
_TN = (((0,), (0,)), ((), ()))


def _mm(a, b):
    return jnp.dot(a.astype(BF16), b.astype(BF16), preferred_element_type=F32)


def _mm_nt(a, b):
    return lax.dot_general(a.astype(BF16), b.astype(BF16), _NT, preferred_element_type=F32)


def _mm_tn(a, b):
    return lax.dot_general(a.astype(BF16), b.astype(BF16), _TN, preferred_element_type=F32)


def _split3(a):
    hi = a.astype(BF16)
    r1 = a - hi.astype(F32)
    mid = r1.astype(BF16)
    lo = (r1 - mid.astype(F32)).astype(BF16)
    return hi, mid, lo


def _mm_exact_rhs(a, b_bf16):
    hi, mid, lo = _split3(a)
    return (jnp.dot(hi, b_bf16, preferred_element_type=F32)
            + jnp.dot(mid, b_bf16, preferred_element_type=F32)
            + jnp.dot(lo, b_bf16, preferred_element_type=F32))


def _mm_f32(a, b):
    ah, am, al = _split3(a)
    bh, bm, bl = _split3(b)
    d = lambda x, y: jnp.dot(x, y, preferred_element_type=F32)
    return (d(ah, bh) + (d(ah, bm) + d(am, bh)) + (d(ah, bl) + d(am, bm) + d(al, bh)))


def _silu(x):
    return x * jax.nn.sigmoid(x)


def _softplus(x):
    return jnp.maximum(x, 0.0) + jnp.log1p(jnp.exp(-jnp.abs(x)))


def _params(*sem):
    return pltpu.CompilerParams(dimension_semantics=sem, vmem_limit_bytes=VMEM_LIMIT)


def _mod_kernel(cond_ref, w_ref, b_ref, o_ref):
    c = cond_ref[...]
    o_ref[0] = _mm(_silu(c), w_ref[0]) + b_ref[0]


def _modulation(cond, w_mod, b_mod):
    rows = cond.shape[0]
    cn = 1536
    return pl.pallas_call(
        _mod_kernel,
        grid=(DEPTH, N_MOD * D_MODEL // cn),
        in_specs=[pl.BlockSpec((rows, D_MODEL), lambda l, j: (0, 0)),
                  pl.BlockSpec((1, D_MODEL, cn), lambda l, j: (l, 0, j)),
                  pl.BlockSpec((1, 1, cn), lambda l, j: (l, 0, j))],
        out_specs=pl.BlockSpec((1, rows, cn), lambda l, j: (l, 0, j)),
        out_shape=jax.ShapeDtypeStruct((DEPTH, rows, N_MOD * D_MODEL), F32),
        compiler_params=_params("arbitrary", "arbitrary"),
        name="modulation",
    )(cond, w_mod, b_mod.reshape(DEPTH, 1, N_MOD * D_MODEL))


def _seg_mean_sq(x, bd_ref):
    xx = x * x
    hi = xx.astype(BF16)
    lo = (xx - hi.astype(F32)).astype(BF16)
    n = x.shape[-1]
    bd = bd_ref[:n, :n]
    return jnp.dot(hi, bd, preferred_element_type=F32) + jnp.dot(lo, bd, preferred_element_type=F32)


def _rope(x, cos, sin_signed):
    n = x.shape[-1]
    nxt = pltpu.roll(x, n - 1, 1)
    prv = pltpu.roll(x, 1, 1)
    lane = lax.broadcasted_iota(jnp.int32, x.shape, 1)
    swapped = jnp.where(lane % 2 == 0, nxt, prv)
    return x * cos + swapped * sin_signed


def _in_kernel(x_ref, sh_ref, sc_ref, g1_ref, wq_ref, wkv_ref, wut_ref, wg_ref, wz_ref, wab_ref, wabt_ref,
               qg_ref, kg_ref, bd_ref, cos_ref, sin_ref,
               q_out, k_out, v_out, ut_out, g_out, z_out, ab_out, abt_out, *, rope):
    x = x_ref[0]
    ms = jnp.mean(x * x, axis=-1, keepdims=True)
    h = (x * lax.rsqrt(ms + EPS) * g1_ref[...]) * (1.0 + sc_ref[0]) + sh_ref[0]
    hb = h.astype(BF16)

    q = jnp.dot(hb, wq_ref[...], preferred_element_type=F32)
    q = q * lax.rsqrt(_seg_mean_sq(q, bd_ref) + EPS) * qg_ref[...]
    kv = jnp.dot(hb, wkv_ref[...], preferred_element_type=F32)
    k = kv[:, :ATT_KV_W]
    k = k * lax.rsqrt(_seg_mean_sq(k, bd_ref) + EPS) * kg_ref[...]
    if rope:
        cos = cos_ref[...]
        sin = sin_ref[...]
        q = _rope(q, jnp.concatenate([cos] * (ATT_Q_W // LANES), axis=1),
                  jnp.concatenate([sin] * (ATT_Q_W // LANES), axis=1))
        k = _rope(k, cos, sin)
    q_out[0] = (q * (HEAD_DIM ** -0.5)).astype(BF16)
    k_out[0] = k
    v_out[0] = kv[:, ATT_KV_W:]
    ut_out[...] = lax.dot_general(wut_ref[...], hb, _NT, preferred_element_type=F32)
    g_out[0] = jnp.dot(hb, wg_ref[...], preferred_element_type=F32)
    z_out[0] = jnp.dot(hb, wz_ref[...], preferred_element_type=F32)
    ab_out[0] = jnp.dot(hb, wab_ref[...], preferred_element_type=F32)
    abt_out[0] = lax.dot_general(wabt_ref[...], hb, _NT, preferred_element_type=F32)


def _in_projection(x, mod3, mod_row, p, consts, *, rope, tm):
    bg, lg, _ = x.shape
    nt = lg // tm
    row = mod_row
    full = lambda shape: pl.BlockSpec(shape, lambda t, b: (0,) * len(shape))
    tok = lambda w: pl.BlockSpec((1, tm, w), lambda t, b: (b, t, 0))
    in_specs = [
        tok(D_MODEL),
        pl.BlockSpec((1, 1, D_MODEL), lambda t, b: (row(b), 0, 0)),
        pl.BlockSpec((1, 1, D_MODEL), lambda t, b: (row(b), 0, 1)),
        full((1, D_MODEL)),
        full((D_MODEL, ATT_Q_W)), full((D_MODEL, 2 * ATT_KV_W)), full((SSM_WIDTH, D_MODEL)),
        full((D_MODEL, GDN_QKV_W)), full((D_MODEL, GDN_V_W)), full((D_MODEL, LANES)), full((GATE_W, D_MODEL)),
        full((1, ATT_Q_W)), full((1, ATT_KV_W)), full((ATT_Q_W, ATT_Q_W)),
        pl.BlockSpec((tm, LANES), lambda t, b: (t, 0)),
        pl.BlockSpec((tm, LANES), lambda t, b: (t, 0)),
    ]
    out_specs = [
        tok(ATT_Q_W), tok(ATT_KV_W), tok(ATT_KV_W),
        pl.BlockSpec((SSM_WIDTH, tm), lambda t, b: (0, b * nt + t)),
        tok(GDN_QKV_W), tok(GDN_V_W), tok(LANES),
        pl.BlockSpec((1, GATE_W, tm), lambda t, b: (b, 0, t)),
    ]
    out_shape = [
        jax.ShapeDtypeStruct((bg, lg, ATT_Q_W), BF16),
        jax.ShapeDtypeStruct((bg, lg, ATT_KV_W), F32),
        jax.ShapeDtypeStruct((bg, lg, ATT_KV_W), F32),
        jax.ShapeDtypeStruct((SSM_WIDTH, bg * lg), F32),
        jax.ShapeDtypeStruct((bg, lg, GDN_QKV_W), F32),
        jax.ShapeDtypeStruct((bg, lg, GDN_V_W), F32),
        jax.ShapeDtypeStruct((bg, lg, LANES), F32),
        jax.ShapeDtypeStruct((bg, GATE_W, lg), F32),
    ]
    return pl.pallas_call(
        functools.partial(_in_kernel, rope=rope),
        grid=(nt, bg), in_specs=in_specs, out_specs=out_specs, out_shape=out_shape,
        compiler_params=_params("arbitrary", "arbitrary"),
        name="in_projection",
    )(x, mod3, mod3, p["norm1_g"], p["wq"], p["wkv"], p["wut"], p["wg"], p["wz"], p["wab"], p["wabt"],
      p["q_norm_g"], p["k_norm_g"], consts["bd_mean"], consts["cos"][:lg], consts["sin"][:lg])


def _attend(q_ref, pieces, sink_ref, o_ref):
    for h in range(N_HEADS):
        j = h // GQA
        qh = q_ref[0, :, h * HEAD_DIM:(h + 1) * HEAD_DIM]
        sink = sink_ref[h]
        scores = []
        for k, _, mask in pieces:
            s = _mm_nt(qh, k[:, j * HEAD_DIM:(j + 1) * HEAD_DIM])
            if mask is not None:
                s = jnp.where(mask, s, NEG_INF)
            scores.append(s)
        m = jnp.maximum(functools.reduce(jnp.maximum, [jnp.max(s, axis=-1, keepdims=True) for s in scores]), sink)
        den = jnp.exp(sink - m)
        acc = None
        for s, (_, v, _) in zip(scores, pieces):
            pr = jnp.exp(s - m)
            den = den + jnp.sum(pr, axis=-1, keepdims=True)
            pv = _mm(pr, v[:, j * HEAD_DIM:(j + 1) * HEAD_DIM])
            acc = pv if acc is None else acc + pv
        o_ref[0, :, h * HEAD_DIM:(h + 1) * HEAD_DIM] = (acc / den).astype(o_ref.dtype)


def _latent_attn_kernel(q_ref, kp_ref, kc_ref, kn_ref, vp_ref, vc_ref, vn_ref, ck_ref, cv_ref, sink_ref, o_ref, *, nb):
    i = pl.program_id(1)
    r = lax.broadcasted_iota(jnp.int32, (ATT_BLOCK, ATT_BLOCK), 0)
    c = lax.broadcasted_iota(jnp.int32, (ATT_BLOCK, ATT_BLOCK), 1)
    mask_prev = jnp.logical_and(c >= r, i > 0)
    mask_next = jnp.logical_and(c <= r, i < nb - 1)
    pieces = [(kp_ref[0], vp_ref[0], mask_prev), (kc_ref[0], vc_ref[0], None), (kn_ref[0], vn_ref[0], mask_next),
              (ck_ref[0, 0], cv_ref[0, 0], None)]
    _attend(q_ref, pieces, sink_ref, o_ref)


def _latent_attention(q, k, v, cache_k4, cache_v4, layer, sink):
    bg, lg, _ = q.shape
    nb = lg // ATT_BLOCK
    past = cache_k4.shape[2]
    blk = lambda w, f: pl.BlockSpec((1, ATT_BLOCK, w), f)
    prev = lambda b, i: (b, jnp.maximum(i - 1, 0), 0)
    cur = lambda b, i: (b, i, 0)
    nxt = lambda b, i: (b, jnp.minimum(i + 1, nb - 1), 0)
    ctx = pl.BlockSpec((1, 1, past, ATT_KV_W), lambda b, i: (b, layer, 0, 0))
    return pl.pallas_call(
        functools.partial(_latent_attn_kernel, nb=nb),
        grid=(bg, nb),
        in_specs=[blk(ATT_Q_W, cur), blk(ATT_KV_W, prev), blk(ATT_KV_W, cur), blk(ATT_KV_W, nxt),
                  blk(ATT_KV_W, prev), blk(ATT_KV_W, cur), blk(ATT_KV_W, nxt), ctx, ctx,
                  pl.BlockSpec(memory_space=pltpu.SMEM)],
        out_specs=blk(ATT_Q_W, cur),
        out_shape=jax.ShapeDtypeStruct((bg, lg, ATT_Q_W), BF16),
        compiler_params=_params("arbitrary", "arbitrary"),
        name="latent_attention",
    )(q, k, k, k, v, v, v, cache_k4, cache_v4, sink)


def _context_attn_kernel(q_ref, k_ref, v_ref, sink_ref, o_ref):
    _attend(q_ref, [(k_ref[0], v_ref[0], None)], sink_ref, o_ref)


def _context_attention(q, k, v, sink):
    bg, lg, _ = q.shape
    nb = lg // ATT_BLOCK
    return pl.pallas_call(
        _context_attn_kernel,
        grid=(bg, nb),
        in_specs=[pl.BlockSpec((1, ATT_BLOCK, ATT_Q_W), lambda b, i: (b, i, 0)),
                  pl.BlockSpec((1, lg, ATT_KV_W), lambda b, i: (b, 0, 0)),
                  pl.BlockSpec((1, lg, ATT_KV_W), lambda b, i: (b, 0, 0)),
                  pl.BlockSpec(memory_space=pltpu.SMEM)],
        out_specs=pl.BlockSpec((1, ATT_BLOCK, ATT_Q_W), lambda b, i: (b, i, 0)),
        out_shape=jax.ShapeDtypeStruct((bg, lg, ATT_Q_W), BF16),
        compiler_params=_params("arbitrary", "arbitrary"),
        name="context_attention",
    )(q, k, v, sink)


def _powers(er, th, e):
    mag = jnp.exp(er * e)
    return mag * jnp.cos(th * e), mag * jnp.sin(th * e)


def _ssm_prep_kernel(pc_ref, pr_ref, br_ref, bi_ref, cr_ref, ci_ref, crt_ref, cit_ref,
                     kv_out, rs_out, f_out, at_out):
    t = SSM_T
    p = SSM_STATE
    lane2 = lax.broadcasted_iota(jnp.int32, (1, 2 * t), 1)
    lane1 = lax.broadcasted_iota(jnp.int32, (1, t), 1)
    e_kv = [jnp.maximum(lane2 - t, 0).astype(F32), jnp.maximum(t - lane2, 0).astype(F32)]
    m_kv = [lane2 >= t, jnp.logical_and(lane2 >= 1, lane2 <= t)]
    e_rs = [(t - 1 - lane1).astype(F32), lane1.astype(F32)]
    e_f = [(lane1 + 1).astype(F32), (t - lane1).astype(F32)]

    bbr, bbi, pk, prs, pf = [], [], [], [], []
    for d in range(N_DIR):
        lr = pc_ref[d, 0, :, 0:1]
        li = pc_ref[d, 0, :, 1:2]
        step = jnp.exp(pc_ref[d, 0, :, 2:3])
        er = lr * step
        th = li * step
        mag = jnp.exp(er)
        ar = mag * jnp.cos(th)
        ai = mag * jnp.sin(th)
        den = lr * lr + li * li
        nr = ar - 1.0
        fr = (nr * lr + ai * li) / den
        fi = (ai * lr - nr * li) / den
        b_re = br_ref[d, 0]
        b_im = bi_ref[d, 0]
        bbr.append(fr * b_re - fi * b_im)
        bbi.append(fr * b_im + fi * b_re)
        kr, ki = _powers(er, th, e_kv[d])
        pk.append((jnp.where(m_kv[d], kr, 0.0), jnp.where(m_kv[d], ki, 0.0)))
        prs.append(_powers(er, th, e_rs[d]))
        pf.append(_powers(er, th, e_f[d]))
        lr_r = pr_ref[d, 0, 0:1, :]
        li_r = pr_ref[d, 0, 1:2, :]
        step_r = jnp.exp(pr_ref[d, 0, 2:3, :])
        mag_t = jnp.exp(lr_r * step_r * float(t))
        at_out[0, :, d * p:(d + 1) * p] = mag_t * jnp.cos(li_r * step_r * float(t))
        at_out[0, :, (N_DIR + d) * p:(N_DIR + d + 1) * p] = mag_t * jnp.sin(li_r * step_r * float(t))

    cmat = jnp.concatenate([cr_ref[0, 0], -ci_ref[0, 0], cr_ref[1, 0], -ci_ref[1, 0]], axis=1)
    for ci in range(SSM_GROUP_CH):
        rows = []
        for d in range(N_DIR):
            cbr = bbr[d][:, ci:ci + 1]
            cbi = bbi[d][:, ci:ci + 1]
            kr, ki = pk[d]
            rows += [kr * cbr - ki * cbi, kr * cbi + ki * cbr]
        kv_out[0, ci * SSM_GROUP_CH:(ci + 1) * SSM_GROUP_CH, :] = _mm_f32(cmat, jnp.concatenate(rows, axis=0))
        for d in range(N_DIR):
            cbr = bbr[d][:, ci:ci + 1]
            cbi = bbi[d][:, ci:ci + 1]
            sr, si = prs[d]
            rs_out[0, ci, d * p:(d + 1) * p, :] = (sr * cbr - si * cbi).astype(BF16)
            rs_out[0, ci, (N_DIR + d) * p:(N_DIR + d + 1) * p, :] = (sr * cbi + si * cbr).astype(BF16)
    for co in range(SSM_GROUP_CH):
        for d in range(N_DIR):
            ccr = crt_ref[d, 0, :, co:co + 1]
            cci = cit_ref[d, 0, :, co:co + 1]
            fr_, fi_ = pf[d]
            f_out[0, d * p:(d + 1) * p, co * t:(co + 1) * t] = (ccr * fr_ - cci * fi_).astype(BF16)
            f_out[0, (N_DIR + d) * p:(N_DIR + d + 1) * p, co * t:(co + 1) * t] = (-(ccr * fi_ + cci * fr_)).astype(BF16)


def _ssm_prep(p):
    t = SSM_T
    g = SSM_GROUPS
    ns = N_DIR * 2 * SSM_STATE
    spec4 = lambda a, b: pl.BlockSpec((N_DIR, 1, a, b), lambda i: (0, i, 0, 0))
    return pl.pallas_call(
        _ssm_prep_kernel,
        grid=(g,),
        in_specs=[spec4(SSM_STATE, 3), spec4(3, SSM_STATE),
                  spec4(SSM_STATE, SSM_GROUP_CH), spec4(SSM_STATE, SSM_GROUP_CH),
                  spec4(SSM_GROUP_CH, SSM_STATE), spec4(SSM_GROUP_CH, SSM_STATE),
                  spec4(SSM_STATE, SSM_GROUP_CH), spec4(SSM_STATE, SSM_GROUP_CH)],
        out_specs=[pl.BlockSpec((1, SSM_GROUP_CH * SSM_GROUP_CH, 2 * t), lambda i: (i, 0, 0)),
                   pl.BlockSpec((1, SSM_GROUP_CH, ns, t), lambda i: (i, 0, 0, 0)),
                   pl.BlockSpec((1, ns, SSM_GROUP_CH * t), lambda i: (i, 0, 0)),
                   pl.BlockSpec((1, 1, ns), lambda i: (i, 0, 0))],
        out_shape=[jax.ShapeDtypeStruct((g, SSM_GROUP_CH * SSM_GROUP_CH, 2 * t), F32),
                   jax.ShapeDtypeStruct((g, SSM_GROUP_CH, ns, t), BF16),
                   jax.ShapeDtypeStruct((g, ns, SSM_GROUP_CH * t), BF16),
                   jax.ShapeDtypeStruct((g, 1, ns), F32)],
        compiler_params=_params("arbitrary"),
        name="ssm_prep",
    )(p["ssm_pc"], p["ssm_pr"], p["ssm_b_re"], p["ssm_b_im"], p["ssm_c_re"], p["ssm_c_im"],
      p["ssm_c_re_t"], p["ssm_c_im_t"])


def _ssm_kernel(u_ref, kv_ref, rs_ref, f_ref, at_ref, h0_ref, d_ref, y_out, hfin_out,
                m_scr, s_scr, hpf_scr, hpb_scr, *, bg, nc):
    t = SSM_T
    nch = SSM_GROUP_CH
    g = pl.program_id(0)
    half = N_DIR * SSM_STATE

    def gen(ci, carry):
        for co in range(nch):
            row = kv_ref[0, pl.ds(ci * nch + co, 1), :]
            rolled = pltpu.roll(jnp.broadcast_to(row, (t, 2 * t)), t, 1, stride=1, stride_axis=0)
            m_scr[ci, :, co * t:(co + 1) * t] = rolled[:, :t].astype(BF16)
        return carry
    lax.fori_loop(0, nch, gen, 0)

    s = None
    for ci in range(nch):
        part = lax.dot_general(u_ref[0, ci].astype(BF16), rs_ref[0, ci], _NT, preferred_element_type=F32)
        s = part if s is None else s + part
    s_scr[0] = s[:, :half]
    s_scr[1] = s[:, half:]

    a_re = at_ref[0, :, :half]
    a_im = at_ref[0, :, half:]

    def advance(re, im, rows):
        return (a_re * re - a_im * im + s_scr[0, rows, :], a_re * im + a_im * re + s_scr[1, rows, :])

    def scan(i, carry):
        f_re, f_im, b_re, b_im = carry
        rows_f = pl.ds(i, bg, stride=nc)
        rows_b = pl.ds(nc - 1 - i, bg, stride=nc)
        hpf_scr[0, rows_f, :] = f_re
        hpf_scr[1, rows_f, :] = f_im
        hpb_scr[0, rows_b, :] = b_re
        hpb_scr[1, rows_b, :] = b_im
        return advance(f_re, f_im, rows_f) + advance(b_re, b_im, rows_b)
    h0_re = h0_ref[0, :, :half]
    h0_im = h0_ref[0, :, half:]
    f_re, f_im, b_re, b_im = lax.fori_loop(0, nc, scan, (h0_re, h0_im, h0_re, h0_im))
    is_fwd = lax.broadcasted_iota(jnp.int32, (1, half), 1) < SSM_STATE
    hfin_out[0, :, :half] = jnp.where(is_fwd, f_re, b_re)
    hfin_out[0, :, half:] = jnp.where(is_fwd, f_im, b_im)
    hprev = jnp.concatenate([jnp.where(is_fwd, hpf_scr[0], hpb_scr[0]),
                             jnp.where(is_fwd, hpf_scr[1], hpb_scr[1])], axis=1)

    y = jnp.dot(hprev.astype(BF16), f_ref[0], preferred_element_type=F32)
    for ci in range(nch):
        y = y + jnp.dot(u_ref[0, ci].astype(BF16), m_scr[ci], preferred_element_type=F32)
    for co in range(nch):
        y_out[0, co] = y[:, co * t:(co + 1) * t] + d_ref[g * nch + co] * u_ref[0, co]


def _ssm_mix(ut, tables, h0, d_skip, *, bg, lg):
    t = SSM_T
    g = SSM_GROUPS
    nch = SSM_GROUP_CH
    nc = lg // t
    nb = bg * nc
    ns = N_DIR * 2 * SSM_STATE
    kv, rs, f, at = tables
    u4 = ut.reshape(g, nch, nb, t)
    y, hfin = pl.pallas_call(
        functools.partial(_ssm_kernel, bg=bg, nc=nc),
        grid=(g,),
        in_specs=[pl.BlockSpec((1, nch, nb, t), lambda i: (i, 0, 0, 0)),
                  pl.BlockSpec((1, nch * nch, 2 * t), lambda i: (i, 0, 0)),
                  pl.BlockSpec((1, nch, ns, t), lambda i: (i, 0, 0, 0)),
                  pl.BlockSpec((1, ns, nch * t), lambda i: (i, 0, 0)),
                  pl.BlockSpec((1, 1, ns), lambda i: (i, 0, 0)),
                  pl.BlockSpec((1, bg, ns), lambda i: (i, 0, 0)),
                  pl.BlockSpec(memory_space=pltpu.SMEM)],
        out_specs=[pl.BlockSpec((1, nch, nb, t), lambda i: (i, 0, 0, 0)),
                   pl.BlockSpec((1, bg, ns), lambda i: (i, 0, 0))],
        out_shape=[jax.ShapeDtypeStruct((g, nch, nb, t), F32),
                   jax.ShapeDtypeStruct((g, bg, ns), F32)],
        scratch_shapes=[pltpu.VMEM((nch, t, nch * t), BF16),
                        pltpu.VMEM((2, nb, ns // 2), F32), pltpu.VMEM((2, nb, ns // 2), F32),
                        pltpu.VMEM((2, nb, ns // 2), F32)],
        compiler_params=_params("arbitrary"),
        name="ssm_mix",
    )(u4, kv, rs, f, at, h0, d_skip)
    return y.reshape(g * nch, bg * lg), hfin


def _gdn_pre_kernel(x_ref, xp_ref, xn_ref, w_ref, bd_ref, o_ref, *, nt):
    i = pl.program_id(1)
    x = x_ref[0]
    tt = x.shape[0]
    row = lax.broadcasted_iota(jnp.int32, x.shape, 0)
    prev_row = jnp.where(i > 0, xp_ref[0, SUBLANES - 1:SUBLANES, :], 0.0)
    next_row = jnp.where(i < nt - 1, xn_ref[0, 0:1, :], 0.0)
    x_m1 = jnp.where(row == 0, prev_row, pltpu.roll(x, 1, 0))
    x_p1 = jnp.where(row == tt - 1, next_row, pltpu.roll(x, tt - 1, 0))
    y = _silu(x_m1 * w_ref[0:1, :] + x * w_ref[1:2, :] + x_p1 * w_ref[2:3, :])
    q = y[:, :GDN_K_W]
    k = y[:, GDN_K_W:2 * GDN_K_W]
    o_ref[0, :, :GDN_K_W] = q * lax.rsqrt(_seg_mean_sq(q, bd_ref) * GDN_DK + EPS) * (GDN_DK ** -0.5)
    o_ref[0, :, GDN_K_W:2 * GDN_K_W] = k * lax.rsqrt(_seg_mean_sq(k, bd_ref) * GDN_DK + EPS)
    o_ref[0, :, 2 * GDN_K_W:] = y[:, 2 * GDN_K_W:]


def _gdn_pre(gqkv, conv_w, bd_sum, *, tt):
    bg, lg, w = gqkv.shape
    nt = lg // tt
    hb = tt // SUBLANES
    return pl.pallas_call(
        functools.partial(_gdn_pre_kernel, nt=nt),
        grid=(bg, nt),
        in_specs=[pl.BlockSpec((1, tt, w), lambda b, i: (b, i, 0)),
                  pl.BlockSpec((1, SUBLANES, w), lambda b, i: (b, jnp.maximum(i * hb - 1, 0), 0)),
                  pl.BlockSpec((1, SUBLANES, w), lambda b, i: (b, jnp.minimum((i + 1) * hb, lg // SUBLANES - 1), 0)),
                  pl.BlockSpec((3, w), lambda b, i: (0, 0)),
                  pl.BlockSpec((GDN_K_W, GDN_K_W), lambda b, i: (0, 0))],
        out_specs=pl.BlockSpec((1, tt, w), lambda b, i: (b, i, 0)),
        out_shape=jax.ShapeDtypeStruct((bg, lg, w), F32),
        compiler_params=_params("arbitrary", "arbitrary"),
        name="gdn_pre",
    )(gqkv, gqkv, gqkv, conv_w, bd_sum)


def _tri_inverse(a, same16, same32):
    n = a.shape[0]
    r = lax.broadcasted_iota(jnp.int32, (n, n), 0)
    c = lax.broadcasted_iota(jnp.int32, (n, n), 1)
    eye = (r == c).astype(F32)
    neg = jnp.where(same16, -a, 0.0)
    x = eye + neg
    pw = neg
    for _ in range(3):
        pw = _mm(pw, pw)
        x = x + _mm(x, pw)
    off32 = jnp.where(jnp.logical_and(same32, jnp.logical_not(same16)), a, 0.0)
    x = x - _mm(x, _mm(off32, x))
    off64 = jnp.where(same32, 0.0, a)
    x = x - _mm(x, _mm(off64, x))
    return x


def _gdn_kernel(xf_ref, xb_ref, abf_ref, abb_ref, abtf_ref, abtb_ref, prow_ref, pcol_ref, s0_ref,
                of_ref, ob_ref, sfin_ref, s_scr, *, nt, tt):
    n = pl.program_id(1)
    ck = GDN_CHUNK
    cpt = tt // ck

    @pl.when(n == 0)
    def _():
        s_scr[...] = s0_ref[0]

    r = lax.broadcasted_iota(jnp.int32, (ck, ck), 0)
    c = lax.broadcasted_iota(jnp.int32, (ck, ck), 1)
    same16 = (r // 16) == (c // 16)
    same32 = (r // 32) == (c // 32)
    row_in = lax.broadcasted_iota(jnp.int32, (tt, LANES), 0) % ck
    lane_in = lax.broadcasted_iota(jnp.int32, (GATE_W, tt), 1) % ck

    for d, (x_ref, ab_ref, abt_ref, o_ref) in enumerate(((xf_ref, abf_ref, abtf_ref, of_ref),
                                                          (xb_ref, abb_ref, abtb_ref, ob_ref))):
        rev = d == 1
        ab = ab_ref[0]
        g_col = -jnp.exp(prow_ref[0:1, :]) * _softplus(ab + prow_ref[1:2, :])
        beta_col = jax.nn.sigmoid(ab)
        abt = abt_ref[0]
        g_row = -jnp.exp(pcol_ref[:, 0:1]) * _softplus(abt + pcol_ref[:, 1:2])
        sh = 1
        while sh < ck:
            if rev:
                g_col = g_col + jnp.where(row_in < ck - sh, pltpu.roll(g_col, tt - sh, 0), 0.0)
                g_row = g_row + jnp.where(lane_in < ck - sh, pltpu.roll(g_row, tt - sh, 1), 0.0)
            else:
                g_col = g_col + jnp.where(row_in >= sh, pltpu.roll(g_col, sh, 0), 0.0)
                g_row = g_row + jnp.where(lane_in >= sh, pltpu.roll(g_row, sh, 1), 0.0)
            sh *= 2
        incl = (r <= c) if rev else (r >= c)
        strict = (r < c) if rev else (r > c)
        last = 0 if rev else ck - 1

        for cc in (range(cpt - 1, -1, -1) if rev else range(cpt)):
            c0 = cc * ck
            for h in range(GDN_HEADS):
                j = d * GDN_HEADS + h
                q = x_ref[0, c0:c0 + ck, h * GDN_DK:(h + 1) * GDN_DK]
                k = x_ref[0, c0:c0 + ck, GDN_K_W + h * GDN_DK:GDN_K_W + (h + 1) * GDN_DK]
                v = x_ref[0, c0:c0 + ck, 2 * GDN_K_W + h * GDN_DV:2 * GDN_K_W + (h + 1) * GDN_DV]
                gc = g_col[c0:c0 + ck, j:j + 1]
                gr = g_row[j:j + 1, c0:c0 + ck]
                beta = beta_col[c0:c0 + ck, GATE_W // 2 + j:GATE_W // 2 + j + 1]
                g_last = gc[last:last + 1, :]
                decay = jnp.exp(jnp.where(incl, gc - gr, -jnp.inf))
                kk = _mm_nt(k, k)
                qk = _mm_nt(q, k)
                a = jnp.where(strict, kk * decay * beta, 0.0)
                t_inv = _tri_inverse(a, same16, same32)
                egc = jnp.exp(gc)
                uw = _mm(t_inv, jnp.concatenate([v * beta, k * (beta * egc)], axis=1))
                u = uw[:, :GDN_DV]
                w = uw[:, GDN_DV:]
                s = s_scr[j]
                v_new = u - _mm(w, s)
                o_ref[0, c0:c0 + ck, h * GDN_DV:(h + 1) * GDN_DV] = _mm(q * egc, s) + _mm(qk * decay, v_new)
                s_scr[j] = s * jnp.exp(g_last) + _mm_tn(k * jnp.exp(g_last - gc), v_new)

    @pl.when(n == nt - 1)
    def _():
        sfin_ref[0] = s_scr[...]


def _gdn_mix(qkv, gab, gabt, prow, pcol, s0, *, tt):
    bg, lg, w = qkv.shape
    nt = lg // tt
    nh = N_DIR * GDN_HEADS
    fwd = lambda b, n: (b, n, 0)
    bwd = lambda b, n: (b, nt - 1 - n, 0)
    o_f, o_b, s_fin = pl.pallas_call(
        functools.partial(_gdn_kernel, nt=nt, tt=tt),
        grid=(bg, nt),
        in_specs=[pl.BlockSpec((1, tt, w), fwd), pl.BlockSpec((1, tt, w), bwd),
                  pl.BlockSpec((1, tt, LANES), fwd), pl.BlockSpec((1, tt, LANES), bwd),
                  pl.BlockSpec((1, GATE_W, tt), lambda b, n: (b, 0, n)),
                  pl.BlockSpec((1, GATE_W, tt), lambda b, n: (b, 0, nt - 1 - n)),
                  pl.BlockSpec((2, LANES), lambda b, n: (0, 0)),
                  pl.BlockSpec((GATE_W, 2), lambda b, n: (0, 0)),
                  pl.BlockSpec((1, nh, GDN_DK, GDN_DV), lambda b, n: (b, 0, 0, 0))],
        out_specs=[pl.BlockSpec((1, tt, GDN_V_W), fwd), pl.BlockSpec((1, tt, GDN_V_W), bwd),
                   pl.BlockSpec((1, nh, GDN_DK, GDN_DV), lambda b, n: (b, 0, 0, 0))],
        out_shape=[jax.ShapeDtypeStruct((bg, lg, GDN_V_W), F32), jax.ShapeDtypeStruct((bg, lg, GDN_V_W), F32),
                   jax.ShapeDtypeStruct((bg, nh, GDN_DK, GDN_DV), F32)],
        scratch_shapes=[pltpu.VMEM((nh, GDN_DK, GDN_DV), F32)],
        compiler_params=_params("arbitrary", "arbitrary"),
        name="gdn_mix",
    )(qkv, qkv, gab, gab, gabt, gabt, prow, pcol, s0)
    return o_f, o_b, s_fin


def _gelu_tanh(x):
    return 0.5 * x * (1.0 + jnp.tanh(math.sqrt(2.0 / math.pi) * (x + 0.044715 * (x * x * x))))


def _out_kernel(x_ref, ga_ref, shm_ref, scm_ref, gm_ref, attn_ref, yt_ref, of_ref, ob_ref, gz_ref,
                g2_ref, gng_ref, bd_ref, wglut_ref, bglu_ref, wo_ref, w1_ref, w2_ref, o_ref):
    x = x_ref[0]
    z = _gelu_tanh(yt_ref[...])
    gate = jax.nn.sigmoid(jnp.dot(wglut_ref[...], z.astype(BF16), preferred_element_type=F32) + bglu_ref[...])
    ssm_t = (z * gate).astype(BF16)
    o = of_ref[0] + ob_ref[0]
    gdn = o * lax.rsqrt(_seg_mean_sq(o, bd_ref) + EPS) * gng_ref[...] * _silu(gz_ref[0])
    mixed = (jnp.dot(attn_ref[0], wo_ref[:ATT_Q_W, :], preferred_element_type=F32)
             + lax.dot_general(ssm_t, wo_ref[ATT_Q_W:ATT_Q_W + SSM_WIDTH, :], _TN, preferred_element_type=F32)
             + jnp.dot(gdn.astype(BF16), wo_ref[ATT_Q_W + SSM_WIDTH:, :], preferred_element_type=F32))
    x1 = x + ga_ref[0] * mixed
    ms = jnp.mean(x1 * x1, axis=-1, keepdims=True)
    h2 = ((x1 * lax.rsqrt(ms + EPS) * g2_ref[...]) * (1.0 + scm_ref[0]) + shm_ref[0]).astype(BF16)
    acc = None
    fc = D_MODEL
    for j in range(D_FF // fc):
        a = jnp.maximum(jnp.dot(h2, w1_ref[:, j * fc:(j + 1) * fc], preferred_element_type=F32), 0.0)
        part = jnp.dot((a * a).astype(BF16), w2_ref[j * fc:(j + 1) * fc, :], preferred_element_type=F32)
        acc = part if acc is None else acc + part
    o_ref[0] = x1 + gm_ref[0] * acc


def _out_projection(x, mod3, mod_row, attn, yt, o_f, o_b, gz, p, consts, *, tm):
    bg, lg, _ = x.shape
    nt = lg // tm
    row = mod_row
    full = lambda shape: pl.BlockSpec(shape, lambda t, b: (0,) * len(shape), pipeline_mode=pl.Buffered(1))
    tok = lambda w: pl.BlockSpec((1, tm, w), lambda t, b: (b, t, 0))
    modc = lambda j: pl.BlockSpec((1, 1, D_MODEL), lambda t, b: (row(b), 0, j))
    mix_w = ATT_Q_W + SSM_WIDTH + GDN_V_W
    return pl.pallas_call(
        _out_kernel,
        grid=(nt, bg),
        in_specs=[tok(D_MODEL), modc(2), modc(3), modc(4), modc(5),
                  tok(ATT_Q_W), pl.BlockSpec((SSM_WIDTH, tm), lambda t, b: (0, b * nt + t)),
                  tok(GDN_V_W), tok(GDN_V_W), tok(GDN_V_W),
                  full((1, D_MODEL)), full((1, GDN_V_W)), full((GDN_V_W, GDN_V_W)),
                  full((SSM_WIDTH, SSM_WIDTH)), full((SSM_WIDTH, 1)),
                  full((mix_w, D_MODEL)), full((D_MODEL, D_FF)), full((D_FF, D_MODEL))],
        out_specs=tok(D_MODEL),
        out_shape=jax.ShapeDtypeStruct((bg, lg, D_MODEL), F32),
        compiler_params=_params("arbitrary", "arbitrary"),
        name="out_projection",
    )(x, mod3, mod3, mod3, mod3, attn, yt, o_f, o_b, gz,
      p["norm2_g"], p["gdn_norm_g"], consts["bd_mean"][:GDN_V_W, :GDN_V_W],
      p["w_glu_t"], p["b_glu"], p["w_out"], p["w_ff1"], p["w_ff2"])


def _constants(max_len):
    n_freq = HEAD_DIM // 4
    rows = jnp.repeat(jnp.arange(max_len // GRID_W, dtype=F32), GRID_W)
    cols = jnp.tile(jnp.arange(GRID_W, dtype=F32), max_len // GRID_W)
    inv_freq = jnp.power(ROPE_BASE, -jnp.arange(n_freq, dtype=F32) / n_freq)
    ang = jnp.concatenate([rows[:, None] * inv_freq, cols[:, None] * inv_freq], axis=-1)
    cos = jnp.repeat(jnp.cos(ang), 2, axis=-1)
    sin = jnp.repeat(jnp.sin(ang), 2, axis=-1) * jnp.tile(jnp.array([-1.0, 1.0], F32), HEAD_DIM // 2)
    seg = jnp.arange(ATT_Q_W) // HEAD_DIM
    same = seg[:, None] == seg[None, :]
    return {
        "cos": jnp.tile(cos, (1, LANES // HEAD_DIM)),
        "sin": jnp.tile(sin, (1, LANES // HEAD_DIM)),
        "bd_mean": jnp.where(same, 1.0 / HEAD_DIM, 0.0).astype(BF16),
    }


def _layer_params(l, w):
    w_in = w["w_in"][l]
    off = [0]
    for size in (ATT_Q_W, ATT_KV_W, ATT_KV_W, SSM_WIDTH, GDN_K_W, GDN_K_W, GDN_V_W, GDN_V_W, GATE_W):
        off.append(off[-1] + size)
    wab = w_in[:, off[8]:off[9]]
    col = lambda a: a.reshape(-1, 1)
    pad_lanes = lambda a: jnp.pad(a, ((0, 0), (0, LANES - a.shape[1])))
    alog = w["gdn_a_log"][l].reshape(1, -1)
    dtb = w["gdn_dt_bias"][l].reshape(1, -1)
    zeros = jnp.zeros_like(alog)
    ls = jnp.broadcast_to(w["ssm_log_step"][l][..., None], (N_DIR, SSM_GROUPS, SSM_STATE))
    return {
        "norm1_g": w["norm1_g"][l].reshape(1, -1), "norm2_g": w["norm2_g"][l].reshape(1, -1),
        "wq": w_in[:, off[0]:off[1]].astype(BF16), "wkv": w_in[:, off[1]:off[3]].astype(BF16),
        "wut": w_in[:, off[3]:off[4]].T.astype(BF16), "wg": w_in[:, off[4]:off[7]].astype(BF16),
        "wz": w_in[:, off[7]:off[8]].astype(BF16),
        "wab": pad_lanes(wab).astype(BF16), "wabt": wab.T.astype(BF16),
        "q_norm_g": jnp.tile(w["q_norm_g"][l], N_HEADS).reshape(1, -1),
        "k_norm_g": jnp.tile(w["k_norm_g"][l], N_KV_HEADS).reshape(1, -1),
        "attn_sink": w["attn_sink"][l],
        "ssm_pc": jnp.stack([w["ssm_lam_re"][l], w["ssm_lam_im"][l], ls], axis=-1),
        "ssm_pr": jnp.stack([w["ssm_lam_re"][l], w["ssm_lam_im"][l], ls], axis=-2),
        "ssm_b_re": w["ssm_b_re"][l], "ssm_b_im": w["ssm_b_im"][l],
        "ssm_c_re": w["ssm_c_re"][l], "ssm_c_im": w["ssm_c_im"][l],
        "ssm_c_re_t": jnp.swapaxes(w["ssm_c_re"][l], -1, -2), "ssm_c_im_t": jnp.swapaxes(w["ssm_c_im"][l], -1, -2),
        "ssm_d": w["ssm_d"][l],
        "w_glu_t": w["ssm_w_glu"][l].T.astype(BF16), "b_glu": col(w["ssm_b_glu"][l]),
        "gdn_conv_w": w["gdn_conv_w"][l],
        "gdn_prow": pad_lanes(jnp.concatenate([jnp.concatenate([alog, zeros], 1), jnp.concatenate([dtb, zeros], 1)], 0)),
        "gdn_pcol": jnp.concatenate([jnp.concatenate([alog, zeros], 1), jnp.concatenate([dtb, zeros], 1)], 0).T,
        "gdn_norm_g": jnp.tile(w["gdn_norm_g"][l], GDN_HEADS).reshape(1, -1),
        "w_out": w["w_out"][l].astype(BF16), "w_ff1": w["w_ff1"][l].astype(BF16), "w_ff2": w["w_ff2"][l].astype(BF16),
    }


def _state_to_lanes(h0):
    bg = h0.shape[0]
    return h0.transpose(3, 0, 2, 1, 4).reshape(SSM_GROUPS, bg, 2 * N_DIR * SSM_STATE)


def _lanes_to_state(h):
    bg = h.shape[1]
    return h.reshape(SSM_GROUPS, bg, 2, N_DIR, SSM_STATE).transpose(1, 3, 2, 0, 4)


def _layer(x, mod3, mod_row, p, consts, tables, ctx_kv, layer, ssm_h0, gdn_s0, *, tm, gdn_tt):
    bg, lg, _ = x.shape
    latent = ctx_kv is not None
    q, k, v, ut, gqkv, gz, gab, gabt = _in_projection(x, mod3, mod_row, p, consts, rope=latent, tm=tm)
    if latent:
        attn = _latent_attention(q, k, v, ctx_kv[0], ctx_kv[1], layer, p["attn_sink"])
    else:
        attn = _context_attention(q, k, v, p["attn_sink"])
    yt, h_fin = _ssm_mix(ut, tables, _state_to_lanes(ssm_h0), p["ssm_d"], bg=bg, lg=lg)
    qkv = _gdn_pre(gqkv, p["gdn_conv_w"], (consts["bd_mean"][:GDN_K_W, :GDN_K_W]), tt=min(lg, 512))
    o_f, o_b, s_fin = _gdn_mix(qkv, gab, gabt, p["gdn_prow"], p["gdn_pcol"],
                               gdn_s0.reshape(bg, N_DIR * GDN_HEADS, GDN_DK, GDN_DV), tt=gdn_tt)
    x = _out_projection(x, mod3, mod_row, attn, yt, o_f, o_b, gz, p, consts, tm=tm)
    return x, (k, v, _lanes_to_state(h_fin), s_fin.reshape(bg, N_DIR, GDN_HEADS, GDN_DK, GDN_DV))


def kernel(x_prompt, x_sample, c, cache_k, cache_v, state_ssm, state_gdn, c_ctx, norm1_g, norm2_g, w_mod, b_mod, w_in, q_norm_g, k_norm_g, attn_sink, ssm_lam_re, ssm_lam_im, ssm_log_step, ssm_b_re, ssm_b_im, ssm_c_re, ssm_c_im, ssm_d, ssm_w_glu, ssm_b_glu, gdn_conv_w, gdn_a_log, gdn_dt_bias, gdn_norm_g, w_out, w_ff1, w_ff2):
    w = dict(norm1_g=norm1_g, norm2_g=norm2_g, w_in=w_in, q_norm_g=q_norm_g, k_norm_g=k_norm_g, attn_sink=attn_sink,
             ssm_lam_re=ssm_lam_re, ssm_lam_im=ssm_lam_im, ssm_log_step=ssm_log_step, ssm_b_re=ssm_b_re,
             ssm_b_im=ssm_b_im, ssm_c_re=ssm_c_re, ssm_c_im=ssm_c_im, ssm_d=ssm_d, ssm_w_glu=ssm_w_glu,
             ssm_b_glu=ssm_b_glu, gdn_conv_w=gdn_conv_w, gdn_a_log=gdn_a_log, gdn_dt_bias=gdn_dt_bias,
             gdn_norm_g=gdn_norm_g, w_out=w_out, w_ff1=w_ff1, w_ff2=w_ff2)
    n_ctx, seq, _ = x_prompt.shape
    n_dec, dec_seq, _ = x_sample.shape
    depth = w_in.shape[0]
    past = cache_k.shape[2]

    n_rows = -(-(n_dec + 1) // SUBLANES) * SUBLANES
    cond = jnp.zeros((n_rows, D_MODEL), F32).at[:n_dec].set(c).at[n_dec].set(c_ctx)
    mod = _modulation(cond, w_mod, b_mod)

    consts = _constants(max(seq, dec_seq))
    cache_k4 = cache_k.reshape(n_dec, depth, past, ATT_KV_W)
    cache_v4 = cache_v.reshape(n_dec, depth, past, ATT_KV_W)
    ssm_zero = jnp.zeros((n_ctx, N_DIR, 2, SSM_GROUPS, SSM_STATE), F32)
    gdn_zero = jnp.zeros((n_ctx, N_DIR, GDN_HEADS, GDN_DK, GDN_DV), F32)

    xp, xs = x_prompt, x_sample
    ks, vs, ss, gs = [], [], [], []
    for l in range(depth):
        p = _layer_params(l, w)
        tables = _ssm_prep(p)
        mod3 = mod[l].reshape(n_rows, 1, N_MOD * D_MODEL)
        xp, (k_l, v_l, s_l, g_l) = _layer(xp, mod3, lambda b: n_dec, p, consts, tables, None, l,
                                          ssm_zero, gdn_zero, tm=min(seq, 512), gdn_tt=min(seq, 128))
        ks.append(k_l.reshape(n_ctx, seq, N_KV_HEADS, HEAD_DIM))
        vs.append(v_l.reshape(n_ctx, seq, N_KV_HEADS, HEAD_DIM))
        ss.append(s_l)
        gs.append(g_l)
        xs, _ = _layer(xs, mod3, lambda b: b, p, consts, tables, (cache_k4, cache_v4), l,
                       state_ssm[:, l], state_gdn[:, l], tm=min(dec_seq, 512), gdn_tt=min(dec_seq, 128))
    return (xp, xs, jnp.stack(ks, axis=1), jnp.stack(vs, axis=1), jnp.stack(ss, axis=1), jnp.stack(gs, axis=1))
```

```python
import functools
import math

import jax
import jax.numpy as jnp
from jax import lax
from jax.experimental import pallas as pl
from jax.experimental.pallas import tpu as pltpu

F32 = jnp.float32
BF16 = jnp.bfloat16
EPS = 1e-6
NEG_INF = -1e30

D_MODEL = 1024
DEPTH = 4
GRID_W = 64
N_DIR = 2
N_HEADS = 8
N_KV_HEADS = 2
GQA = N_HEADS // N_KV_HEADS
HEAD_DIM = 64
ATT_BLOCK = 128
ROPE_BASE = 10000.0
SSM_GROUP_CH = 16
SSM_GROUPS = 16
SSM_WIDTH = SSM_GROUPS * SSM_GROUP_CH
SSM_STATE = 64
GDN_HEADS = 4
GDN_DK = 64
GDN_DV = 64
GDN_K_W = GDN_HEADS * GDN_DK
GDN_V_W = GDN_HEADS * GDN_DV
GDN_QKV_W = 2 * GDN_K_W + GDN_V_W
GDN_CHUNK = 64
ATT_Q_W = N_HEADS * HEAD_DIM
ATT_KV_W = N_KV_HEADS * HEAD_DIM
D_FF = 4 * D_MODEL
N_MOD = 6
GATE_W = 2 * N_DIR * GDN_HEADS

LANES = 128
SUBLANES = 8
SSM_T = LANES
VMEM_LIMIT = 56 * 1024 * 1024

_NT = (((1,), (1,)), ((), ()))
_TN = (((0,), (0,)), ((), ()))


def _mm(a, b):
    return jnp.dot(a.astype(BF16), b.astype(BF16), preferred_element_type=F32)


def _mm_nt(a, b):
    return lax.dot_general(a.astype(BF16), b.astype(BF16), _NT, preferred_element_type=F32)


def _mm_tn(a, b):
    return lax.dot_general(a.astype(BF16), b.astype(BF16), _TN, preferred_element_type=F32)


def _split3(a):
    hi = a.astype(BF16)
    r1 = a - hi.astype(F32)
    mid = r1.astype(BF16)
    lo = (r1 - mid.astype(F32)).astype(BF16)
    return hi, mid, lo


def _mm_exact_rhs(a, b_bf16):
    hi, mid, lo = _split3(a)
    return (jnp.dot(hi, b_bf16, preferred_element_type=F32)
            + jnp.dot(mid, b_bf16, preferred_element_type=F32)
            + jnp.dot(lo, b_bf16, preferred_element_type=F32))


def _mm_f32(a, b):
    ah, am, al = _split3(a)
    bh, bm, bl = _split3(b)
    d = lambda x, y: jnp.dot(x, y, preferred_element_type=F32)
    return (d(ah, bh) + (d(ah, bm) + d(am, bh)) + (d(ah, bl) + d(am, bm) + d(al, bh)))


def _silu(x):
    return x * jax.nn.sigmoid(x)


def _softplus(x):
    return jnp.maximum(x, 0.0) + jnp.log1p(jnp.exp(-jnp.abs(x)))


def _params(*sem):
    return pltpu.CompilerParams(dimension_semantics=sem, vmem_limit_bytes=VMEM_LIMIT)


def _mod_kernel(cond_ref, w_ref, b_ref, o_ref):
    c = cond_ref[...]
    o_ref[0] = _mm(_silu(c), w_ref[0]) + b_ref[0]


def _modulation(cond, w_mod, b_mod):
    rows = cond.shape[0]
    cn = 1536
    return pl.pallas_call(
        _mod_kernel,
        grid=(DEPTH, N_MOD * D_MODEL // cn),
        in_specs=[pl.BlockSpec((rows, D_MODEL), lambda l, j: (0, 0)),
                  pl.BlockSpec((1, D_MODEL, cn), lambda l, j: (l, 0, j)),
                  pl.BlockSpec((1, 1, cn), lambda l, j: (l, 0, j))],
        out_specs=pl.BlockSpec((1, rows, cn), lambda l, j: (l, 0, j)),
        out_shape=jax.ShapeDtypeStruct((DEPTH, rows, N_MOD * D_MODEL), F32),
        compiler_params=_params("arbitrary", "arbitrary"),
        name="modulation",
    )(cond, w_mod, b_mod.reshape(DEPTH, 1, N_MOD * D_MODEL))


def _seg_mean_sq(x, bd_ref):
    xx = x * x
    hi = xx.astype(BF16)
    lo = (xx - hi.astype(F32)).astype(BF16)
    n = x.shape[-1]
    bd = bd_ref[:n, :n]
    return jnp.dot(hi, bd, preferred_element_type=F32) + jnp.dot(lo, bd, preferred_element_type=F32)


def _rope(x, cos, sin_signed):
    n = x.shape[-1]
    nxt = pltpu.roll(x, n - 1, 1)
    prv = pltpu.roll(x, 1, 1)
    lane = lax.broadcasted_iota(jnp.int32, x.shape, 1)
    swapped = jnp.where(lane % 2 == 0, nxt, prv)
    return x * cos + swapped * sin_signed


def _in_kernel(x_ref, sh_ref, sc_ref, g1_ref, wq_ref, wkv_ref, wut_ref, wg_ref, wz_ref, wab_ref, wabt_ref,
               qg_ref, kg_ref, bd_ref, cos_ref, sin_ref,
               q_out, k_out, v_out, ut_out, g_out, z_out, ab_out, abt_out, *, rope):
    x = x_ref[0]
    ms = jnp.mean(x * x, axis=-1, keepdims=True)
    h = (x * lax.rsqrt(ms + EPS) * g1_ref[...]) * (1.0 + sc_ref[0]) + sh_ref[0]
    hb = h.astype(BF16)

    q = jnp.dot(hb, wq_ref[...], preferred_element_type=F32)
    q = q * lax.rsqrt(_seg_mean_sq(q, bd_ref) + EPS) * qg_ref[...]
    kv = jnp.dot(hb, wkv_ref[...], preferred_element_type=F32)
    k = kv[:, :ATT_KV_W]
    k = k * lax.rsqrt(_seg_mean_sq(k, bd_ref) + EPS) * kg_ref[...]
    if rope:
        cos = cos_ref[...]
        sin = sin_ref[...]
        q = _rope(q, jnp.concatenate([cos] * (ATT_Q_W // LANES), axis=1),
                  jnp.concatenate([sin] * (ATT_Q_W // LANES), axis=1))
        k = _rope(k, cos, sin)
    q_out[0] = (q * (HEAD_DIM ** -0.5)).astype(BF16)
    k_out[0] = k
    v_out[0] = kv[:, ATT_KV_W:]
    ut_out[...] = lax.dot_general(wut_ref[...], hb, _NT, preferred_element_type=F32)
    g_out[0] = jnp.dot(hb, wg_ref[...], preferred_element_type=F32)
    z_out[0] = jnp.dot(hb, wz_ref[...], preferred_element_type=F32)
    ab_out[0] = jnp.dot(hb, wab_ref[...], preferred_element_type=F32)
    abt_out[0] = lax.dot_general(wabt_ref[...], hb, _NT, preferred_element_type=F32)


def _in_projection(x, mod3, mod_row, p, consts, *, rope, tm):
    bg, lg, _ = x.shape
    nt = lg // tm
    row = mod_row
    full = lambda shape: pl.BlockSpec(shape, lambda t, b: (0,) * len(shape))
    tok = lambda w: pl.BlockSpec((1, tm, w), lambda t, b: (b, t, 0))
    in_specs = [
        tok(D_MODEL),
        pl.BlockSpec((1, 1, D_MODEL), lambda t, b: (row(b), 0, 0)),
        pl.BlockSpec((1, 1, D_MODEL), lambda t, b: (row(b), 0, 1)),
        full((1, D_MODEL)),
        full((D_MODEL, ATT_Q_W)), full((D_MODEL, 2 * ATT_KV_W)), full((SSM_WIDTH, D_MODEL)),
        full((D_MODEL, GDN_QKV_W)), full((D_MODEL, GDN_V_W)), full((D_MODEL, LANES)), full((GATE_W, D_MODEL)),
        full((1, ATT_Q_W)), full((1, ATT_KV_W)), full((ATT_Q_W, ATT_Q_W)),
        pl.BlockSpec((tm, LANES), lambda t, b: (t, 0)),
        pl.BlockSpec((tm, LANES), lambda t, b: (t, 0)),
    ]
    out_specs = [
        tok(ATT_Q_W), tok(ATT_KV_W), tok(ATT_KV_W),
        pl.BlockSpec((SSM_WIDTH, tm), lambda t, b: (0, b * nt + t)),
        tok(GDN_QKV_W), tok(GDN_V_W), tok(LANES),
        pl.BlockSpec((1, GATE_W, tm), lambda t, b: (b, 0, t)),
    ]
    out_shape = [
        jax.ShapeDtypeStruct((bg, lg, ATT_Q_W), BF16),
        jax.ShapeDtypeStruct((bg, lg, ATT_KV_W), F32),
        jax.ShapeDtypeStruct((bg, lg, ATT_KV_W), F32),
        jax.ShapeDtypeStruct((SSM_WIDTH, bg * lg), F32),
        jax.ShapeDtypeStruct((bg, lg, GDN_QKV_W), F32),
        jax.ShapeDtypeStruct((bg, lg, GDN_V_W), F32),
        jax.ShapeDtypeStruct((bg, lg, LANES), F32),
        jax.ShapeDtypeStruct((bg, GATE_W, lg), F32),
    ]
    return pl.pallas_call(
        functools.partial(_in_kernel, rope=rope),
        grid=(nt, bg), in_specs=in_specs, out_specs=out_specs, out_shape=out_shape,
        compiler_params=_params("arbitrary", "arbitrary"),
        name="in_projection",
    )(x, mod3, mod3, p["norm1_g"], p["wq"], p["wkv"], p["wut"], p["wg"], p["wz"], p["wab"], p["wabt"],
      p["q_norm_g"], p["k_norm_g"], consts["bd_mean"], consts["cos"][:lg], consts["sin"][:lg])


def _attend(q_ref, k_all, v_all, masks, sink_ref, o_ref):
    nq = q_ref.shape[1]
    lane = lax.broadcasted_iota(jnp.int32, (1, LANES), 1)
    low = lane < HEAD_DIM
    keep_low = low.astype(BF16)
    keep_high = 1 - keep_low
    tiles = [q_ref[0, :, i * LANES:(i + 1) * LANES] for i in range(GQA)]
    q_rows = jnp.concatenate([t * keep_low for t in tiles] + [t * keep_high for t in tiles], axis=0)
    s = lax.dot_general(q_rows, k_all, _NT, preferred_element_type=F32)
    n_keys = s.shape[1]
    probs, inv_den = [], []
    for h in range(N_HEADS):
        sh = s[h * nq:(h + 1) * nq, :]
        if masks:
            cols = []
            for c0 in range(0, n_keys, ATT_BLOCK):
                piece = sh[:, c0:c0 + ATT_BLOCK]
                cols.append(jnp.where(masks[c0], piece, NEG_INF) if c0 in masks else piece)
            sh = jnp.concatenate(cols, axis=1)
        sink = sink_ref[h]
        m = jnp.maximum(jnp.max(sh, axis=-1, keepdims=True), sink)
        p = jnp.exp(sh - m)
        inv_den.append(1.0 / (jnp.exp(sink - m) + jnp.sum(p, axis=-1, keepdims=True)))
        probs.append(p.astype(BF16))
    o = jnp.dot(jnp.concatenate(probs, axis=0), v_all, preferred_element_type=F32)
    for i in range(GQA):
        o_lo = o[i * nq:(i + 1) * nq, :] * inv_den[i]
        o_hi = o[(GQA + i) * nq:(GQA + i + 1) * nq, :] * inv_den[GQA + i]
        o_ref[0, :, i * LANES:(i + 1) * LANES] = jnp.where(low, o_lo, o_hi).astype(o_ref.dtype)


def _latent_attn_kernel(q_ref, kp_ref, kc_ref, kn_ref, vp_ref, vc_ref, vn_ref, ck_ref, cv_ref, sink_ref, o_ref, *, nb):
    i = pl.program_id(1)
    r = lax.broadcasted_iota(jnp.int32, (ATT_BLOCK, ATT_BLOCK), 0)
    c = lax.broadcasted_iota(jnp.int32, (ATT_BLOCK, ATT_BLOCK), 1)
    masks = {0: jnp.logical_and(c >= r, i > 0), 2 * ATT_BLOCK: jnp.logical_and(c <= r, i < nb - 1)}
    k_all = jnp.concatenate([kp_ref[0], kc_ref[0], kn_ref[0], ck_ref[0, 0]], axis=0).astype(BF16)
    v_all = jnp.concatenate([vp_ref[0], vc_ref[0], vn_ref[0], cv_ref[0, 0]], axis=0).astype(BF16)
    _attend(q_ref, k_all, v_all, masks, sink_ref, o_ref)


def _latent_attention(q, k, v, cache_k4, cache_v4, layer, sink):
    bg, lg, _ = q.shape
    nb = lg // ATT_BLOCK
    past = cache_k4.shape[2]
    blk = lambda w, f: pl.BlockSpec((1, ATT_BLOCK, w), f)
    prev = lambda b, i: (b, jnp.maximum(i - 1, 0), 0)
    cur = lambda b, i: (b, i, 0)
    nxt = lambda b, i: (b, jnp.minimum(i + 1, nb - 1), 0)
    ctx = pl.BlockSpec((1, 1, past, ATT_KV_W), lambda b, i: (b, layer, 0, 0))
    return pl.pallas_call(
        functools.partial(_latent_attn_kernel, nb=nb),
        grid=(bg, nb),
        in_specs=[blk(ATT_Q_W, cur), blk(ATT_KV_W, prev), blk(ATT_KV_W, cur), blk(ATT_KV_W, nxt),
                  blk(ATT_KV_W, prev), blk(ATT_KV_W, cur), blk(ATT_KV_W, nxt), ctx, ctx,
                  pl.BlockSpec(memory_space=pltpu.SMEM)],
        out_specs=blk(ATT_Q_W, cur),
        out_shape=jax.ShapeDtypeStruct((bg, lg, ATT_Q_W), BF16),
        compiler_params=_params("arbitrary", "arbitrary"),
        name="latent_attention",
    )(q, k, k, k, v, v, v, cache_k4, cache_v4, sink)


def _context_attn_kernel(q_ref, k_ref, v_ref, sink_ref, o_ref):
    _attend(q_ref, k_ref[0].astype(BF16), v_ref[0].astype(BF16), {}, sink_ref, o_ref)


def _context_attention(q, k, v, sink):
    bg, lg, _ = q.shape
    nb = lg // ATT_BLOCK
    return pl.pallas_call(
        _context_attn_kernel,
        grid=(bg, nb),
        in_specs=[pl.BlockSpec((1, ATT_BLOCK, ATT_Q_W), lambda b, i: (b, i, 0)),
                  pl.BlockSpec((1, lg, ATT_KV_W), lambda b, i: (b, 0, 0)),
                  pl.BlockSpec((1, lg, ATT_KV_W), lambda b, i: (b, 0, 0)),
                  pl.BlockSpec(memory_space=pltpu.SMEM)],
        out_specs=pl.BlockSpec((1, ATT_BLOCK, ATT_Q_W), lambda b, i: (b, i, 0)),
        out_shape=jax.ShapeDtypeStruct((bg, lg, ATT_Q_W), BF16),
        compiler_params=_params("arbitrary", "arbitrary"),
        name="context_attention",
    )(q, k, v, sink)


def _powers(er, th, e):
    mag = jnp.exp(er * e)
    return mag * jnp.cos(th * e), mag * jnp.sin(th * e)


def _ssm_prep_kernel(pc_ref, pr_ref, br_ref, bi_ref, cr_ref, ci_ref, crt_ref, cit_ref,
                     kv_out, rs_out, f_out, at_out):
    t = SSM_T
    p = SSM_STATE
    lane2 = lax.broadcasted_iota(jnp.int32, (1, 2 * t), 1)
    lane1 = lax.broadcasted_iota(jnp.int32, (1, t), 1)
    e_kv = [jnp.maximum(lane2 - t, 0).astype(F32), jnp.maximum(t - lane2, 0).astype(F32)]
    m_kv = [lane2 >= t, jnp.logical_and(lane2 >= 1, lane2 <= t)]
    e_rs = [(t - 1 - lane1).astype(F32), lane1.astype(F32)]
    e_f = [(lane1 + 1).astype(F32), (t - lane1).astype(F32)]

    bbr, bbi, pk, prs, pf = [], [], [], [], []
    for d in range(N_DIR):
        lr = pc_ref[d, 0, :, 0:1]
        li = pc_ref[d, 0, :, 1:2]
        step = jnp.exp(pc_ref[d, 0, :, 2:3])
        er = lr * step
        th = li * step
        mag = jnp.exp(er)
        ar = mag * jnp.cos(th)
        ai = mag * jnp.sin(th)
        den = lr * lr + li * li
        nr = ar - 1.0
        fr = (nr * lr + ai * li) / den
        fi = (ai * lr - nr * li) / den
        b_re = br_ref[d, 0]
        b_im = bi_ref[d, 0]
        bbr.append(fr * b_re - fi * b_im)
        bbi.append(fr * b_im + fi * b_re)
        kr, ki = _powers(er, th, e_kv[d])
        pk.append((jnp.where(m_kv[d], kr, 0.0), jnp.where(m_kv[d], ki, 0.0)))
        prs.append(_powers(er, th, e_rs[d]))
        pf.append(_powers(er, th, e_f[d]))
        lr_r = pr_ref[d, 0, 0:1, :]
        li_r = pr_ref[d, 0, 1:2, :]
        step_r = jnp.exp(pr_ref[d, 0, 2:3, :])
        mag_t = jnp.exp(lr_r * step_r * float(t))
        at_out[0, :, d * p:(d + 1) * p] = mag_t * jnp.cos(li_r * step_r * float(t))
        at_out[0, :, (N_DIR + d) * p:(N_DIR + d + 1) * p] = mag_t * jnp.sin(li_r * step_r * float(t))

    cmat = jnp.concatenate([cr_ref[0, 0], -ci_ref[0, 0], cr_ref[1, 0], -ci_ref[1, 0]], axis=1)
    for ci in range(SSM_GROUP_CH):
        rows = []
        for d in range(N_DIR):
            cbr = bbr[d][:, ci:ci + 1]
            cbi = bbi[d][:, ci:ci + 1]
            kr, ki = pk[d]
            rows += [kr * cbr - ki * cbi, kr * cbi + ki * cbr]
        kv_out[0, ci * SSM_GROUP_CH:(ci + 1) * SSM_GROUP_CH, :] = _mm_f32(cmat, jnp.concatenate(rows, axis=0))
        for d in range(N_DIR):
            cbr = bbr[d][:, ci:ci + 1]
            cbi = bbi[d][:, ci:ci + 1]
            sr, si = prs[d]
            rs_out[0, d * p:(d + 1) * p, ci * t:(ci + 1) * t] = (sr * cbr - si * cbi).astype(BF16)
            rs_out[0, (N_DIR + d) * p:(N_DIR + d + 1) * p, ci * t:(ci + 1) * t] = (sr * cbi + si * cbr).astype(BF16)
    for co in range(SSM_GROUP_CH):
        for d in range(N_DIR):
            ccr = crt_ref[d, 0, :, co:co + 1]
            cci = cit_ref[d, 0, :, co:co + 1]
            fr_, fi_ = pf[d]
            f_out[0, d * p:(d + 1) * p, co * t:(co + 1) * t] = (ccr * fr_ - cci * fi_).astype(BF16)
            f_out[0, (N_DIR + d) * p:(N_DIR + d + 1) * p, co * t:(co + 1) * t] = (-(ccr * fi_ + cci * fr_)).astype(BF16)


def _ssm_prep(p):
    t = SSM_T
    g = SSM_GROUPS
    ns = N_DIR * 2 * SSM_STATE
    spec4 = lambda a, b: pl.BlockSpec((N_DIR, 1, a, b), lambda i: (0, i, 0, 0))
    return pl.pallas_call(
        _ssm_prep_kernel,
        grid=(g,),
        in_specs=[spec4(SSM_STATE, 3), spec4(3, SSM_STATE),
                  spec4(SSM_STATE, SSM_GROUP_CH), spec4(SSM_STATE, SSM_GROUP_CH),
                  spec4(SSM_GROUP_CH, SSM_STATE), spec4(SSM_GROUP_CH, SSM_STATE),
                  spec4(SSM_STATE, SSM_GROUP_CH), spec4(SSM_STATE, SSM_GROUP_CH)],
        out_specs=[pl.BlockSpec((1, SSM_GROUP_CH * SSM_GROUP_CH, 2 * t), lambda i: (i, 0, 0)),
                   pl.BlockSpec((1, ns, SSM_GROUP_CH * t), lambda i: (i, 0, 0)),
                   pl.BlockSpec((1, ns, SSM_GROUP_CH * t), lambda i: (i, 0, 0)),
                   pl.BlockSpec((1, 1, ns), lambda i: (i, 0, 0))],
        out_shape=[jax.ShapeDtypeStruct((g, SSM_GROUP_CH * SSM_GROUP_CH, 2 * t), F32),
                   jax.ShapeDtypeStruct((g, ns, SSM_GROUP_CH * t), BF16),
                   jax.ShapeDtypeStruct((g, ns, SSM_GROUP_CH * t), BF16),
                   jax.ShapeDtypeStruct((g, 1, ns), F32)],
        compiler_params=_params("arbitrary"),
        name="ssm_prep",
    )(p["ssm_pc"], p["ssm_pr"], p["ssm_b_re"], p["ssm_b_im"], p["ssm_c_re"], p["ssm_c_im"],
      p["ssm_c_re_t"], p["ssm_c_im_t"])


def _ssm_kernel(*refs, dims):
    ng = len(dims)
    u_refs = refs[:ng]
    kv_ref, rs_ref, f_ref, at_ref = refs[ng:ng + 4]
    h0_refs = refs[ng + 4:2 * ng + 4]
    d_ref = refs[2 * ng + 4]
    y_outs = refs[2 * ng + 5:3 * ng + 5]
    hfin_outs = refs[3 * ng + 5:4 * ng + 5]
    m_scr = refs[4 * ng + 5]
    scratch = refs[4 * ng + 6:]
    t = SSM_T
    nch = SSM_GROUP_CH
    g = pl.program_id(0)
    half = N_DIR * SSM_STATE

    def gen(ci, carry):
        r0 = pl.multiple_of(ci * t, t)
        for co in range(nch):
            row = kv_ref[0, pl.ds(ci * nch + co, 1), :]
            rolled = pltpu.roll(jnp.broadcast_to(row, (t, 2 * t)), t, 1, stride=1, stride_axis=0)
            m_scr[pl.ds(r0, t), co * t:(co + 1) * t] = rolled[:, :t].astype(BF16)
        return carry
    lax.fori_loop(0, nch, gen, 0)

    a_re = at_ref[0, :, :half]
    a_im = at_ref[0, :, half:]
    is_fwd = lax.broadcasted_iota(jnp.int32, (1, half), 1) < SSM_STATE
    for gi, (bg, nc) in enumerate(dims):
        u_ref, h0_ref, y_out, hfin_out = u_refs[gi], h0_refs[gi], y_outs[gi], hfin_outs[gi]
        s_scr, hpf_scr, hpb_scr = scratch[3 * gi:3 * gi + 3]
        u_flat = jnp.concatenate([u_ref[0, ci].astype(BF16) for ci in range(nch)], axis=1)

        s = lax.dot_general(u_flat, rs_ref[0], _NT, preferred_element_type=F32)
        s_scr[0] = s[:, :half]
        s_scr[1] = s[:, half:]

        def advance(re, im, rows, s_scr=s_scr):
            return (a_re * re - a_im * im + s_scr[0, rows, :], a_re * im + a_im * re + s_scr[1, rows, :])

        def scan(i, carry, bg=bg, nc=nc, hpf_scr=hpf_scr, hpb_scr=hpb_scr, advance=advance):
            f_re, f_im, b_re, b_im = carry
            rows_f = pl.ds(i, bg, stride=nc)
            rows_b = pl.ds(nc - 1 - i, bg, stride=nc)
            hpf_scr[0, rows_f, :] = f_re
            hpf_scr[1, rows_f, :] = f_im
            hpb_scr[0, rows_b, :] = b_re
            hpb_scr[1, rows_b, :] = b_im
            return advance(f_re, f_im, rows_f) + advance(b_re, b_im, rows_b)
        h0_re = h0_ref[0, :, :half]
        h0_im = h0_ref[0, :, half:]
        f_re, f_im, b_re, b_im = lax.fori_loop(0, nc, scan, (h0_re, h0_im, h0_re, h0_im))
        hfin_out[0, :, :half] = jnp.where(is_fwd, f_re, b_re)
        hfin_out[0, :, half:] = jnp.where(is_fwd, f_im, b_im)
        hprev = jnp.concatenate([jnp.where(is_fwd, hpf_scr[0], hpb_scr[0]),
                                 jnp.where(is_fwd, hpf_scr[1], hpb_scr[1])], axis=1)

        y = (jnp.dot(hprev.astype(BF16), f_ref[0], preferred_element_type=F32)
             + jnp.dot(u_flat, m_scr[...], preferred_element_type=F32))
        for co in range(nch):
            y_out[0, co] = y[:, co * t:(co + 1) * t] + d_ref[g * nch + co] * u_ref[0, co]


def _ssm_mix(uts, tables, h0s, d_skip, shapes):
    t = SSM_T
    g = SSM_GROUPS
    nch = SSM_GROUP_CH
    ns = N_DIR * 2 * SSM_STATE
    kv, rs, f, at = tables
    dims = tuple((bg, lg // t) for bg, lg in shapes)
    u_specs = [pl.BlockSpec((1, nch, bg * nc, t), lambda i: (i, 0, 0, 0)) for bg, nc in dims]
    h_specs = [pl.BlockSpec((1, bg, ns), lambda i: (i, 0, 0)) for bg, _ in dims]
    scratch = [pltpu.VMEM((nch * t, nch * t), BF16)]
    for bg, nc in dims:
        scratch += [pltpu.VMEM((2, bg * nc, ns // 2), F32)] * 3
    outs = pl.pallas_call(
        functools.partial(_ssm_kernel, dims=dims),
        grid=(g,),
        in_specs=u_specs + [pl.BlockSpec((1, nch * nch, 2 * t), lambda i: (i, 0, 0)),
                            pl.BlockSpec((1, ns, nch * t), lambda i: (i, 0, 0)),
                            pl.BlockSpec((1, ns, nch * t), lambda i: (i, 0, 0)),
                            pl.BlockSpec((1, 1, ns), lambda i: (i, 0, 0))]
                 + h_specs + [pl.BlockSpec(memory_space=pltpu.SMEM)],
        out_specs=u_specs + h_specs,
        out_shape=[jax.ShapeDtypeStruct((g, nch, bg * nc, t), F32) for bg, nc in dims]
                  + [jax.ShapeDtypeStruct((g, bg, ns), F32) for bg, _ in dims],
        scratch_shapes=scratch,
        compiler_params=_params("arbitrary"),
        name="ssm_mix",
    )(*[ut.reshape(g, nch, bg * nc, t) for ut, (bg, nc) in zip(uts, dims)], kv, rs, f, at, *h0s, d_skip)
    n = len(dims)
    return ([y.reshape(g * nch, bg * nc * t) for y, (bg, nc) in zip(outs[:n], dims)], list(outs[n:]))


def _gdn_pre_kernel(x_ref, xp_ref, xn_ref, w_ref, bd_ref, o_ref, *, nt):
    i = pl.program_id(1)
    x = x_ref[0]
    tt = x.shape[0]
    row = lax.broadcasted_iota(jnp.int32, x.shape, 0)
    prev_row = jnp.where(i > 0, xp_ref[0, SUBLANES - 1:SUBLANES, :], 0.0)
    next_row = jnp.where(i < nt - 1, xn_ref[0, 0:1, :], 0.0)
    x_m1 = jnp.where(row == 0, prev_row, pltpu.roll(x, 1, 0))
    x_p1 = jnp.where(row == tt - 1, next_row, pltpu.roll(x, tt - 1, 0))
    y = _silu(x_m1 * w_ref[0:1, :] + x * w_ref[1:2, :] + x_p1 * w_ref[2:3, :])
    q = y[:, :GDN_K_W]
    k = y[:, GDN_K_W:2 * GDN_K_W]
    o_ref[0, :, :GDN_K_W] = q * lax.rsqrt(_seg_mean_sq(q, bd_ref) * GDN_DK + EPS) * (GDN_DK ** -0.5)
    o_ref[0, :, GDN_K_W:2 * GDN_K_W] = k * lax.rsqrt(_seg_mean_sq(k, bd_ref) * GDN_DK + EPS)
    o_ref[0, :, 2 * GDN_K_W:] = y[:, 2 * GDN_K_W:]


def _gdn_pre(gqkv, conv_w, bd_sum, *, tt):
    bg, lg, w = gqkv.shape
    nt = lg // tt
    hb = tt // SUBLANES
    return pl.pallas_call(
        functools.partial(_gdn_pre_kernel, nt=nt),
        grid=(bg, nt),
        in_specs=[pl.BlockSpec((1, tt, w), lambda b, i: (b, i, 0)),
                  pl.BlockSpec((1, SUBLANES, w), lambda b, i: (b, jnp.maximum(i * hb - 1, 0), 0)),
                  pl.BlockSpec((1, SUBLANES, w), lambda b, i: (b, jnp.minimum((i + 1) * hb, lg // SUBLANES - 1), 0)),
                  pl.BlockSpec((3, w), lambda b, i: (0, 0)),
                  pl.BlockSpec((GDN_K_W, GDN_K_W), lambda b, i: (0, 0))],
        out_specs=pl.BlockSpec((1, tt, w), lambda b, i: (b, i, 0)),
        out_shape=jax.ShapeDtypeStruct((bg, lg, w), F32),
        compiler_params=_params("arbitrary", "arbitrary"),
        name="gdn_pre",
    )(gqkv, gqkv, gqkv, conv_w, bd_sum)


def _tri_inverse_many(a_list, same16, same32):
    n = a_list[0].shape[0]
    r = lax.broadcasted_iota(jnp.int32, (n, n), 0)
    c = lax.broadcasted_iota(jnp.int32, (n, n), 1)
    eye = (r == c).astype(F32)
    pw = [jnp.where(same16, -a, 0.0) for a in a_list]
    x = [eye + p for p in pw]
    for _ in range(3):
        pw = [_mm(p, p) for p in pw]
        x = [xi + _mm(xi, p) for xi, p in zip(x, pw)]
    for mask in (jnp.logical_and(same32, jnp.logical_not(same16)), jnp.logical_not(same32)):
        t = [_mm(jnp.where(mask, a, 0.0), xi) for a, xi in zip(a_list, x)]
        x = [xi - _mm(xi, ti) for xi, ti in zip(x, t)]
    return x


def _gdn_kernel(xf_ref, xb_ref, abf_ref, abb_ref, abtf_ref, abtb_ref, prow_ref, pcol_ref, s0_ref,
                of_ref, ob_ref, sfin_ref, s_scr, *, nt, tt):
    n = pl.program_id(1)
    ck = GDN_CHUNK
    cpt = tt // ck

    @pl.when(n == 0)
    def _():
        s_scr[...] = s0_ref[0]

    r = lax.broadcasted_iota(jnp.int32, (ck, ck), 0)
    c = lax.broadcasted_iota(jnp.int32, (ck, ck), 1)
    same16 = (r // 16) == (c // 16)
    same32 = (r // 32) == (c // 32)
    row_in = lax.broadcasted_iota(jnp.int32, (tt, LANES), 0) % ck
    lane_in = lax.broadcasted_iota(jnp.int32, (GATE_W, tt), 1) % ck

    steps = [[] for _ in range(cpt)]
    for d, (x_ref, ab_ref, abt_ref, o_ref) in enumerate(((xf_ref, abf_ref, abtf_ref, of_ref),
                                                          (xb_ref, abb_ref, abtb_ref, ob_ref))):
        rev = d == 1
        ab = ab_ref[0]
        g_col = -jnp.exp(prow_ref[0:1, :]) * _softplus(ab + prow_ref[1:2, :])
        beta_col = jax.nn.sigmoid(ab)
        abt = abt_ref[0]
        g_row = -jnp.exp(pcol_ref[:, 0:1]) * _softplus(abt + pcol_ref[:, 1:2])
        sh = 1
        while sh < ck:
            if rev:
                g_col = g_col + jnp.where(row_in < ck - sh, pltpu.roll(g_col, tt - sh, 0), 0.0)
                g_row = g_row + jnp.where(lane_in < ck - sh, pltpu.roll(g_row, tt - sh, 1), 0.0)
            else:
                g_col = g_col + jnp.where(row_in >= sh, pltpu.roll(g_col, sh, 0), 0.0)
                g_row = g_row + jnp.where(lane_in >= sh, pltpu.roll(g_row, sh, 1), 0.0)
            sh *= 2
        incl = (r <= c) if rev else (r >= c)
        strict = (r < c) if rev else (r > c)
        last = 0 if rev else ck - 1
        for i, cc in enumerate(range(cpt - 1, -1, -1) if rev else range(cpt)):
            c0 = cc * ck
            for h in range(GDN_HEADS):
                j = d * GDN_HEADS + h
                gc = g_col[c0:c0 + ck, j:j + 1]
                steps[i].append(dict(
                    j=j, o_ref=o_ref, rows=slice(c0, c0 + ck), cols=slice(h * GDN_DV, (h + 1) * GDN_DV),
                    strict=strict,
                    q=x_ref[0, c0:c0 + ck, h * GDN_DK:(h + 1) * GDN_DK],
                    k=x_ref[0, c0:c0 + ck, GDN_K_W + h * GDN_DK:GDN_K_W + (h + 1) * GDN_DK],
                    v=x_ref[0, c0:c0 + ck, 2 * GDN_K_W + h * GDN_DV:2 * GDN_K_W + (h + 1) * GDN_DV],
                    gc=gc, beta=beta_col[c0:c0 + ck, GATE_W // 2 + j:GATE_W // 2 + j + 1],
                    g_last=gc[last:last + 1, :],
                    decay=jnp.exp(jnp.where(incl, gc - g_row[j:j + 1, c0:c0 + ck], -jnp.inf))))
    items = [it for st in steps for it in st]
    for it in items:
        it["kk"] = _mm_nt(it["k"], it["k"])
    for it in items:
        it["attn"] = _mm_nt(it["q"], it["k"]) * it["decay"]
    t_inv = _tri_inverse_many([jnp.where(it["strict"], it["kk"] * it["decay"] * it["beta"], 0.0) for it in items],
                              same16, same32)
    for it, ti in zip(items, t_inv):
        egc = jnp.exp(it["gc"])
        uw = _mm(ti, jnp.concatenate([it["v"] * it["beta"], it["k"] * (it["beta"] * egc)], axis=1))
        it["u"] = uw[:, :GDN_DV]
        it["w"] = uw[:, GDN_DV:]
        it["qe"] = it["q"] * egc
        it["kd"] = it["k"] * jnp.exp(it["g_last"] - it["gc"])

    state = [s_scr[j] for j in range(N_DIR * GDN_HEADS)]
    for st in steps:
        v_new = [it["u"] - _mm(it["w"], state[it["j"]]) for it in st]
        o_state = [_mm(it["qe"], state[it["j"]]) for it in st]
        for it, vn, os_ in zip(st, v_new, o_state):
            it["o_ref"][0, it["rows"], it["cols"]] = os_ + _mm(it["attn"], vn)
        for it, vn in zip(st, v_new):
            state[it["j"]] = state[it["j"]] * jnp.exp(it["g_last"]) + _mm_tn(it["kd"], vn)
    for j, s in enumerate(state):
        s_scr[j] = s

    @pl.when(n == nt - 1)
    def _():
        sfin_ref[0] = s_scr[...]


def _gdn_mix(qkv, gab, gabt, prow, pcol, s0, *, tt):
    bg, lg, w = qkv.shape
    nt = lg // tt
    nh = N_DIR * GDN_HEADS
    fwd = lambda b, n: (b, n, 0)
    bwd = lambda b, n: (b, nt - 1 - n, 0)
    o_f, o_b, s_fin = pl.pallas_call(
        functools.partial(_gdn_kernel, nt=nt, tt=tt),
        grid=(bg, nt),
        in_specs=[pl.BlockSpec((1, tt, w), fwd), pl.BlockSpec((1, tt, w), bwd),
                  pl.BlockSpec((1, tt, LANES), fwd), pl.BlockSpec((1, tt, LANES), bwd),
                  pl.BlockSpec((1, GATE_W, tt), lambda b, n: (b, 0, n)),
                  pl.BlockSpec((1, GATE_W, tt), lambda b, n: (b, 0, nt - 1 - n)),
                  pl.BlockSpec((2, LANES), lambda b, n: (0, 0)),
                  pl.BlockSpec((GATE_W, 2), lambda b, n: (0, 0)),
                  pl.BlockSpec((1, nh, GDN_DK, GDN_DV), lambda b, n: (b, 0, 0, 0))],
        out_specs=[pl.BlockSpec((1, tt, GDN_V_W), fwd), pl.BlockSpec((1, tt, GDN_V_W), bwd),
                   pl.BlockSpec((1, nh, GDN_DK, GDN_DV), lambda b, n: (b, 0, 0, 0))],
        out_shape=[jax.ShapeDtypeStruct((bg, lg, GDN_V_W), F32), jax.ShapeDtypeStruct((bg, lg, GDN_V_W), F32),
                   jax.ShapeDtypeStruct((bg, nh, GDN_DK, GDN_DV), F32)],
        scratch_shapes=[pltpu.VMEM((nh, GDN_DK, GDN_DV), F32)],
        compiler_params=_params("arbitrary", "arbitrary"),
        name="gdn_mix",
    )(qkv, qkv, gab, gab, gabt, gabt, prow, pcol, s0)
    return o_f, o_b, s_fin


def _gelu_tanh(x):
    return 0.5 * x * (1.0 + jnp.tanh(math.sqrt(2.0 / math.pi) * (x + 0.044715 * (x * x * x))))


def _out_kernel(x_ref, ga_ref, shm_ref, scm_ref, gm_ref, attn_ref, yt_ref, of_ref, ob_ref, gz_ref,
                g2_ref, gng_ref, bd_ref, wglut_ref, bglu_ref, wo_ref, w1_ref, w2_ref, o_ref):
    x = x_ref[0]
    z = _gelu_tanh(yt_ref[...])
    gate = jax.nn.sigmoid(jnp.dot(wglut_ref[...], z.astype(BF16), preferred_element_type=F32) + bglu_ref[...])
    ssm_t = (z * gate).astype(BF16)
    o = of_ref[0] + ob_ref[0]
    gdn = o * lax.rsqrt(_seg_mean_sq(o, bd_ref) + EPS) * gng_ref[...] * _silu(gz_ref[0])
    mixed = (jnp.dot(attn_ref[0], wo_ref[:ATT_Q_W, :], preferred_element_type=F32)
             + lax.dot_general(ssm_t, wo_ref[ATT_Q_W:ATT_Q_W + SSM_WIDTH, :], _TN, preferred_element_type=F32)
             + jnp.dot(gdn.astype(BF16), wo_ref[ATT_Q_W + SSM_WIDTH:, :], preferred_element_type=F32))
    x1 = x + ga_ref[0] * mixed
    ms = jnp.mean(x1 * x1, axis=-1, keepdims=True)
    h2 = ((x1 * lax.rsqrt(ms + EPS) * g2_ref[...]) * (1.0 + scm_ref[0]) + shm_ref[0]).astype(BF16)
    acc = None
    fc = D_MODEL
    for j in range(D_FF // fc):
        a = jnp.maximum(jnp.dot(h2, w1_ref[:, j * fc:(j + 1) * fc], preferred_element_type=F32), 0.0)
        part = jnp.dot((a * a).astype(BF16), w2_ref[j * fc:(j + 1) * fc, :], preferred_element_type=F32)
        acc = part if acc is None else acc + part
    o_ref[0] = x1 + gm_ref[0] * acc


def _out_projection(x, mod3, mod_row, attn, yt, o_f, o_b, gz, p, consts, *, tm):
    bg, lg, _ = x.shape
    nt = lg // tm
    row = mod_row
    full = lambda shape: pl.BlockSpec(shape, lambda t, b: (0,) * len(shape), pipeline_mode=pl.Buffered(1))
    tok = lambda w: pl.BlockSpec((1, tm, w), lambda t, b: (b, t, 0))
    modc = lambda j: pl.BlockSpec((1, 1, D_MODEL), lambda t, b: (row(b), 0, j))
    mix_w = ATT_Q_W + SSM_WIDTH + GDN_V_W
    return pl.pallas_call(
        _out_kernel,
        grid=(nt, bg),
        in_specs=[tok(D_MODEL), modc(2), modc(3), modc(4), modc(5),
                  tok(ATT_Q_W), pl.BlockSpec((SSM_WIDTH, tm), lambda t, b: (0, b * nt + t)),
                  tok(GDN_V_W), tok(GDN_V_W), tok(GDN_V_W),
                  full((1, D_MODEL)), full((1, GDN_V_W)), full((GDN_V_W, GDN_V_W)),
                  full((SSM_WIDTH, SSM_WIDTH)), full((SSM_WIDTH, 1)),
                  full((mix_w, D_MODEL)), full((D_MODEL, D_FF)), full((D_FF, D_MODEL))],
        out_specs=tok(D_MODEL),
        out_shape=jax.ShapeDtypeStruct((bg, lg, D_MODEL), F32),
        compiler_params=_params("arbitrary", "arbitrary"),
        name="out_projection",
    )(x, mod3, mod3, mod3, mod3, attn, yt, o_f, o_b, gz,
      p["norm2_g"], p["gdn_norm_g"], consts["bd_mean"][:GDN_V_W, :GDN_V_W],
      p["w_glu_t"], p["b_glu"], p["w_out"], p["w_ff1"], p["w_ff2"])


def _constants(max_len):
    n_freq = HEAD_DIM // 4
    rows = jnp.repeat(jnp.arange(max_len // GRID_W, dtype=F32), GRID_W)
    cols = jnp.tile(jnp.arange(GRID_W, dtype=F32), max_len // GRID_W)
    inv_freq = jnp.power(ROPE_BASE, -jnp.arange(n_freq, dtype=F32) / n_freq)
    ang = jnp.concatenate([rows[:, None] * inv_freq, cols[:, None] * inv_freq], axis=-1)
    cos = jnp.repeat(jnp.cos(ang), 2, axis=-1)
    sin = jnp.repeat(jnp.sin(ang), 2, axis=-1) * jnp.tile(jnp.array([-1.0, 1.0], F32), HEAD_DIM // 2)
    seg = jnp.arange(ATT_Q_W) // HEAD_DIM
    same = seg[:, None] == seg[None, :]
    return {
        "cos": jnp.tile(cos, (1, LANES // HEAD_DIM)),
        "sin": jnp.tile(sin, (1, LANES // HEAD_DIM)),
        "bd_mean": jnp.where(same, 1.0 / HEAD_DIM, 0.0).astype(BF16),
    }


def _layer_params(l, w):
    w_in = w["w_in"][l]
    off = [0]
    for size in (ATT_Q_W, ATT_KV_W, ATT_KV_W, SSM_WIDTH, GDN_K_W, GDN_K_W, GDN_V_W, GDN_V_W, GATE_W):
        off.append(off[-1] + size)
    wab = w_in[:, off[8]:off[9]]
    col = lambda a: a.reshape(-1, 1)
    pad_lanes = lambda a: jnp.pad(a, ((0, 0), (0, LANES - a.shape[1])))
    alog = w["gdn_a_log"][l].reshape(1, -1)
    dtb = w["gdn_dt_bias"][l].reshape(1, -1)
    zeros = jnp.zeros_like(alog)
    ls = jnp.broadcast_to(w["ssm_log_step"][l][..., None], (N_DIR, SSM_GROUPS, SSM_STATE))
    pair_order = jnp.arange(N_HEADS).reshape(N_KV_HEADS, GQA).T.reshape(-1)
    head_cols = (pair_order[:, None] * HEAD_DIM + jnp.arange(HEAD_DIM)[None, :]).reshape(-1)
    w_out = w["w_out"][l]
    w_out = jnp.concatenate([w_out[:ATT_Q_W][head_cols], w_out[ATT_Q_W:]], axis=0)
    return {
        "norm1_g": w["norm1_g"][l].reshape(1, -1), "norm2_g": w["norm2_g"][l].reshape(1, -1),
        "wq": w_in[:, off[0]:off[1]][:, head_cols].astype(BF16), "wkv": w_in[:, off[1]:off[3]].astype(BF16),
        "wut": w_in[:, off[3]:off[4]].T.astype(BF16), "wg": w_in[:, off[4]:off[7]].astype(BF16),
        "wz": w_in[:, off[7]:off[8]].astype(BF16),
        "wab": pad_lanes(wab).astype(BF16), "wabt": wab.T.astype(BF16),
        "q_norm_g": jnp.tile(w["q_norm_g"][l], N_HEADS).reshape(1, -1),
        "k_norm_g": jnp.tile(w["k_norm_g"][l], N_KV_HEADS).reshape(1, -1),
        "attn_sink": w["attn_sink"][l],
        "ssm_pc": jnp.stack([w["ssm_lam_re"][l], w["ssm_lam_im"][l], ls], axis=-1),
        "ssm_pr": jnp.stack([w["ssm_lam_re"][l], w["ssm_lam_im"][l], ls], axis=-2),
        "ssm_b_re": w["ssm_b_re"][l], "ssm_b_im": w["ssm_b_im"][l],
        "ssm_c_re": w["ssm_c_re"][l], "ssm_c_im": w["ssm_c_im"][l],
        "ssm_c_re_t": jnp.swapaxes(w["ssm_c_re"][l], -1, -2), "ssm_c_im_t": jnp.swapaxes(w["ssm_c_im"][l], -1, -2),
        "ssm_d": w["ssm_d"][l],
        "w_glu_t": w["ssm_w_glu"][l].T.astype(BF16), "b_glu": col(w["ssm_b_glu"][l]),
        "gdn_conv_w": w["gdn_conv_w"][l],
        "gdn_prow": pad_lanes(jnp.concatenate([jnp.concatenate([alog, zeros], 1), jnp.concatenate([dtb, zeros], 1)], 0)),
        "gdn_pcol": jnp.concatenate([jnp.concatenate([alog, zeros], 1), jnp.concatenate([dtb, zeros], 1)], 0).T,
        "gdn_norm_g": jnp.tile(w["gdn_norm_g"][l], GDN_HEADS).reshape(1, -1),
        "w_out": w_out.astype(BF16), "w_ff1": w["w_ff1"][l].astype(BF16), "w_ff2": w["w_ff2"][l].astype(BF16),
    }


def _state_to_lanes(h0):
    bg = h0.shape[0]
    return h0.transpose(3, 0, 2, 1, 4).reshape(SSM_GROUPS, bg, 2 * N_DIR * SSM_STATE)


def _lanes_to_state(h):
    bg = h.shape[1]
    return h.reshape(SSM_GROUPS, bg, 2, N_DIR, SSM_STATE).transpose(1, 3, 2, 0, 4)


def _layer(groups, mod3, p, consts, layer):
    tables = _ssm_prep(p)
    proj = []
    for gr in groups:
        lg = gr["x"].shape[1]
        proj.append(_in_projection(gr["x"], mod3, gr["mod_row"], p, consts, rope=gr["ctx_kv"] is not None,
                                   tm=min(lg, 512)))
    shapes = [gr["x"].shape[:2] for gr in groups]
    yts, h_fins = _ssm_mix([pr[3] for pr in proj], tables, [_state_to_lanes(gr["ssm_h0"]) for gr in groups],
                           p["ssm_d"], shapes)
    xs, aux = [], []
    for gr, pr, yt, h_fin in zip(groups, proj, yts, h_fins):
        bg, lg, _ = gr["x"].shape
        q, k, v, _, gqkv, gz, gab, gabt = pr
        if gr["ctx_kv"] is not None:
            attn = _latent_attention(q, k, v, gr["ctx_kv"][0], gr["ctx_kv"][1], layer, p["attn_sink"])
        else:
            attn = _context_attention(q, k, v, p["attn_sink"])
        qkv = _gdn_pre(gqkv, p["gdn_conv_w"], consts["bd_mean"][:GDN_K_W, :GDN_K_W], tt=min(lg, 512))
        o_f, o_b, s_fin = _gdn_mix(qkv, gab, gabt, p["gdn_prow"], p["gdn_pcol"],
                                   gr["gdn_s0"].reshape(bg, N_DIR * GDN_HEADS, GDN_DK, GDN_DV), tt=min(lg, 128))
        xs.append(_out_projection(gr["x"], mod3, gr["mod_row"], attn, yt, o_f, o_b, gz, p, consts, tm=min(lg, 512)))
        aux.append((k, v, _lanes_to_state(h_fin), s_fin.reshape(bg, N_DIR, GDN_HEADS, GDN_DK, GDN_DV)))
    return xs, aux


def kernel(x_prompt, x_sample, c, cache_k, cache_v, state_ssm, state_gdn, c_ctx, norm1_g, norm2_g, w_mod, b_mod, w_in, q_norm_g, k_norm_g, attn_sink, ssm_lam_re, ssm_lam_im, ssm_log_step, ssm_b_re, ssm_b_im, ssm_c_re, ssm_c_im, ssm_d, ssm_w_glu, ssm_b_glu, gdn_conv_w, gdn_a_log, gdn_dt_bias, gdn_norm_g, w_out, w_ff1, w_ff2):
    w = dict(norm1_g=norm1_g, norm2_g=norm2_g, w_in=w_in, q_norm_g=q_norm_g, k_norm_g=k_norm_g, attn_sink=attn_sink,
             ssm_lam_re=ssm_lam_re, ssm_lam_im=ssm_lam_im, ssm_log_step=ssm_log_step, ssm_b_re=ssm_b_re,
             ssm_b_im=ssm_b_im, ssm_c_re=ssm_c_re, ssm_c_im=ssm_c_im, ssm_d=ssm_d, ssm_w_glu=ssm_w_glu,
             ssm_b_glu=ssm_b_glu, gdn_conv_w=gdn_conv_w, gdn_a_log=gdn_a_log, gdn_dt_bias=gdn_dt_bias,
             gdn_norm_g=gdn_norm_g, w_out=w_out, w_ff1=w_ff1, w_ff2=w_ff2)
    n_ctx, seq, _ = x_prompt.shape
    n_dec, dec_seq, _ = x_sample.shape
    depth = w_in.shape[0]
    past = cache_k.shape[2]

    n_rows = -(-(n_dec + 1) // SUBLANES) * SUBLANES
    cond = jnp.zeros((n_rows, D_MODEL), F32).at[:n_dec].set(c).at[n_dec].set(c_ctx)
    mod = _modulation(cond, w_mod, b_mod)

    consts = _constants(max(seq, dec_seq))
    cache_k4 = cache_k.reshape(n_dec, depth, past, ATT_KV_W)
    cache_v4 = cache_v.reshape(n_dec, depth, past, ATT_KV_W)
    ssm_zero = jnp.zeros((n_ctx, N_DIR, 2, SSM_GROUPS, SSM_STATE), F32)
    gdn_zero = jnp.zeros((n_ctx, N_DIR, GDN_HEADS, GDN_DK, GDN_DV), F32)

    xp, xs = x_prompt, x_sample
    ks, vs, ss, gs = [], [], [], []
    for l in range(depth):
        p = _layer_params(l, w)
        mod3 = mod[l].reshape(n_rows, 1, N_MOD * D_MODEL)
        groups = [dict(x=xp, mod_row=lambda b: n_dec, ctx_kv=None, ssm_h0=ssm_zero, gdn_s0=gdn_zero),
                  dict(x=xs, mod_row=lambda b: b, ctx_kv=(cache_k4, cache_v4), ssm_h0=state_ssm[:, l],
                       gdn_s0=state_gdn[:, l])]
        (xp, xs), ((k_l, v_l, s_l, g_l), _) = _layer(groups, mod3, p, consts, l)
        ks.append(k_l.reshape(n_ctx, seq, N_KV_HEADS, HEAD_DIM))
        vs.append(v_l.reshape(n_ctx, seq, N_KV_HEADS, HEAD_DIM))
        ss.append(s_l)
        gs.append(g_l)
    return (xp, xs, jnp.stack(ks, axis=1), jnp.stack(vs, axis=1), jnp.stack(ss, axis=1), jnp.stack(gs, axis=1))
```

```python
import functools
import math

import jax
import jax.numpy as jnp
from jax import lax
from jax.experimental import pallas as pl
from jax.experimental.pallas import tpu as pltpu

F32 = jnp.float32
BF16 = jnp.bfloat16
EPS = 1e-6
NEG_INF = -1e30

D_MODEL = 1024
DEPTH = 4
GRID_W = 64
N_DIR = 2
N_HEADS = 8
N_KV_HEADS = 2
GQA = N_HEADS // N_KV_HEADS
HEAD_DIM = 64
ATT_BLOCK = 128
ROPE_BASE = 10000.0
SSM_GROUP_CH = 16
SSM_GROUPS = 16
SSM_WIDTH = SSM_GROUPS * SSM_GROUP_CH
SSM_STATE = 64
GDN_HEADS = 4
GDN_DK = 64
GDN_DV = 64
GDN_K_W = GDN_HEADS * GDN_DK
GDN_V_W = GDN_HEADS * GDN_DV
GDN_QKV_W = 2 * GDN_K_W + GDN_V_W
GDN_CHUNK = 64
ATT_Q_W = N_HEADS * HEAD_DIM
ATT_KV_W = N_KV_HEADS * HEAD_DIM
D_FF = 4 * D_MODEL
N_MOD = 6
GATE_W = 2 * N_DIR * GDN_HEADS

LANES = 128
SUBLANES = 8
SSM_T = LANES
GDN_TILE = 512
VMEM_LIMIT = 56 * 1024 * 1024

_NT = (((1,), (1,)), ((), ()))
_TN = (((0,), (0,)), ((), ()))


def _mm(a, b):
    return jnp.dot(a.astype(BF16), b.astype(BF16), preferred_element_type=F32)


def _mm_nt(a, b):
    return lax.dot_general(a.astype(BF16), b.astype(BF16), _NT, preferred_element_type=F32)


def _mm_tn(a, b):
    return lax.dot_general(a.astype(BF16), b.astype(BF16), _TN, preferred_element_type=F32)


def _split3(a):
    hi = a.astype(BF16)
    r1 = a - hi.astype(F32)
    mid = r1.astype(BF16)
    lo = (r1 - mid.astype(F32)).astype(BF16)
    return hi, mid, lo


def _mm_exact_rhs(a, b_bf16):
    hi, mid, lo = _split3(a)
    return (jnp.dot(hi, b_bf16, preferred_element_type=F32)
            + jnp.dot(mid, b_bf16, preferred_element_type=F32)
            + jnp.dot(lo, b_bf16, preferred_element_type=F32))


def _mm_f32(a, b):
    ah, am, al = _split3(a)
    bh, bm, bl = _split3(b)
    d = lambda x, y: jnp.dot(x, y, preferred_element_type=F32)
    return (d(ah, bh) + (d(ah, bm) + d(am, bh)) + (d(ah, bl) + d(am, bm) + d(al, bh)))


def _silu(x):
    return x * jax.nn.sigmoid(x)


def _softplus(x):
    return jnp.maximum(x, 0.0) + jnp.log1p(jnp.exp(-jnp.abs(x)))


def _params(*sem):
    return pltpu.CompilerParams(dimension_semantics=sem, vmem_limit_bytes=VMEM_LIMIT)


def _mod_kernel(cond_ref, w_ref, b_ref, o_ref):
    c = cond_ref[...]
    o_ref[0] = _mm(_silu(c), w_ref[0]) + b_ref[0]


def _modulation(cond, w_mod, b_mod):
    rows = cond.shape[0]
    cn = 1536
    return pl.pallas_call(
        _mod_kernel,
        grid=(DEPTH, N_MOD * D_MODEL // cn),
        in_specs=[pl.BlockSpec((rows, D_MODEL), lambda l, j: (0, 0)),
                  pl.BlockSpec((1, D_MODEL, cn), lambda l, j: (l, 0, j)),
                  pl.BlockSpec((1, 1, cn), lambda l, j: (l, 0, j))],
        out_specs=pl.BlockSpec((1, rows, cn), lambda l, j: (l, 0, j)),
        out_shape=jax.ShapeDtypeStruct((DEPTH, rows, N_MOD * D_MODEL), F32),
        compiler_params=_params("arbitrary", "arbitrary"),
        name="modulation",
    )(cond, w_mod, b_mod.reshape(DEPTH, 1, N_MOD * D_MODEL))


def _seg_mean_sq(x, bd_ref):
    n = x.shape[-1]
    return jnp.dot((x * x).astype(BF16), bd_ref[:n, :n], preferred_element_type=F32)


def _rope(x, cos, sin_signed):
    n = x.shape[-1]
    nxt = pltpu.roll(x, n - 1, 1)
    prv = pltpu.roll(x, 1, 1)
    lane = lax.broadcasted_iota(jnp.int32, x.shape, 1)
    swapped = jnp.where(lane % 2 == 0, nxt, prv)
    return x * cos + swapped * sin_signed


def _in_kernel(x_ref, sh_ref, sc_ref, g1_ref, wq_ref, wkv_ref, wut_ref, wg_ref, wz_ref, wab_ref, wabt_ref,
               qg_ref, kg_ref, bd_ref, cos_ref, sin_ref,
               q_out, k_out, v_out, ut_out, g_out, z_out, ab_out, abt_out, *, rope):
    x = x_ref[0]
    ms = jnp.mean(x * x, axis=-1, keepdims=True)
    h = (x * lax.rsqrt(ms + EPS) * g1_ref[...]) * (1.0 + sc_ref[0]) + sh_ref[0]
    hb = h.astype(BF16)

    q = jnp.dot(hb, wq_ref[...], preferred_element_type=F32)
    q = q * lax.rsqrt(_seg_mean_sq(q, bd_ref) + EPS) * qg_ref[...]
    kv = jnp.dot(hb, wkv_ref[...], preferred_element_type=F32)
    k = kv[:, :ATT_KV_W]
    k = k * lax.rsqrt(_seg_mean_sq(k, bd_ref) + EPS) * kg_ref[...]
    if rope:
        cos = cos_ref[...]
        sin = sin_ref[...]
        q = _rope(q, jnp.concatenate([cos] * (ATT_Q_W // LANES), axis=1),
                  jnp.concatenate([sin] * (ATT_Q_W // LANES), axis=1))
        k = _rope(k, cos, sin)
    q_out[0] = (q * (HEAD_DIM ** -0.5)).astype(BF16)
    k_out[0] = k
    v_out[0] = kv[:, ATT_KV_W:]
    ut_out[...] = lax.dot_general(wut_ref[...], hb, _NT, preferred_element_type=F32)
    g_out[0] = jnp.dot(hb, wg_ref[...], preferred_element_type=F32)
    z_out[0] = jnp.dot(hb, wz_ref[...], preferred_element_type=F32)
    ab_out[0] = jnp.dot(hb, wab_ref[...], preferred_element_type=F32)
    abt_out[0] = lax.dot_general(wabt_ref[...], hb, _NT, preferred_element_type=F32)


def _in_projection(x, mod3, mod_row, p, consts, *, rope, tm):
    bg, lg, _ = x.shape
    nt = lg // tm
    row = mod_row
    full = lambda shape: pl.BlockSpec(shape, lambda t, b: (0,) * len(shape))
    tok = lambda w: pl.BlockSpec((1, tm, w), lambda t, b: (b, t, 0))
    in_specs = [
        tok(D_MODEL),
        pl.BlockSpec((1, 1, D_MODEL), lambda t, b: (row(b), 0, 0)),
        pl.BlockSpec((1, 1, D_MODEL), lambda t, b: (row(b), 0, 1)),
        full((1, D_MODEL)),
        full((D_MODEL, ATT_Q_W)), full((D_MODEL, 2 * ATT_KV_W)), full((SSM_WIDTH, D_MODEL)),
        full((D_MODEL, GDN_QKV_W)), full((D_MODEL, GDN_V_W)), full((D_MODEL, LANES)), full((GATE_W, D_MODEL)),
        full((1, ATT_Q_W)), full((1, ATT_KV_W)), full((ATT_Q_W, ATT_Q_W)),
        pl.BlockSpec((tm, LANES), lambda t, b: (t, 0)),
        pl.BlockSpec((tm, LANES), lambda t, b: (t, 0)),
    ]
    out_specs = [
        tok(ATT_Q_W), tok(ATT_KV_W), tok(ATT_KV_W),
        pl.BlockSpec((SSM_WIDTH, tm), lambda t, b: (0, b * nt + t)),
        tok(GDN_QKV_W), tok(GDN_V_W), tok(LANES),
        pl.BlockSpec((1, GATE_W, tm), lambda t, b: (b, 0, t)),
    ]
    out_shape = [
        jax.ShapeDtypeStruct((bg, lg, ATT_Q_W), BF16),
        jax.ShapeDtypeStruct((bg, lg, ATT_KV_W), F32),
        jax.ShapeDtypeStruct((bg, lg, ATT_KV_W), F32),
        jax.ShapeDtypeStruct((SSM_WIDTH, bg * lg), F32),
        jax.ShapeDtypeStruct((bg, lg, GDN_QKV_W), F32),
        jax.ShapeDtypeStruct((bg, lg, GDN_V_W), F32),
        jax.ShapeDtypeStruct((bg, lg, LANES), F32),
        jax.ShapeDtypeStruct((bg, GATE_W, lg), F32),
    ]
    return pl.pallas_call(
        functools.partial(_in_kernel, rope=rope),
        grid=(nt, bg), in_specs=in_specs, out_specs=out_specs, out_shape=out_shape,
        compiler_params=_params("arbitrary", "arbitrary"),
        name="in_projection",
    )(x, mod3, mod3, p["norm1_g"], p["wq"], p["wkv"], p["wut"], p["wg"], p["wz"], p["wab"], p["wabt"],
      p["q_norm_g"], p["k_norm_g"], consts["bd_mean"], consts["cos"][:lg], consts["sin"][:lg])


def _attend(q_ref, k_all, v_all, masks, sink_ref, o_ref):
    nq = q_ref.shape[1]
    lane = lax.broadcasted_iota(jnp.int32, (1, LANES), 1)
    low = lane < HEAD_DIM
    keep_low = low.astype(BF16)
    keep_high = 1 - keep_low
    tiles = [q_ref[0, :, i * LANES:(i + 1) * LANES] for i in range(GQA)]
    q_rows = jnp.concatenate([t * keep_low for t in tiles] + [t * keep_high for t in tiles], axis=0)
    s = lax.dot_general(q_rows, k_all, _NT, preferred_element_type=F32)
    n_keys = s.shape[1]
    probs, inv_den = [], []
    for h in range(N_HEADS):
        sh = s[h * nq:(h + 1) * nq, :]
        if masks:
            cols = []
            for c0 in range(0, n_keys, ATT_BLOCK):
                piece = sh[:, c0:c0 + ATT_BLOCK]
                cols.append(jnp.where(masks[c0], piece, NEG_INF) if c0 in masks else piece)
            sh = jnp.concatenate(cols, axis=1)
        sink = sink_ref[h]
        m = jnp.maximum(jnp.max(sh, axis=-1, keepdims=True), sink)
        p = jnp.exp(sh - m)
        inv_den.append(1.0 / (jnp.exp(sink - m) + jnp.sum(p, axis=-1, keepdims=True)))
        probs.append(p.astype(BF16))
    o = jnp.dot(jnp.concatenate(probs, axis=0), v_all, preferred_element_type=F32)
    for i in range(GQA):
        o_lo = o[i * nq:(i + 1) * nq, :] * inv_den[i]
        o_hi = o[(GQA + i) * nq:(GQA + i + 1) * nq, :] * inv_den[GQA + i]
        o_ref[0, :, i * LANES:(i + 1) * LANES] = jnp.where(low, o_lo, o_hi).astype(o_ref.dtype)


def _latent_attn_kernel(q_ref, kp_ref, kc_ref, kn_ref, vp_ref, vc_ref, vn_ref, ck_ref, cv_ref, sink_ref, o_ref, *, nb):
    i = pl.program_id(1)
    r = lax.broadcasted_iota(jnp.int32, (ATT_BLOCK, ATT_BLOCK), 0)
    c = lax.broadcasted_iota(jnp.int32, (ATT_BLOCK, ATT_BLOCK), 1)
    masks = {0: jnp.logical_and(c >= r, i > 0), 2 * ATT_BLOCK: jnp.logical_and(c <= r, i < nb - 1)}
    k_all = jnp.concatenate([kp_ref[0], kc_ref[0], kn_ref[0], ck_ref[0, 0]], axis=0).astype(BF16)
    v_all = jnp.concatenate([vp_ref[0], vc_ref[0], vn_ref[0], cv_ref[0, 0]], axis=0).astype(BF16)
    _attend(q_ref, k_all, v_all, masks, sink_ref, o_ref)


def _latent_attention(q, k, v, cache_k4, cache_v4, layer, sink):
    bg, lg, _ = q.shape
    nb = lg // ATT_BLOCK
    past = cache_k4.shape[2]
    blk = lambda w, f: pl.BlockSpec((1, ATT_BLOCK, w), f)
    prev = lambda b, i: (b, jnp.maximum(i - 1, 0), 0)
    cur = lambda b, i: (b, i, 0)
    nxt = lambda b, i: (b, jnp.minimum(i + 1, nb - 1), 0)
    ctx = pl.BlockSpec((1, 1, past, ATT_KV_W), lambda b, i: (b, layer, 0, 0))
    return pl.pallas_call(
        functools.partial(_latent_attn_kernel, nb=nb),
        grid=(bg, nb),
        in_specs=[blk(ATT_Q_W, cur), blk(ATT_KV_W, prev), blk(ATT_KV_W, cur), blk(ATT_KV_W, nxt),
                  blk(ATT_KV_W, prev), blk(ATT_KV_W, cur), blk(ATT_KV_W, nxt), ctx, ctx,
                  pl.BlockSpec(memory_space=pltpu.SMEM)],
        out_specs=blk(ATT_Q_W, cur),
        out_shape=jax.ShapeDtypeStruct((bg, lg, ATT_Q_W), BF16),
        compiler_params=_params("arbitrary", "arbitrary"),
        name="latent_attention",
    )(q, k, k, k, v, v, v, cache_k4, cache_v4, sink)


def _context_attn_kernel(q_ref, k_ref, v_ref, sink_ref, o_ref):
    _attend(q_ref, k_ref[0].astype(BF16), v_ref[0].astype(BF16), {}, sink_ref, o_ref)


def _context_attention(q, k, v, sink):
    bg, lg, _ = q.shape
    nb = lg // ATT_BLOCK
    return pl.pallas_call(
        _context_attn_kernel,
        grid=(bg, nb),
        in_specs=[pl.BlockSpec((1, ATT_BLOCK, ATT_Q_W), lambda b, i: (b, i, 0)),
                  pl.BlockSpec((1, lg, ATT_KV_W), lambda b, i: (b, 0, 0)),
                  pl.BlockSpec((1, lg, ATT_KV_W), lambda b, i: (b, 0, 0)),
                  pl.BlockSpec(memory_space=pltpu.SMEM)],
        out_specs=pl.BlockSpec((1, ATT_BLOCK, ATT_Q_W), lambda b, i: (b, i, 0)),
        out_shape=jax.ShapeDtypeStruct((bg, lg, ATT_Q_W), BF16),
        compiler_params=_params("arbitrary", "arbitrary"),
        name="context_attention",
    )(q, k, v, sink)


def _powers(er, th, e):
    mag = jnp.exp(er * e)
    return mag * jnp.cos(th * e), mag * jnp.sin(th * e)


def _ssm_prep_kernel(pc_ref, pr_ref, br_ref, bi_ref, cr_ref, ci_ref, crt_ref, cit_ref,
                     kv_out, rs_out, f_out, at_out):
    t = SSM_T
    p = SSM_STATE
    lane2 = lax.broadcasted_iota(jnp.int32, (1, 2 * t), 1)
    lane1 = lax.broadcasted_iota(jnp.int32, (1, t), 1)
    e_kv = [jnp.maximum(lane2 - t, 0).astype(F32), jnp.maximum(t - lane2, 0).astype(F32)]
    m_kv = [lane2 >= t, jnp.logical_and(lane2 >= 1, lane2 <= t)]
    e_rs = [(t - 1 - lane1).astype(F32), lane1.astype(F32)]
    e_f = [(lane1 + 1).astype(F32), (t - lane1).astype(F32)]

    bbr, bbi, pk, prs, pf = [], [], [], [], []
    for d in range(N_DIR):
        lr = pc_ref[d, 0, :, 0:1]
        li = pc_ref[d, 0, :, 1:2]
        step = jnp.exp(pc_ref[d, 0, :, 2:3])
        er = lr * step
        th = li * step
        mag = jnp.exp(er)
        ar = mag * jnp.cos(th)
        ai = mag * jnp.sin(th)
        den = lr * lr + li * li
        nr = ar - 1.0
        fr = (nr * lr + ai * li) / den
        fi = (ai * lr - nr * li) / den
        b_re = br_ref[d, 0]
        b_im = bi_ref[d, 0]
        bbr.append(fr * b_re - fi * b_im)
        bbi.append(fr * b_im + fi * b_re)
        kr, ki = _powers(er, th, e_kv[d])
        pk.append((jnp.where(m_kv[d], kr, 0.0), jnp.where(m_kv[d], ki, 0.0)))
        prs.append(_powers(er, th, e_rs[d]))
        pf.append(_powers(er, th, e_f[d]))
        lr_r = pr_ref[d, 0, 0:1, :]
        li_r = pr_ref[d, 0, 1:2, :]
        step_r = jnp.exp(pr_ref[d, 0, 2:3, :])
        mag_t = jnp.exp(lr_r * step_r * float(t))
        at_out[0, :, d * p:(d + 1) * p] = mag_t * jnp.cos(li_r * step_r * float(t))
        at_out[0, :, (N_DIR + d) * p:(N_DIR + d + 1) * p] = mag_t * jnp.sin(li_r * step_r * float(t))

    cmat = jnp.concatenate([cr_ref[0, 0], -ci_ref[0, 0], cr_ref[1, 0], -ci_ref[1, 0]], axis=1)
    for ci in range(SSM_GROUP_CH):
        rows = []
        for d in range(N_DIR):
            cbr = bbr[d][:, ci:ci + 1]
            cbi = bbi[d][:, ci:ci + 1]
            kr, ki = pk[d]
            rows += [kr * cbr - ki * cbi, kr * cbi + ki * cbr]
        kv_out[0, ci * SSM_GROUP_CH:(ci + 1) * SSM_GROUP_CH, :] = _mm_f32(cmat, jnp.concatenate(rows, axis=0))
        for d in range(N_DIR):
            cbr = bbr[d][:, ci:ci + 1]
            cbi = bbi[d][:, ci:ci + 1]
            sr, si = prs[d]
            rs_out[0, d * p:(d + 1) * p, ci * t:(ci + 1) * t] = (sr * cbr - si * cbi).astype(BF16)
            rs_out[0, (N_DIR + d) * p:(N_DIR + d + 1) * p, ci * t:(ci + 1) * t] = (sr * cbi + si * cbr).astype(BF16)
    for co in range(SSM_GROUP_CH):
        for d in range(N_DIR):
            ccr = crt_ref[d, 0, :, co:co + 1]
            cci = cit_ref[d, 0, :, co:co + 1]
            fr_, fi_ = pf[d]
            f_out[0, d * p:(d + 1) * p, co * t:(co + 1) * t] = (ccr * fr_ - cci * fi_).astype(BF16)
            f_out[0, (N_DIR + d) * p:(N_DIR + d + 1) * p, co * t:(co + 1) * t] = (-(ccr * fi_ + cci * fr_)).astype(BF16)


def _ssm_prep(p):
    t = SSM_T
    g = SSM_GROUPS
    ns = N_DIR * 2 * SSM_STATE
    spec4 = lambda a, b: pl.BlockSpec((N_DIR, 1, a, b), lambda i: (0, i, 0, 0))
    return pl.pallas_call(
        _ssm_prep_kernel,
        grid=(g,),
        in_specs=[spec4(SSM_STATE, 3), spec4(3, SSM_STATE),
                  spec4(SSM_STATE, SSM_GROUP_CH), spec4(SSM_STATE, SSM_GROUP_CH),
                  spec4(SSM_GROUP_CH, SSM_STATE), spec4(SSM_GROUP_CH, SSM_STATE),
                  spec4(SSM_STATE, SSM_GROUP_CH), spec4(SSM_STATE, SSM_GROUP_CH)],
        out_specs=[pl.BlockSpec((1, SSM_GROUP_CH * SSM_GROUP_CH, 2 * t), lambda i: (i, 0, 0)),
                   pl.BlockSpec((1, ns, SSM_GROUP_CH * t), lambda i: (i, 0, 0)),
                   pl.BlockSpec((1, ns, SSM_GROUP_CH * t), lambda i: (i, 0, 0)),
                   pl.BlockSpec((1, 1, ns), lambda i: (i, 0, 0))],
        out_shape=[jax.ShapeDtypeStruct((g, SSM_GROUP_CH * SSM_GROUP_CH, 2 * t), F32),
                   jax.ShapeDtypeStruct((g, ns, SSM_GROUP_CH * t), BF16),
                   jax.ShapeDtypeStruct((g, ns, SSM_GROUP_CH * t), BF16),
                   jax.ShapeDtypeStruct((g, 1, ns), F32)],
        compiler_params=_params("arbitrary"),
        name="ssm_prep",
    )(p["ssm_pc"], p["ssm_pr"], p["ssm_b_re"], p["ssm_b_im"], p["ssm_c_re"], p["ssm_c_im"],
      p["ssm_c_re_t"], p["ssm_c_im_t"])


def _ssm_kernel(*refs, dims):
    ng = len(dims)
    u_refs = refs[:ng]
    kv_ref, rs_ref, f_ref, at_ref = refs[ng:ng + 4]
    h0_refs = refs[ng + 4:2 * ng + 4]
    d_ref = refs[2 * ng + 4]
    y_outs = refs[2 * ng + 5:3 * ng + 5]
    hfin_outs = refs[3 * ng + 5:4 * ng + 5]
    m_scr = refs[4 * ng + 5]
    scratch = refs[4 * ng + 6:]
    t = SSM_T
    nch = SSM_GROUP_CH
    g = pl.program_id(0)
    half = N_DIR * SSM_STATE

    def gen(ci, carry):
        r0 = pl.multiple_of(ci * t, t)
        for co in range(nch):
            row = kv_ref[0, pl.ds(ci * nch + co, 1), :]
            rolled = pltpu.roll(jnp.broadcast_to(row, (t, 2 * t)), t, 1, stride=1, stride_axis=0)
            m_scr[pl.ds(r0, t), co * t:(co + 1) * t] = rolled[:, :t].astype(BF16)
        return carry
    lax.fori_loop(0, nch, gen, 0)

    a_re = at_ref[0, :, :half]
    a_im = at_ref[0, :, half:]
    is_fwd = lax.broadcasted_iota(jnp.int32, (1, half), 1) < SSM_STATE
    for gi, (bg, nc) in enumerate(dims):
        u_ref, h0_ref, y_out, hfin_out = u_refs[gi], h0_refs[gi], y_outs[gi], hfin_outs[gi]
        s_scr, hpf_scr, hpb_scr = scratch[3 * gi:3 * gi + 3]
        u_flat = jnp.concatenate([u_ref[0, ci].astype(BF16) for ci in range(nch)], axis=1)

        s = lax.dot_general(u_flat, rs_ref[0], _NT, preferred_element_type=F32)
        s_scr[0] = s[:, :half]
        s_scr[1] = s[:, half:]

        def advance(re, im, rows, s_scr=s_scr):
            return (a_re * re - a_im * im + s_scr[0, rows, :], a_re * im + a_im * re + s_scr[1, rows, :])

        def scan(i, carry, bg=bg, nc=nc, hpf_scr=hpf_scr, hpb_scr=hpb_scr, advance=advance):
            f_re, f_im, b_re, b_im = carry
            rows_f = pl.ds(i, bg, stride=nc)
            rows_b = pl.ds(nc - 1 - i, bg, stride=nc)
            hpf_scr[0, rows_f, :] = f_re
            hpf_scr[1, rows_f, :] = f_im
            hpb_scr[0, rows_b, :] = b_re
            hpb_scr[1, rows_b, :] = b_im
            return advance(f_re, f_im, rows_f) + advance(b_re, b_im, rows_b)
        h0_re = h0_ref[0, :, :half]
        h0_im = h0_ref[0, :, half:]
        f_re, f_im, b_re, b_im = lax.fori_loop(0, nc, scan, (h0_re, h0_im, h0_re, h0_im))
        hfin_out[0, :, :half] = jnp.where(is_fwd, f_re, b_re)
        hfin_out[0, :, half:] = jnp.where(is_fwd, f_im, b_im)
        hprev = jnp.concatenate([jnp.where(is_fwd, hpf_scr[0], hpb_scr[0]),
                                 jnp.where(is_fwd, hpf_scr[1], hpb_scr[1])], axis=1)

        y = (jnp.dot(hprev.astype(BF16), f_ref[0], preferred_element_type=F32)
             + jnp.dot(u_flat, m_scr[...], preferred_element_type=F32))
        for co in range(nch):
            y_out[0, co] = y[:, co * t:(co + 1) * t] + d_ref[g * nch + co] * u_ref[0, co]


def _ssm_mix(uts, tables, h0s, d_skip, shapes):
    t = SSM_T
    g = SSM_GROUPS
    nch = SSM_GROUP_CH
    ns = N_DIR * 2 * SSM_STATE
    kv, rs, f, at = tables
    dims = tuple((bg, lg // t) for bg, lg in shapes)
    u_specs = [pl.BlockSpec((1, nch, bg * nc, t), lambda i: (i, 0, 0, 0)) for bg, nc in dims]
    h_specs = [pl.BlockSpec((1, bg, ns), lambda i: (i, 0, 0)) for bg, _ in dims]
    scratch = [pltpu.VMEM((nch * t, nch * t), BF16)]
    for bg, nc in dims:
        scratch += [pltpu.VMEM((2, bg * nc, ns // 2), F32)] * 3
    outs = pl.pallas_call(
        functools.partial(_ssm_kernel, dims=dims),
        grid=(g,),
        in_specs=u_specs + [pl.BlockSpec((1, nch * nch, 2 * t), lambda i: (i, 0, 0)),
                            pl.BlockSpec((1, ns, nch * t), lambda i: (i, 0, 0)),
                            pl.BlockSpec((1, ns, nch * t), lambda i: (i, 0, 0)),
                            pl.BlockSpec((1, 1, ns), lambda i: (i, 0, 0))]
                 + h_specs + [pl.BlockSpec(memory_space=pltpu.SMEM)],
        out_specs=u_specs + h_specs,
        out_shape=[jax.ShapeDtypeStruct((g, nch, bg * nc, t), F32) for bg, nc in dims]
                  + [jax.ShapeDtypeStruct((g, bg, ns), F32) for bg, _ in dims],
        scratch_shapes=scratch,
        compiler_params=_params("arbitrary"),
        name="ssm_mix",
    )(*[ut.reshape(g, nch, bg * nc, t) for ut, (bg, nc) in zip(uts, dims)], kv, rs, f, at, *h0s, d_skip)
    n = len(dims)
    return ([y.reshape(g * nch, bg * nc * t) for y, (bg, nc) in zip(outs[:n], dims)], list(outs[n:]))


def _gdn_pre_kernel(x_ref, xp_ref, xn_ref, w_ref, bd_ref, o_ref, *, nt):
    i = pl.program_id(1)
    x = x_ref[0]
    tt = x.shape[0]
    row = lax.broadcasted_iota(jnp.int32, x.shape, 0)
    prev_row = jnp.where(i > 0, xp_ref[0, SUBLANES - 1:SUBLANES, :], 0.0)
    next_row = jnp.where(i < nt - 1, xn_ref[0, 0:1, :], 0.0)
    x_m1 = jnp.where(row == 0, prev_row, pltpu.roll(x, 1, 0))
    x_p1 = jnp.where(row == tt - 1, next_row, pltpu.roll(x, tt - 1, 0))
    y = _silu(x_m1 * w_ref[0:1, :] + x * w_ref[1:2, :] + x_p1 * w_ref[2:3, :])
    q = y[:, :GDN_K_W]
    k = y[:, GDN_K_W:2 * GDN_K_W]
    o_ref[0, :, :GDN_K_W] = q * lax.rsqrt(_seg_mean_sq(q, bd_ref) * GDN_DK + EPS) * (GDN_DK ** -0.5)
    o_ref[0, :, GDN_K_W:2 * GDN_K_W] = k * lax.rsqrt(_seg_mean_sq(k, bd_ref) * GDN_DK + EPS)
    o_ref[0, :, 2 * GDN_K_W:] = y[:, 2 * GDN_K_W:]


def _gdn_pre(gqkv, conv_w, bd_sum, *, tt):
    bg, lg, w = gqkv.shape
    nt = lg // tt
    hb = tt // SUBLANES
    return pl.pallas_call(
        functools.partial(_gdn_pre_kernel, nt=nt),
        grid=(bg, nt),
        in_specs=[pl.BlockSpec((1, tt, w), lambda b, i: (b, i, 0)),
                  pl.BlockSpec((1, SUBLANES, w), lambda b, i: (b, jnp.maximum(i * hb - 1, 0), 0)),
                  pl.BlockSpec((1, SUBLANES, w), lambda b, i: (b, jnp.minimum((i + 1) * hb, lg // SUBLANES - 1), 0)),
                  pl.BlockSpec((3, w), lambda b, i: (0, 0)),
                  pl.BlockSpec((GDN_K_W, GDN_K_W), lambda b, i: (0, 0))],
        out_specs=pl.BlockSpec((1, tt, w), lambda b, i: (b, i, 0)),
        out_shape=jax.ShapeDtypeStruct((bg, lg, w), F32),
        compiler_params=_params("arbitrary", "arbitrary"),
        name="gdn_pre",
    )(gqkv, gqkv, gqkv, conv_w, bd_sum)


def _head_blocks(x, head_masks):
    xb = x.astype(BF16)
    return jnp.concatenate([xb * m for m in head_masks], axis=0)


def _hmm(a, b_blocks):
    return jnp.dot(a.astype(BF16), b_blocks, preferred_element_type=F32)


def _tri_inverse_heads(a_list, same16, same32, eye, head_masks):
    n = a_list[0].shape[0]
    pw = [jnp.where(same16, -a, 0.0) for a in a_list]
    x = [eye + p for p in pw]
    pw = [_hmm(p, _head_blocks(p, head_masks)) for p in pw]
    for _ in range(2):
        both = [_hmm(jnp.concatenate([xi, p], axis=0), _head_blocks(p, head_masks)) for xi, p in zip(x, pw)]
        x = [xi + b[:n] for xi, b in zip(x, both)]
        pw = [b[n:] for b in both]
    x = [xi + _hmm(xi, _head_blocks(p, head_masks)) for xi, p in zip(x, pw)]
    for mask in (jnp.logical_and(same32, jnp.logical_not(same16)), jnp.logical_not(same32)):
        t = [_hmm(jnp.where(mask, a, 0.0), _head_blocks(xi, head_masks)) for a, xi in zip(a_list, x)]
        x = [xi - _hmm(xi, _head_blocks(ti, head_masks)) for xi, ti in zip(x, t)]
    return x


def _gdn_kernel(xf_ref, xb_ref, abf_ref, abb_ref, abtf_ref, abtb_ref, prow_ref, pcol_ref, ex_ref, s0_ref,
                of_ref, ob_ref, sfin_ref, s_scr, *, nt, tt):
    n = pl.program_id(1)
    ck = GDN_CHUNK
    cpt = tt // ck
    hw = GDN_HEADS * ck

    @pl.when(n == 0)
    def _():
        s_scr[...] = s0_ref[0]

    r = lax.broadcasted_iota(jnp.int32, (ck, hw), 0)
    lane = lax.broadcasted_iota(jnp.int32, (ck, hw), 1)
    c = lane % ck
    same16 = (r // 16) == (c // 16)
    same32 = (r // 32) == (c // 32)
    eye = (r == c).astype(F32)
    head_sel = [(lane // ck) == h for h in range(GDN_HEADS)]
    lane1 = lax.broadcasted_iota(jnp.int32, (1, hw), 1)
    head_masks = [((lane1 // ck) == h).astype(BF16) for h in range(GDN_HEADS)]
    row_in = lax.broadcasted_iota(jnp.int32, (tt, LANES), 0) % ck
    lane_in = lax.broadcasted_iota(jnp.int32, (GATE_W, tt), 1) % ck
    is_decay_lane = lax.broadcasted_iota(jnp.int32, (1, LANES), 1) < GATE_W // 2

    steps = [[] for _ in range(cpt)]
    for d, (x_ref, ab_ref, abt_ref, o_ref) in enumerate(((xf_ref, abf_ref, abtf_ref, of_ref),
                                                          (xb_ref, abb_ref, abtb_ref, ob_ref))):
        rev = d == 1
        ab = ab_ref[0]
        g_col = -jnp.exp(prow_ref[0:1, :]) * _softplus(ab + prow_ref[1:2, :])
        abt = abt_ref[0]
        g_row = -jnp.exp(pcol_ref[:, 0:1]) * _softplus(abt + pcol_ref[:, 1:2])
        sh = 1
        while sh < ck:
            if rev:
                g_col = g_col + jnp.where(row_in < ck - sh, pltpu.roll(g_col, tt - sh, 0), 0.0)
                g_row = g_row + jnp.where(lane_in < ck - sh, pltpu.roll(g_row, tt - sh, 1), 0.0)
            else:
                g_col = g_col + jnp.where(row_in >= sh, pltpu.roll(g_col, sh, 0), 0.0)
                g_row = g_row + jnp.where(lane_in >= sh, pltpu.roll(g_row, sh, 1), 0.0)
            sh *= 2
        spread = _mm_exact_rhs(jnp.where(is_decay_lane, g_col, jax.nn.sigmoid(ab)), ex_ref[d])
        incl = (r <= c) if rev else (r >= c)
        strict = (r < c) if rev else (r > c)
        last = 0 if rev else ck - 1
        for i, cc in enumerate(range(cpt - 1, -1, -1) if rev else range(cpt)):
            c0 = cc * ck
            gc = spread[c0:c0 + ck, :hw]
            gr = jnp.concatenate([g_row[d * GDN_HEADS + h:d * GDN_HEADS + h + 1, c0:c0 + ck]
                                  for h in range(GDN_HEADS)], axis=1)
            steps[i].append(dict(
                d=d, o_ref=o_ref, rows=slice(c0, c0 + ck), strict=strict,
                q=x_ref[0, c0:c0 + ck, :GDN_K_W], k=x_ref[0, c0:c0 + ck, GDN_K_W:2 * GDN_K_W],
                v=x_ref[0, c0:c0 + ck, 2 * GDN_K_W:],
                gc=gc, beta=spread[c0:c0 + ck, hw:], g_last=gc[last:last + 1, :],
                decay=jnp.exp(jnp.where(incl, gc - gr, -jnp.inf))))
    items = [it for st in steps for it in st]
    for it in items:
        k_blocks = _head_blocks(it["k"], head_masks)
        kq = lax.dot_general(jnp.concatenate([it["k"], it["q"]], axis=0).astype(BF16), k_blocks, _NT,
                             preferred_element_type=F32)
        it["kk"] = kq[:ck]
        it["attn"] = kq[ck:] * it["decay"]
    t_inv = _tri_inverse_heads([jnp.where(it["strict"], it["kk"] * it["decay"] * it["beta"], 0.0) for it in items],
                               same16, same32, eye, head_masks)
    for it, ti in zip(items, t_inv):
        egc = jnp.exp(it["gc"])
        it["u"] = _hmm(ti, _head_blocks(it["v"] * it["beta"], head_masks))
        w = _hmm(ti, _head_blocks(it["k"] * (it["beta"] * egc), head_masks))
        it["w_qe"] = jnp.concatenate([w, it["q"] * egc], axis=0).astype(BF16)
        it["kd"] = (it["k"] * jnp.exp(it["g_last"] - it["gc"])).astype(BF16)

    state = [s_scr[d] for d in range(N_DIR)]
    for st in steps:
        from_state = [_hmm(it["w_qe"], _head_blocks(state[it["d"]], head_masks)) for it in st]
        v_new = [it["u"] - fs[:ck] for it, fs in zip(st, from_state)]
        for it, fs, vn in zip(st, from_state, v_new):
            it["o_ref"][0, it["rows"], :] = fs[ck:] + _hmm(it["attn"], _head_blocks(vn, head_masks))
        for it, vn in zip(st, v_new):
            cross = lax.dot_general(it["kd"], vn.astype(BF16), _TN, preferred_element_type=F32)
            upd = functools.reduce(lambda a, b: a + b,
                                   [jnp.where(head_sel[h], cross[h * ck:(h + 1) * ck, :], 0.0)
                                    for h in range(GDN_HEADS)])
            state[it["d"]] = state[it["d"]] * jnp.exp(it["g_last"]) + upd
    for d in range(N_DIR):
        s_scr[d] = state[d]

    @pl.when(n == nt - 1)
    def _():
        sfin_ref[0] = s_scr[...]


def _gdn_mix(qkv, gab, gabt, prow, pcol, s0, *, tt):
    bg, lg, w = qkv.shape
    nt = lg // tt
    fwd = lambda b, n: (b, n, 0)
    bwd = lambda b, n: (b, nt - 1 - n, 0)
    s0 = s0.transpose(0, 1, 3, 2, 4).reshape(bg, N_DIR, GDN_DK, GDN_V_W)
    gate_col = jnp.arange(LANES)[None, :, None]
    want = (jnp.arange(N_DIR)[:, None, None] * GDN_HEADS + jnp.arange(GDN_V_W)[None, None, :] // GDN_DV)
    spread = jnp.concatenate([gate_col == want, gate_col == want + GATE_W // 2], axis=2).astype(BF16)
    st_spec = pl.BlockSpec((1, N_DIR, GDN_DK, GDN_V_W), lambda b, n: (b, 0, 0, 0))
    o_f, o_b, s_fin = pl.pallas_call(
        functools.partial(_gdn_kernel, nt=nt, tt=tt),
        grid=(bg, nt),
        in_specs=[pl.BlockSpec((1, tt, w), fwd), pl.BlockSpec((1, tt, w), bwd),
                  pl.BlockSpec((1, tt, LANES), fwd), pl.BlockSpec((1, tt, LANES), bwd),
                  pl.BlockSpec((1, GATE_W, tt), lambda b, n: (b, 0, n)),
                  pl.BlockSpec((1, GATE_W, tt), lambda b, n: (b, 0, nt - 1 - n)),
                  pl.BlockSpec((2, LANES), lambda b, n: (0, 0)),
                  pl.BlockSpec((GATE_W, 2), lambda b, n: (0, 0)),
                  pl.BlockSpec((N_DIR, LANES, 2 * GDN_V_W), lambda b, n: (0, 0, 0)),
                  st_spec],
        out_specs=[pl.BlockSpec((1, tt, GDN_V_W), fwd), pl.BlockSpec((1, tt, GDN_V_W), bwd), st_spec],
        out_shape=[jax.ShapeDtypeStruct((bg, lg, GDN_V_W), F32), jax.ShapeDtypeStruct((bg, lg, GDN_V_W), F32),
                   jax.ShapeDtypeStruct((bg, N_DIR, GDN_DK, GDN_V_W), F32)],
        scratch_shapes=[pltpu.VMEM((N_DIR, GDN_DK, GDN_V_W), F32)],
        compiler_params=_params("arbitrary", "arbitrary"),
        name="gdn_mix",
    )(qkv, qkv, gab, gab, gabt, gabt, prow, pcol, spread, s0)
    s_fin = s_fin.reshape(bg, N_DIR, GDN_DK, GDN_HEADS, GDN_DV).transpose(0, 1, 3, 2, 4)
    return o_f, o_b, s_fin


def _gelu_tanh(x):
    return 0.5 * x * (1.0 + jnp.tanh(math.sqrt(2.0 / math.pi) * (x + 0.044715 * (x * x * x))))


def _out_kernel(x_ref, ga_ref, shm_ref, scm_ref, gm_ref, attn_ref, yt_ref, of_ref, ob_ref, gz_ref,
                g2_ref, gng_ref, bd_ref, wglut_ref, bglu_ref, wo_ref, w1_ref, w2_ref, o_ref):
    x = x_ref[0]
    z = _gelu_tanh(yt_ref[...])
    gate = jax.nn.sigmoid(jnp.dot(wglut_ref[...], z.astype(BF16), preferred_element_type=F32) + bglu_ref[...])
    ssm_t = (z * gate).astype(BF16)
    o = of_ref[0] + ob_ref[0]
    gdn = o * lax.rsqrt(_seg_mean_sq(o, bd_ref) + EPS) * gng_ref[...] * _silu(gz_ref[0])
    mixed = (jnp.dot(attn_ref[0], wo_ref[:ATT_Q_W, :], preferred_element_type=F32)
             + lax.dot_general(ssm_t, wo_ref[ATT_Q_W:ATT_Q_W + SSM_WIDTH, :], _TN, preferred_element_type=F32)
             + jnp.dot(gdn.astype(BF16), wo_ref[ATT_Q_W + SSM_WIDTH:, :], preferred_element_type=F32))
    x1 = x + ga_ref[0] * mixed
    ms = jnp.mean(x1 * x1, axis=-1, keepdims=True)
    h2 = ((x1 * lax.rsqrt(ms + EPS) * g2_ref[...]) * (1.0 + scm_ref[0]) + shm_ref[0]).astype(BF16)
    acc = None
    fc = D_MODEL
    for j in range(D_FF // fc):
        a = jnp.maximum(jnp.dot(h2, w1_ref[:, j * fc:(j + 1) * fc], preferred_element_type=F32), 0.0)
        part = jnp.dot((a * a).astype(BF16), w2_ref[j * fc:(j + 1) * fc, :], preferred_element_type=F32)
        acc = part if acc is None else acc + part
    o_ref[0] = x1 + gm_ref[0] * acc


def _out_projection(x, mod3, mod_row, attn, yt, o_f, o_b, gz, p, consts, *, tm):
    bg, lg, _ = x.shape
    nt = lg // tm
    row = mod_row
    full = lambda shape: pl.BlockSpec(shape, lambda t, b: (0,) * len(shape), pipeline_mode=pl.Buffered(1))
    tok = lambda w: pl.BlockSpec((1, tm, w), lambda t, b: (b, t, 0))
    modc = lambda j: pl.BlockSpec((1, 1, D_MODEL), lambda t, b: (row(b), 0, j))
    mix_w = ATT_Q_W + SSM_WIDTH + GDN_V_W
    return pl.pallas_call(
        _out_kernel,
        grid=(nt, bg),
        in_specs=[tok(D_MODEL), modc(2), modc(3), modc(4), modc(5),
                  tok(ATT_Q_W), pl.BlockSpec((SSM_WIDTH, tm), lambda t, b: (0, b * nt + t)),
                  tok(GDN_V_W), tok(GDN_V_W), tok(GDN_V_W),
                  full((1, D_MODEL)), full((1, GDN_V_W)), full((GDN_V_W, GDN_V_W)),
                  full((SSM_WIDTH, SSM_WIDTH)), full((SSM_WIDTH, 1)),
                  full((mix_w, D_MODEL)), full((D_MODEL, D_FF)), full((D_FF, D_MODEL))],
        out_specs=tok(D_MODEL),
        out_shape=jax.ShapeDtypeStruct((bg, lg, D_MODEL), F32),
        compiler_params=_params("arbitrary", "arbitrary"),
        name="out_projection",
    )(x, mod3, mod3, mod3, mod3, attn, yt, o_f, o_b, gz,
      p["norm2_g"], p["gdn_norm_g"], consts["bd_mean"][:GDN_V_W, :GDN_V_W],
      p["w_glu_t"], p["b_glu"], p["w_out"], p["w_ff1"], p["w_ff2"])


def _constants(max_len):
    n_freq = HEAD_DIM // 4
    rows = jnp.repeat(jnp.arange(max_len // GRID_W, dtype=F32), GRID_W)
    cols = jnp.tile(jnp.arange(GRID_W, dtype=F32), max_len // GRID_W)
    inv_freq = jnp.power(ROPE_BASE, -jnp.arange(n_freq, dtype=F32) / n_freq)
    ang = jnp.concatenate([rows[:, None] * inv_freq, cols[:, None] * inv_freq], axis=-1)
    cos = jnp.repeat(jnp.cos(ang), 2, axis=-1)
    sin = jnp.repeat(jnp.sin(ang), 2, axis=-1) * jnp.tile(jnp.array([-1.0, 1.0], F32), HEAD_DIM // 2)
    seg = jnp.arange(ATT_Q_W) // HEAD_DIM
    same = seg[:, None] == seg[None, :]
    return {
        "cos": jnp.tile(cos, (1, LANES // HEAD_DIM)),
        "sin": jnp.tile(sin, (1, LANES // HEAD_DIM)),
        "bd_mean": jnp.where(same, 1.0 / HEAD_DIM, 0.0).astype(BF16),
    }


def _layer_params(l, w):
    w_in = w["w_in"][l]
    off = [0]
    for size in (ATT_Q_W, ATT_KV_W, ATT_KV_W, SSM_WIDTH, GDN_K_W, GDN_K_W, GDN_V_W, GDN_V_W, GATE_W):
        off.append(off[-1] + size)
    wab = w_in[:, off[8]:off[9]]
    col = lambda a: a.reshape(-1, 1)
    pad_lanes = lambda a: jnp.pad(a, ((0, 0), (0, LANES - a.shape[1])))
    alog = w["gdn_a_log"][l].reshape(1, -1)
    dtb = w["gdn_dt_bias"][l].reshape(1, -1)
    zeros = jnp.zeros_like(alog)
    ls = jnp.broadcast_to(w["ssm_log_step"][l][..., None], (N_DIR, SSM_GROUPS, SSM_STATE))
    pair_order = jnp.arange(N_HEADS).reshape(N_KV_HEADS, GQA).T.reshape(-1)
    head_cols = (pair_order[:, None] * HEAD_DIM + jnp.arange(HEAD_DIM)[None, :]).reshape(-1)
    w_out = w["w_out"][l]
    w_out = jnp.concatenate([w_out[:ATT_Q_W][head_cols], w_out[ATT_Q_W:]], axis=0)
    return {
        "norm1_g": w["norm1_g"][l].reshape(1, -1), "norm2_g": w["norm2_g"][l].reshape(1, -1),
        "wq": w_in[:, off[0]:off[1]][:, head_cols].astype(BF16), "wkv": w_in[:, off[1]:off[3]].astype(BF16),
        "wut": w_in[:, off[3]:off[4]].T.astype(BF16), "wg": w_in[:, off[4]:off[7]].astype(BF16),
        "wz": w_in[:, off[7]:off[8]].astype(BF16),
        "wab": pad_lanes(wab).astype(BF16), "wabt": wab.T.astype(BF16),
        "q_norm_g": jnp.tile(w["q_norm_g"][l], N_HEADS).reshape(1, -1),
        "k_norm_g": jnp.tile(w["k_norm_g"][l], N_KV_HEADS).reshape(1, -1),
        "attn_sink": w["attn_sink"][l],
        "ssm_pc": jnp.stack([w["ssm_lam_re"][l], w["ssm_lam_im"][l], ls], axis=-1),
        "ssm_pr": jnp.stack([w["ssm_lam_re"][l], w["ssm_lam_im"][l], ls], axis=-2),
        "ssm_b_re": w["ssm_b_re"][l], "ssm_b_im": w["ssm_b_im"][l],
        "ssm_c_re": w["ssm_c_re"][l], "ssm_c_im": w["ssm_c_im"][l],
        "ssm_c_re_t": jnp.swapaxes(w["ssm_c_re"][l], -1, -2), "ssm_c_im_t": jnp.swapaxes(w["ssm_c_im"][l], -1, -2),
        "ssm_d": w["ssm_d"][l],
        "w_glu_t": w["ssm_w_glu"][l].T.astype(BF16), "b_glu": col(w["ssm_b_glu"][l]),
        "gdn_conv_w": w["gdn_conv_w"][l],
        "gdn_prow": pad_lanes(jnp.concatenate([jnp.concatenate([alog, zeros], 1), jnp.concatenate([dtb, zeros], 1)], 0)),
        "gdn_pcol": jnp.concatenate([jnp.concatenate([alog, zeros], 1), jnp.concatenate([dtb, zeros], 1)], 0).T,
        "gdn_norm_g": jnp.tile(w["gdn_norm_g"][l], GDN_HEADS).reshape(1, -1),
        "w_out": w_out.astype(BF16), "w_ff1": w["w_ff1"][l].astype(BF16), "w_ff2": w["w_ff2"][l].astype(BF16),
    }


def _state_to_lanes(h0):
    bg = h0.shape[0]
    return h0.transpose(3, 0, 2, 1, 4).reshape(SSM_GROUPS, bg, 2 * N_DIR * SSM_STATE)


def _lanes_to_state(h):
    bg = h.shape[1]
    return h.reshape(SSM_GROUPS, bg, 2, N_DIR, SSM_STATE).transpose(1, 3, 2, 0, 4)


def _layer(groups, mod3, p, consts, layer):
    tables = _ssm_prep(p)
    proj = []
    for gr in groups:
        lg = gr["x"].shape[1]
        proj.append(_in_projection(gr["x"], mod3, gr["mod_row"], p, consts, rope=gr["ctx_kv"] is not None,
                                   tm=min(lg, 512)))
    shapes = [gr["x"].shape[:2] for gr in groups]
    yts, h_fins = _ssm_mix([pr[3] for pr in proj], tables, [_state_to_lanes(gr["ssm_h0"]) for gr in groups],
                           p["ssm_d"], shapes)
    xs, aux = [], []
    for gr, pr, yt, h_fin in zip(groups, proj, yts, h_fins):
        bg, lg, _ = gr["x"].shape
        q, k, v, _, gqkv, gz, gab, gabt = pr
        if gr["ctx_kv"] is not None:
            attn = _latent_attention(q, k, v, gr["ctx_kv"][0], gr["ctx_kv"][1], layer, p["attn_sink"])
        else:
            attn = _context_attention(q, k, v, p["attn_sink"])
        qkv = _gdn_pre(gqkv, p["gdn_conv_w"], consts["bd_mean"][:GDN_K_W, :GDN_K_W], tt=min(lg, 512))
        o_f, o_b, s_fin = _gdn_mix(qkv, gab, gabt, p["gdn_prow"], p["gdn_pcol"], gr["gdn_s0"], tt=min(lg, GDN_TILE))
        xs.append(_out_projection(gr["x"], mod3, gr["mod_row"], attn, yt, o_f, o_b, gz, p, consts, tm=min(lg, 512)))
        aux.append((k, v, _lanes_to_state(h_fin), s_fin))
    return xs, aux


def kernel(x_prompt, x_sample, c, cache_k, cache_v, state_ssm, state_gdn, c_ctx, norm1_g, norm2_g, w_mod, b_mod, w_in, q_norm_g, k_norm_g, attn_sink, ssm_lam_re, ssm_lam_im, ssm_log_step, ssm_b_re, ssm_b_im, ssm_c_re, ssm_c_im, ssm_d, ssm_w_glu, ssm_b_glu, gdn_conv_w, gdn_a_log, gdn_dt_bias, gdn_norm_g, w_out, w_ff1, w_ff2):
    w = dict(norm1_g=norm1_g, norm2_g=norm2_g, w_in=w_in, q_norm_g=q_norm_g, k_norm_g=k_norm_g, attn_sink=attn_sink,
             ssm_lam_re=ssm_lam_re, ssm_lam_im=ssm_lam_im, ssm_log_step=ssm_log_step, ssm_b_re=ssm_b_re,
             ssm_b_im=ssm_b_im, ssm_c_re=ssm_c_re, ssm_c_im=ssm_c_im, ssm_d=ssm_d, ssm_w_glu=ssm_w_glu,
             ssm_b_glu=ssm_b_glu, gdn_conv_w=gdn_conv_w, gdn_a_log=gdn_a_log, gdn_dt_bias=gdn_dt_bias,
             gdn_norm_g=gdn_norm_g, w_out=w_out, w_ff1=w_ff1, w_ff2=w_ff2)
    n_ctx, seq, _ = x_prompt.shape
    n_dec, dec_seq, _ = x_sample.shape
    depth = w_in.shape[0]
    past = cache_k.shape[2]

    n_rows = -(-(n_dec + 1) // SUBLANES) * SUBLANES
    cond = jnp.zeros((n_rows, D_MODEL), F32).at[:n_dec].set(c).at[n_dec].set(c_ctx)
    mod = _modulation(cond, w_mod, b_mod)

    consts = _constants(max(seq, dec_seq))
    cache_k4 = cache_k.reshape(n_dec, depth, past, ATT_KV_W)
    cache_v4 = cache_v.reshape(n_dec, depth, past, ATT_KV_W)
    ssm_zero = jnp.zeros((n_ctx, N_DIR, 2, SSM_GROUPS, SSM_STATE), F32)
    gdn_zero = jnp.zeros((n_ctx, N_DIR, GDN_HEADS, GDN_DK, GDN_DV), F32)

    xp, xs = x_prompt, x_sample
    ks, vs, ss, gs = [], [], [], []
    for l in range(depth):
        p = _layer_params(l, w)
        mod3 = mod[l].reshape(n_rows, 1, N_MOD * D_MODEL)
        groups = [dict(x=xp, mod_row=lambda b: n_dec, ctx_kv=None, ssm_h0=ssm_zero, gdn_s0=gdn_zero),
                  dict(x=xs, mod_row=lambda b: b, ctx_kv=(cache_k4, cache_v4), ssm_h0=state_ssm[:, l],
                       gdn_s0=state_gdn[:, l])]
        (xp, xs), ((k_l, v_l, s_l, g_l), _) = _layer(groups, mod3, p, consts, l)
        ks.append(k_l.reshape(n_ctx, seq, N_KV_HEADS, HEAD_DIM))
        vs.append(v_l.reshape(n_ctx, seq, N_KV_HEADS, HEAD_DIM))
        ss.append(s_l)
        gs.append(g_l)
    return (xp, xs, jnp.stack(ks, axis=1), jnp.stack(vs, axis=1), jnp.stack(ss, axis=1), jnp.stack(gs, axis=1))
```

```python
import functools
import math

import jax
import jax.numpy as jnp
from jax import lax
from jax.experimental import pallas as pl
from jax.experimental.pallas import tpu as pltpu

F32 = jnp.float32
BF16 = jnp.bfloat16
EPS = 1e-6
NEG_INF = -1e30

D_MODEL = 1024
DEPTH = 4
GRID_W = 64
N_DIR = 2
N_HEADS = 8
N_KV_HEADS = 2
GQA = N_HEADS // N_KV_HEADS
HEAD_DIM = 64
ATT_BLOCK = 128
ROPE_BASE = 10000.0
SSM_GROUP_CH = 16
SSM_GROUPS = 16
SSM_WIDTH = SSM_GROUPS * SSM_GROUP_CH
SSM_STATE = 64
GDN_HEADS = 4
GDN_DK = 64
GDN_DV = 64
GDN_K_W = GDN_HEADS * GDN_DK
GDN_V_W = GDN_HEADS * GDN_DV
GDN_QKV_W = 2 * GDN_K_W + GDN_V_W
GDN_CHUNK = 64
ATT_Q_W = N_HEADS * HEAD_DIM
ATT_KV_W = N_KV_HEADS * HEAD_DIM
D_FF = 4 * D_MODEL
N_MOD = 6
GATE_W = 2 * N_DIR * GDN_HEADS

LANES = 128
SUBLANES = 8
SSM_T = LANES
GDN_TILE = 512
GDN_GROUP_CHUNKS = 4
VMEM_LIMIT = 56 * 1024 * 1024

_NT = (((1,), (1,)), ((), ()))
_TN = (((0,), (0,)), ((), ()))


def _mm(a, b):
    return jnp.dot(a.astype(BF16), b.astype(BF16), preferred_element_type=F32)


def _mm_nt(a, b):
    return lax.dot_general(a.astype(BF16), b.astype(BF16), _NT, preferred_element_type=F32)


def _mm_tn(a, b):
    return lax.dot_general(a.astype(BF16), b.astype(BF16), _TN, preferred_element_type=F32)


def _split3(a):
    hi = a.astype(BF16)
    r1 = a - hi.astype(F32)
    mid = r1.astype(BF16)
    lo = (r1 - mid.astype(F32)).astype(BF16)
    return hi, mid, lo


def _mm_exact_rhs(a, b_bf16):
    hi, mid, lo = _split3(a)
    return (jnp.dot(hi, b_bf16, preferred_element_type=F32)
            + jnp.dot(mid, b_bf16, preferred_element_type=F32)
            + jnp.dot(lo, b_bf16, preferred_element_type=F32))


def _mm_3pass(a, b):
    ah, am, _ = _split3(a)
    bh, bm, _ = _split3(b)
    d = lambda x, y: jnp.dot(x, y, preferred_element_type=F32)
    return d(ah, bh) + (d(ah, bm) + d(am, bh))


def _silu(x):
    return x * jax.nn.sigmoid(x)


def _softplus(x):
    return jnp.maximum(x, 0.0) + jnp.log1p(jnp.exp(-jnp.abs(x)))


def _params(*sem):
    return pltpu.CompilerParams(dimension_semantics=sem, vmem_limit_bytes=VMEM_LIMIT)


def _mod_kernel(cond_ref, w_ref, b_ref, o_ref):
    c = cond_ref[...]
    o_ref[0] = _mm(_silu(c), w_ref[0]) + b_ref[0]


def _modulation(cond, w_mod, b_mod):
    rows = cond.shape[0]
    cn = 1536
    return pl.pallas_call(
        _mod_kernel,
        grid=(DEPTH, N_MOD * D_MODEL // cn),
        in_specs=[pl.BlockSpec((rows, D_MODEL), lambda l, j: (0, 0)),
                  pl.BlockSpec((1, D_MODEL, cn), lambda l, j: (l, 0, j)),
                  pl.BlockSpec((1, 1, cn), lambda l, j: (l, 0, j))],
        out_specs=pl.BlockSpec((1, rows, cn), lambda l, j: (l, 0, j)),
        out_shape=jax.ShapeDtypeStruct((DEPTH, rows, N_MOD * D_MODEL), F32),
        compiler_params=_params("arbitrary", "arbitrary"),
        name="modulation",
    )(cond, w_mod, b_mod.reshape(DEPTH, 1, N_MOD * D_MODEL))


def _seg_mean_sq(x, bd_ref):
    n = x.shape[-1]
    return jnp.dot((x * x).astype(BF16), bd_ref[:n, :n], preferred_element_type=F32)


def _rope(x, cos, sin_signed):
    n = x.shape[-1]
    nxt = pltpu.roll(x, n - 1, 1)
    prv = pltpu.roll(x, 1, 1)
    lane = lax.broadcasted_iota(jnp.int32, x.shape, 1)
    swapped = jnp.where(lane % 2 == 0, nxt, prv)
    return x * cos + swapped * sin_signed


def _in_kernel(x_ref, xp_ref, xn_ref, sh_ref, sc_ref, g1_ref, wq_ref, wkv_ref, wut_ref, wg_ref, wz_ref, wab_ref,
               wabt_ref, qg_ref, kg_ref, bd_ref, cos_ref, sin_ref, cw_ref,
               q_out, k_out, v_out, ut_out, g_out, z_out, ab_out, abt_out, *, rope, nt):
    i = pl.program_id(0)
    tm = x_ref.shape[1]
    x = jnp.concatenate([x_ref[0], xp_ref[0], xn_ref[0]], axis=0)
    ms = jnp.mean(x * x, axis=-1, keepdims=True)
    h = (x * lax.rsqrt(ms + EPS) * g1_ref[...]) * (1.0 + sc_ref[0]) + sh_ref[0]
    hb_all = h.astype(BF16)
    hb = hb_all[:tm]

    zg = jnp.dot(hb_all, wg_ref[...], preferred_element_type=F32)
    z = zg[:tm]
    prev_row = jnp.where(i > 0, zg[tm + SUBLANES - 1:tm + SUBLANES], 0.0)
    next_row = jnp.where(i < nt - 1, zg[tm + SUBLANES:tm + SUBLANES + 1], 0.0)
    row = lax.broadcasted_iota(jnp.int32, z.shape, 0)
    z_m1 = jnp.where(row == 0, prev_row, pltpu.roll(z, 1, 0))
    z_p1 = jnp.where(row == tm - 1, next_row, pltpu.roll(z, tm - 1, 0))
    y = _silu(z_m1 * cw_ref[0:1, :] + z * cw_ref[1:2, :] + z_p1 * cw_ref[2:3, :])
    gq = y[:, :GDN_K_W]
    gk = y[:, GDN_K_W:2 * GDN_K_W]
    g_out[0, :, :GDN_K_W] = gq * lax.rsqrt(_seg_mean_sq(gq, bd_ref) * GDN_DK + EPS) * (GDN_DK ** -0.5)
    g_out[0, :, GDN_K_W:2 * GDN_K_W] = gk * lax.rsqrt(_seg_mean_sq(gk, bd_ref) * GDN_DK + EPS)
    g_out[0, :, 2 * GDN_K_W:] = y[:, 2 * GDN_K_W:]

    q = jnp.dot(hb, wq_ref[...], preferred_element_type=F32)
    q = q * lax.rsqrt(_seg_mean_sq(q, bd_ref) + EPS) * qg_ref[...]
    kv = jnp.dot(hb, wkv_ref[...], preferred_element_type=F32)
    k = kv[:, :ATT_KV_W]
    k = k * lax.rsqrt(_seg_mean_sq(k, bd_ref) + EPS) * kg_ref[...]
    if rope:
        cos = cos_ref[...]
        sin = sin_ref[...]
        q = _rope(q, jnp.concatenate([cos] * (ATT_Q_W // LANES), axis=1),
                  jnp.concatenate([sin] * (ATT_Q_W // LANES), axis=1))
        k = _rope(k, cos, sin)
    q_out[0] = (q * (HEAD_DIM ** -0.5)).astype(BF16)
    k_out[0] = k
    v_out[0] = kv[:, ATT_KV_W:]
    ut_out[...] = lax.dot_general(wut_ref[...], hb, _NT, preferred_element_type=F32)
    z_out[0] = jnp.dot(hb, wz_ref[...], preferred_element_type=F32)
    ab_out[0] = jnp.dot(hb, wab_ref[...], preferred_element_type=F32)
    abt_out[0] = lax.dot_general(wabt_ref[...], hb, _NT, preferred_element_type=F32)


def _in_projection(x, mod3, mod_row, p, consts, *, rope, tm):
    bg, lg, _ = x.shape
    nt = lg // tm
    row = mod_row
    full = lambda shape: pl.BlockSpec(shape, lambda t, b: (0,) * len(shape))
    tok = lambda w: pl.BlockSpec((1, tm, w), lambda t, b: (b, t, 0))
    rb = tm // SUBLANES
    in_specs = [
        tok(D_MODEL),
        pl.BlockSpec((1, SUBLANES, D_MODEL), lambda t, b: (b, jnp.maximum(t * rb - 1, 0), 0)),
        pl.BlockSpec((1, SUBLANES, D_MODEL), lambda t, b: (b, jnp.minimum((t + 1) * rb, lg // SUBLANES - 1), 0)),
        pl.BlockSpec((1, 1, D_MODEL), lambda t, b: (row(b), 0, 0)),
        pl.BlockSpec((1, 1, D_MODEL), lambda t, b: (row(b), 0, 1)),
        full((1, D_MODEL)),
        full((D_MODEL, ATT_Q_W)), full((D_MODEL, 2 * ATT_KV_W)), full((SSM_WIDTH, D_MODEL)),
        full((D_MODEL, GDN_QKV_W)), full((D_MODEL, GDN_V_W)), full((D_MODEL, LANES)), full((GATE_W, D_MODEL)),
        full((1, ATT_Q_W)), full((1, ATT_KV_W)), full((ATT_Q_W, ATT_Q_W)),
        pl.BlockSpec((tm, LANES), lambda t, b: (t, 0)),
        pl.BlockSpec((tm, LANES), lambda t, b: (t, 0)),
        full((3, GDN_QKV_W)),
    ]
    out_specs = [
        tok(ATT_Q_W), tok(ATT_KV_W), tok(ATT_KV_W),
        pl.BlockSpec((SSM_WIDTH, tm), lambda t, b: (0, b * nt + t)),
        tok(GDN_QKV_W), tok(GDN_V_W), tok(LANES),
        pl.BlockSpec((1, GATE_W, tm), lambda t, b: (b, 0, t)),
    ]
    out_shape = [
        jax.ShapeDtypeStruct((bg, lg, ATT_Q_W), BF16),
        jax.ShapeDtypeStruct((bg, lg, ATT_KV_W), F32),
        jax.ShapeDtypeStruct((bg, lg, ATT_KV_W), F32),
        jax.ShapeDtypeStruct((SSM_WIDTH, bg * lg), F32),
        jax.ShapeDtypeStruct((bg, lg, GDN_QKV_W), F32),
        jax.ShapeDtypeStruct((bg, lg, GDN_V_W), F32),
        jax.ShapeDtypeStruct((bg, lg, LANES), F32),
        jax.ShapeDtypeStruct((bg, GATE_W, lg), F32),
    ]
    return pl.pallas_call(
        functools.partial(_in_kernel, rope=rope, nt=nt),
        grid=(nt, bg), in_specs=in_specs, out_specs=out_specs, out_shape=out_shape,
        compiler_params=_params("arbitrary", "arbitrary"),
        name="in_projection",
    )(x, x, x, mod3, mod3, p["norm1_g"], p["wq"], p["wkv"], p["wut"], p["wg"], p["wz"], p["wab"], p["wabt"],
      p["q_norm_g"], p["k_norm_g"], consts["bd_mean"], consts["cos"][:lg], consts["sin"][:lg], p["gdn_conv_w"])


def _attend(q_ref, k_all, v_all, masks, sink_ref, o_ref):
    nq = q_ref.shape[1]
    lane = lax.broadcasted_iota(jnp.int32, (1, LANES), 1)
    low = lane < HEAD_DIM
    keep_low = low.astype(BF16)
    keep_high = 1 - keep_low
    tiles = [q_ref[0, :, i * LANES:(i + 1) * LANES] for i in range(GQA)]
    q_rows = jnp.concatenate([t * keep_low for t in tiles] + [t * keep_high for t in tiles], axis=0)
    s = lax.dot_general(q_rows, k_all, _NT, preferred_element_type=F32)
    n_keys = s.shape[1]
    probs, inv_den = [], []
    for h in range(N_HEADS):
        sh = s[h * nq:(h + 1) * nq, :]
        if masks:
            cols = []
            for c0 in range(0, n_keys, ATT_BLOCK):
                piece = sh[:, c0:c0 + ATT_BLOCK]
                cols.append(jnp.where(masks[c0], piece, NEG_INF) if c0 in masks else piece)
            sh = jnp.concatenate(cols, axis=1)
        sink = sink_ref[h]
        m = jnp.maximum(jnp.max(sh, axis=-1, keepdims=True), sink)
        p = jnp.exp(sh - m)
        inv_den.append(1.0 / (jnp.exp(sink - m) + jnp.sum(p, axis=-1, keepdims=True)))
        probs.append(p.astype(BF16))
    o = jnp.dot(jnp.concatenate(probs, axis=0), v_all, preferred_element_type=F32)
    for i in range(GQA):
        o_lo = o[i * nq:(i + 1) * nq, :] * inv_den[i]
        o_hi = o[(GQA + i) * nq:(GQA + i + 1) * nq, :] * inv_den[GQA + i]
        o_ref[0, :, i * LANES:(i + 1) * LANES] = jnp.where(low, o_lo, o_hi).astype(o_ref.dtype)


def _latent_attn_kernel(q_ref, kp_ref, kc_ref, kn_ref, vp_ref, vc_ref, vn_ref, ck_ref, cv_ref, sink_ref, o_ref, *, nb):
    i = pl.program_id(1)
    r = lax.broadcasted_iota(jnp.int32, (ATT_BLOCK, ATT_BLOCK), 0)
    c = lax.broadcasted_iota(jnp.int32, (ATT_BLOCK, ATT_BLOCK), 1)
    masks = {0: jnp.logical_and(c >= r, i > 0), 2 * ATT_BLOCK: jnp.logical_and(c <= r, i < nb - 1)}
    k_all = jnp.concatenate([kp_ref[0], kc_ref[0], kn_ref[0], ck_ref[0, 0]], axis=0).astype(BF16)
    v_all = jnp.concatenate([vp_ref[0], vc_ref[0], vn_ref[0], cv_ref[0, 0]], axis=0).astype(BF16)
    _attend(q_ref, k_all, v_all, masks, sink_ref, o_ref)


def _latent_attention(q, k, v, cache_k4, cache_v4, layer, sink):
    bg, lg, _ = q.shape
    nb = lg // ATT_BLOCK
    past = cache_k4.shape[2]
    blk = lambda w, f: pl.BlockSpec((1, ATT_BLOCK, w), f)
    prev = lambda b, i: (b, jnp.maximum(i - 1, 0), 0)
    cur = lambda b, i: (b, i, 0)
    nxt = lambda b, i: (b, jnp.minimum(i + 1, nb - 1), 0)
    ctx = pl.BlockSpec((1, 1, past, ATT_KV_W), lambda b, i: (b, layer, 0, 0))
    return pl.pallas_call(
        functools.partial(_latent_attn_kernel, nb=nb),
        grid=(bg, nb),
        in_specs=[blk(ATT_Q_W, cur), blk(ATT_KV_W, prev), blk(ATT_KV_W, cur), blk(ATT_KV_W, nxt),
                  blk(ATT_KV_W, prev), blk(ATT_KV_W, cur), blk(ATT_KV_W, nxt), ctx, ctx,
                  pl.BlockSpec(memory_space=pltpu.SMEM)],
        out_specs=blk(ATT_Q_W, cur),
        out_shape=jax.ShapeDtypeStruct((bg, lg, ATT_Q_W), BF16),
        compiler_params=_params("arbitrary", "arbitrary"),
        name="latent_attention",
    )(q, k, k, k, v, v, v, cache_k4, cache_v4, sink)


def _context_attn_kernel(q_ref, k_ref, v_ref, sink_ref, o_ref):
    _attend(q_ref, k_ref[0].astype(BF16), v_ref[0].astype(BF16), {}, sink_ref, o_ref)


def _context_attention(q, k, v, sink):
    bg, lg, _ = q.shape
    nb = lg // ATT_BLOCK
    return pl.pallas_call(
        _context_attn_kernel,
        grid=(bg, nb),
        in_specs=[pl.BlockSpec((1, ATT_BLOCK, ATT_Q_W), lambda b, i: (b, i, 0)),
                  pl.BlockSpec((1, lg, ATT_KV_W), lambda b, i: (b, 0, 0)),
                  pl.BlockSpec((1, lg, ATT_KV_W), lambda b, i: (b, 0, 0)),
                  pl.BlockSpec(memory_space=pltpu.SMEM)],
        out_specs=pl.BlockSpec((1, ATT_BLOCK, ATT_Q_W), lambda b, i: (b, i, 0)),
        out_shape=jax.ShapeDtypeStruct((bg, lg, ATT_Q_W), BF16),
        compiler_params=_params("arbitrary", "arbitrary"),
        name="context_attention",
    )(q, k, v, sink)


def _powers(er, th, e):
    mag = jnp.exp(er * e)
    return mag * jnp.cos(th * e), mag * jnp.sin(th * e)


def _ssm_prep_kernel(pc_ref, pr_ref, br_ref, bi_ref, cr_ref, ci_ref, crt_ref, cit_ref,
                     kv_out, rs_out, f_out, at_out):
    t = SSM_T
    p = SSM_STATE
    lane2 = lax.broadcasted_iota(jnp.int32, (1, 2 * t), 1)
    lane1 = lax.broadcasted_iota(jnp.int32, (1, t), 1)
    e_kv = [jnp.maximum(lane2 - t, 0).astype(F32), jnp.maximum(t - lane2, 0).astype(F32)]
    m_kv = [lane2 >= t, jnp.logical_and(lane2 >= 1, lane2 <= t)]
    e_rs = [(t - 1 - lane1).astype(F32), lane1.astype(F32)]
    e_f = [(lane1 + 1).astype(F32), (t - lane1).astype(F32)]

    bbr, bbi, pk, prs, pf = [], [], [], [], []
    for d in range(N_DIR):
        lr = pc_ref[d, 0, :, 0:1]
        li = pc_ref[d, 0, :, 1:2]
        step = jnp.exp(pc_ref[d, 0, :, 2:3])
        er = lr * step
        th = li * step
        mag = jnp.exp(er)
        ar = mag * jnp.cos(th)
        ai = mag * jnp.sin(th)
        den = lr * lr + li * li
        nr = ar - 1.0
        fr = (nr * lr + ai * li) / den
        fi = (ai * lr - nr * li) / den
        b_re = br_ref[d, 0]
        b_im = bi_ref[d, 0]
        bbr.append(fr * b_re - fi * b_im)
        bbi.append(fr * b_im + fi * b_re)
        kr, ki = _powers(er, th, e_kv[d])
        pk.append((jnp.where(m_kv[d], kr, 0.0), jnp.where(m_kv[d], ki, 0.0)))
        prs.append(_powers(er, th, e_rs[d]))
        pf.append(_powers(er, th, e_f[d]))
        lr_r = pr_ref[d, 0, 0:1, :]
        li_r = pr_ref[d, 0, 1:2, :]
        step_r = jnp.exp(pr_ref[d, 0, 2:3, :])
        mag_t = jnp.exp(lr_r * step_r * float(t))
        at_out[0, :, d * p:(d + 1) * p] = mag_t * jnp.cos(li_r * step_r * float(t))
        at_out[0, :, (N_DIR + d) * p:(N_DIR + d + 1) * p] = mag_t * jnp.sin(li_r * step_r * float(t))

    cmat = jnp.concatenate([cr_ref[0, 0], -ci_ref[0, 0], cr_ref[1, 0], -ci_ref[1, 0]], axis=1)
    for ci in range(SSM_GROUP_CH):
        rows = []
        for d in range(N_DIR):
            cbr = bbr[d][:, ci:ci + 1]
            cbi = bbi[d][:, ci:ci + 1]
            kr, ki = pk[d]
            rows += [kr * cbr - ki * cbi, kr * cbi + ki * cbr]
        kv_out[0, ci * SSM_GROUP_CH:(ci + 1) * SSM_GROUP_CH, :] = _mm_3pass(cmat, jnp.concatenate(rows, axis=0))
        for d in range(N_DIR):
            cbr = bbr[d][:, ci:ci + 1]
            cbi = bbi[d][:, ci:ci + 1]
            sr, si = prs[d]
            rs_out[0, d * p:(d + 1) * p, ci * t:(ci + 1) * t] = (sr * cbr - si * cbi).astype(BF16)
            rs_out[0, (N_DIR + d) * p:(N_DIR + d + 1) * p, ci * t:(ci + 1) * t] = (sr * cbi + si * cbr).astype(BF16)
    for co in range(SSM_GROUP_CH):
        for d in range(N_DIR):
            ccr = crt_ref[d, 0, :, co:co + 1]
            cci = cit_ref[d, 0, :, co:co + 1]
            fr_, fi_ = pf[d]
            f_out[0, d * p:(d + 1) * p, co * t:(co + 1) * t] = (ccr * fr_ - cci * fi_).astype(BF16)
            f_out[0, (N_DIR + d) * p:(N_DIR + d + 1) * p, co * t:(co + 1) * t] = (-(ccr * fi_ + cci * fr_)).astype(BF16)


def _ssm_prep(p):
    t = SSM_T
    g = SSM_GROUPS
    ns = N_DIR * 2 * SSM_STATE
    spec4 = lambda a, b: pl.BlockSpec((N_DIR, 1, a, b), lambda i: (0, i, 0, 0))
    return pl.pallas_call(
        _ssm_prep_kernel,
        grid=(g,),
        in_specs=[spec4(SSM_STATE, 3), spec4(3, SSM_STATE),
                  spec4(SSM_STATE, SSM_GROUP_CH), spec4(SSM_STATE, SSM_GROUP_CH),
                  spec4(SSM_GROUP_CH, SSM_STATE), spec4(SSM_GROUP_CH, SSM_STATE),
                  spec4(SSM_STATE, SSM_GROUP_CH), spec4(SSM_STATE, SSM_GROUP_CH)],
        out_specs=[pl.BlockSpec((1, SSM_GROUP_CH * SSM_GROUP_CH, 2 * t), lambda i: (i, 0, 0)),
                   pl.BlockSpec((1, ns, SSM_GROUP_CH * t), lambda i: (i, 0, 0)),
                   pl.BlockSpec((1, ns, SSM_GROUP_CH * t), lambda i: (i, 0, 0)),
                   pl.BlockSpec((1, 1, ns), lambda i: (i, 0, 0))],
        out_shape=[jax.ShapeDtypeStruct((g, SSM_GROUP_CH * SSM_GROUP_CH, 2 * t), F32),
                   jax.ShapeDtypeStruct((g, ns, SSM_GROUP_CH * t), BF16),
                   jax.ShapeDtypeStruct((g, ns, SSM_GROUP_CH * t), BF16),
                   jax.ShapeDtypeStruct((g, 1, ns), F32)],
        compiler_params=_params("arbitrary"),
        name="ssm_prep",
    )(p["ssm_pc"], p["ssm_pr"], p["ssm_b_re"], p["ssm_b_im"], p["ssm_c_re"], p["ssm_c_im"],
      p["ssm_c_re_t"], p["ssm_c_im_t"])


def _ssm_kernel(*refs, dims):
    ng = len(dims)
    u_refs = refs[:ng]
    kv_ref, rs_ref, f_ref, at_ref = refs[ng:ng + 4]
    h0_refs = refs[ng + 4:2 * ng + 4]
    d_ref = refs[2 * ng + 4]
    y_outs = refs[2 * ng + 5:3 * ng + 5]
    hfin_outs = refs[3 * ng + 5:4 * ng + 5]
    m_scr = refs[4 * ng + 5]
    scratch = refs[4 * ng + 6:]
    t = SSM_T
    nch = SSM_GROUP_CH
    g = pl.program_id(0)
    half = N_DIR * SSM_STATE

    def gen(ci, carry):
        r0 = pl.multiple_of(ci * t, t)
        for co in range(nch):
            row = kv_ref[0, pl.ds(ci * nch + co, 1), :]
            rolled = pltpu.roll(jnp.broadcast_to(row, (t, 2 * t)), t, 1, stride=1, stride_axis=0)
            m_scr[pl.ds(r0, t), co * t:(co + 1) * t] = rolled[:, :t].astype(BF16)
        return carry
    lax.fori_loop(0, nch, gen, 0)

    a_re = at_ref[0, :, :half]
    a_im = at_ref[0, :, half:]
    is_fwd = lax.broadcasted_iota(jnp.int32, (1, half), 1) < SSM_STATE
    for gi, (bg, nc) in enumerate(dims):
        u_ref, h0_ref, y_out, hfin_out = u_refs[gi], h0_refs[gi], y_outs[gi], hfin_outs[gi]
        s_scr, hpf_scr, hpb_scr = scratch[3 * gi:3 * gi + 3]
        u_flat = jnp.concatenate([u_ref[0, ci].astype(BF16) for ci in range(nch)], axis=1)

        s = lax.dot_general(u_flat, rs_ref[0], _NT, preferred_element_type=F32)
        s_scr[0] = s[:, :half]
        s_scr[1] = s[:, half:]

        def advance(re, im, rows, s_scr=s_scr):
            return (a_re * re - a_im * im + s_scr[0, rows, :], a_re * im + a_im * re + s_scr[1, rows, :])

        def scan(i, carry, bg=bg, nc=nc, hpf_scr=hpf_scr, hpb_scr=hpb_scr, advance=advance):
            f_re, f_im, b_re, b_im = carry
            rows_f = pl.ds(i, bg, stride=nc)
            rows_b = pl.ds(nc - 1 - i, bg, stride=nc)
            hpf_scr[0, rows_f, :] = f_re
            hpf_scr[1, rows_f, :] = f_im
            hpb_scr[0, rows_b, :] = b_re
            hpb_scr[1, rows_b, :] = b_im
            return advance(f_re, f_im, rows_f) + advance(b_re, b_im, rows_b)
        h0_re = h0_ref[0, :, :half]
        h0_im = h0_ref[0, :, half:]
        f_re, f_im, b_re, b_im = lax.fori_loop(0, nc, scan, (h0_re, h0_im, h0_re, h0_im))
        hfin_out[0, :, :half] = jnp.where(is_fwd, f_re, b_re)
        hfin_out[0, :, half:] = jnp.where(is_fwd, f_im, b_im)
        hprev = jnp.concatenate([jnp.where(is_fwd, hpf_scr[0], hpb_scr[0]),
                                 jnp.where(is_fwd, hpf_scr[1], hpb_scr[1])], axis=1)

        y = (jnp.dot(hprev.astype(BF16), f_ref[0], preferred_element_type=F32)
             + jnp.dot(u_flat, m_scr[...], preferred_element_type=F32))
        for co in range(nch):
            y_out[0, co] = y[:, co * t:(co + 1) * t] + d_ref[g * nch + co] * u_ref[0, co]


def _ssm_mix(uts, tables, h0s, d_skip, shapes):
    t = SSM_T
    g = SSM_GROUPS
    nch = SSM_GROUP_CH
    ns = N_DIR * 2 * SSM_STATE
    kv, rs, f, at = tables
    dims = tuple((bg, lg // t) for bg, lg in shapes)
    u_specs = [pl.BlockSpec((1, nch, bg * nc, t), lambda i: (i, 0, 0, 0)) for bg, nc in dims]
    h_specs = [pl.BlockSpec((1, bg, ns), lambda i: (i, 0, 0)) for bg, _ in dims]
    scratch = [pltpu.VMEM((nch * t, nch * t), BF16)]
    for bg, nc in dims:
        scratch += [pltpu.VMEM((2, bg * nc, ns // 2), F32)] * 3
    outs = pl.pallas_call(
        functools.partial(_ssm_kernel, dims=dims),
        grid=(g,),
        in_specs=u_specs + [pl.BlockSpec((1, nch * nch, 2 * t), lambda i: (i, 0, 0)),
                            pl.BlockSpec((1, ns, nch * t), lambda i: (i, 0, 0)),
                            pl.BlockSpec((1, ns, nch * t), lambda i: (i, 0, 0)),
                            pl.BlockSpec((1, 1, ns), lambda i: (i, 0, 0))]
                 + h_specs + [pl.BlockSpec(memory_space=pltpu.SMEM)],
        out_specs=u_specs + h_specs,
        out_shape=[jax.ShapeDtypeStruct((g, nch, bg * nc, t), F32) for bg, nc in dims]
                  + [jax.ShapeDtypeStruct((g, bg, ns), F32) for bg, _ in dims],
        scratch_shapes=scratch,
        compiler_params=_params("arbitrary"),
        name="ssm_mix",
    )(*[ut.reshape(g, nch, bg * nc, t) for ut, (bg, nc) in zip(uts, dims)], kv, rs, f, at, *h0s, d_skip)
    n = len(dims)
    return ([y.reshape(g * nch, bg * nc * t) for y, (bg, nc) in zip(outs[:n], dims)], list(outs[n:]))


def _head_blocks(x, head_masks):
    xb = x.astype(BF16)
    return jnp.concatenate([xb * m for m in head_masks], axis=0)


def _hmm(a, b_blocks):
    return jnp.dot(a.astype(BF16), b_blocks, preferred_element_type=F32)


def _tri_inverse_heads(a_list, same16, same32, eye, head_masks):
    n = a_list[0].shape[0]
    pw = [jnp.where(same16, -a, 0.0) for a in a_list]
    x = [eye + p for p in pw]
    pw = [_hmm(p, _head_blocks(p, head_masks)) for p in pw]
    yield
    for _ in range(2):
        both = [_hmm(jnp.concatenate([xi, p], axis=0), _head_blocks(p, head_masks)) for xi, p in zip(x, pw)]
        x = [xi + b[:n] for xi, b in zip(x, both)]
        pw = [b[n:] for b in both]
        yield
    x = [xi + _hmm(xi, _head_blocks(p, head_masks)) for xi, p in zip(x, pw)]
    yield
    for mask in (jnp.logical_and(same32, jnp.logical_not(same16)), jnp.logical_not(same32)):
        t = [_hmm(jnp.where(mask, a, 0.0), _head_blocks(xi, head_masks)) for a, xi in zip(a_list, x)]
        yield
        x = [xi - _hmm(xi, _head_blocks(ti, head_masks)) for xi, ti in zip(x, t)]
        yield
    return x


def _gdn_kernel(xf_ref, xb_ref, abf_ref, abb_ref, abtf_ref, abtb_ref, prow_ref, pcol_ref, ex_ref, s0_ref,
                of_ref, ob_ref, sfin_ref, s_scr, *, nt, tt):
    n = pl.program_id(1)
    ck = GDN_CHUNK
    cpt = tt // ck
    hw = GDN_HEADS * ck

    @pl.when(n == 0)
    def _():
        s_scr[...] = s0_ref[0]

    r = lax.broadcasted_iota(jnp.int32, (ck, hw), 0)
    lane = lax.broadcasted_iota(jnp.int32, (ck, hw), 1)
    c = lane % ck
    same16 = (r // 16) == (c // 16)
    same32 = (r // 32) == (c // 32)
    eye = (r == c).astype(F32)
    head_sel = [(lane // ck) == h for h in range(GDN_HEADS)]
    lane1 = lax.broadcasted_iota(jnp.int32, (1, hw), 1)
    head_masks = [((lane1 // ck) == h).astype(BF16) for h in range(GDN_HEADS)]
    row_in = lax.broadcasted_iota(jnp.int32, (tt, LANES), 0) % ck
    lane_in = lax.broadcasted_iota(jnp.int32, (GATE_W, tt), 1) % ck
    is_decay_lane = lax.broadcasted_iota(jnp.int32, (1, LANES), 1) < GATE_W // 2

    steps = [[] for _ in range(cpt)]
    for d, (x_ref, ab_ref, abt_ref, o_ref) in enumerate(((xf_ref, abf_ref, abtf_ref, of_ref),
                                                          (xb_ref, abb_ref, abtb_ref, ob_ref))):
        rev = d == 1
        ab = ab_ref[0]
        g_col = -jnp.exp(prow_ref[0:1, :]) * _softplus(ab + prow_ref[1:2, :])
        abt = abt_ref[0]
        g_row = -jnp.exp(pcol_ref[:, 0:1]) * _softplus(abt + pcol_ref[:, 1:2])
        sh = 1
        while sh < ck:
            if rev:
                g_col = g_col + jnp.where(row_in < ck - sh, pltpu.roll(g_col, tt - sh, 0), 0.0)
                g_row = g_row + jnp.where(lane_in < ck - sh, pltpu.roll(g_row, tt - sh, 1), 0.0)
            else:
                g_col = g_col + jnp.where(row_in >= sh, pltpu.roll(g_col, sh, 0), 0.0)
                g_row = g_row + jnp.where(lane_in >= sh, pltpu.roll(g_row, sh, 1), 0.0)
            sh *= 2
        spread = _mm_exact_rhs(jnp.where(is_decay_lane, g_col, jax.nn.sigmoid(ab)), ex_ref[d])
        incl = (r <= c) if rev else (r >= c)
        strict = (r < c) if rev else (r > c)
        last = 0 if rev else ck - 1
        for i, cc in enumerate(range(cpt - 1, -1, -1) if rev else range(cpt)):
            c0 = cc * ck
            gc = spread[c0:c0 + ck, :hw]
            gr = jnp.concatenate([g_row[d * GDN_HEADS + h:d * GDN_HEADS + h + 1, c0:c0 + ck]
                                  for h in range(GDN_HEADS)], axis=1)
            steps[i].append(dict(
                d=d, o_ref=o_ref, rows=slice(c0, c0 + ck), strict=strict,
                q=x_ref[0, c0:c0 + ck, :GDN_K_W], k=x_ref[0, c0:c0 + ck, GDN_K_W:2 * GDN_K_W],
                v=x_ref[0, c0:c0 + ck, 2 * GDN_K_W:],
                gc=gc, beta=spread[c0:c0 + ck, hw:], g_last=gc[last:last + 1, :],
                decay=jnp.exp(jnp.where(incl, gc - gr, -jnp.inf))))
    def chunk_local(items):
        for it in items:
            k_blocks = _head_blocks(it["k"], head_masks)
            kq = lax.dot_general(jnp.concatenate([it["k"], it["q"]], axis=0).astype(BF16), k_blocks, _NT,
                                 preferred_element_type=F32)
            it["kk"] = kq[:ck]
            it["attn"] = kq[ck:] * it["decay"]
        yield
        t_inv = yield from _tri_inverse_heads(
            [jnp.where(it["strict"], it["kk"] * it["decay"] * it["beta"], 0.0) for it in items],
            same16, same32, eye, head_masks)
        for it, ti in zip(items, t_inv):
            egc = jnp.exp(it["gc"])
            it["u"] = _hmm(ti, _head_blocks(it["v"] * it["beta"], head_masks))
            w = _hmm(ti, _head_blocks(it["k"] * (it["beta"] * egc), head_masks))
            it["w_qe"] = jnp.concatenate([w, it["q"] * egc], axis=0).astype(BF16)
            it["kd"] = (it["k"] * jnp.exp(it["g_last"] - it["gc"])).astype(BF16)
        yield

    state = [s_scr[d] for d in range(N_DIR)]

    def recurrence(some_steps):
        for st in some_steps:
            from_state = [_hmm(it["w_qe"], _head_blocks(state[it["d"]], head_masks)) for it in st]
            v_new = [it["u"] - fs[:ck] for it, fs in zip(st, from_state)]
            yield
            for it, fs, vn in zip(st, from_state, v_new):
                it["o_ref"][0, it["rows"], :] = fs[ck:] + _hmm(it["attn"], _head_blocks(vn, head_masks))
            for it, vn in zip(st, v_new):
                cross = lax.dot_general(it["kd"], vn.astype(BF16), _TN, preferred_element_type=F32)
                upd = functools.reduce(lambda a, b: a + b,
                                       [jnp.where(head_sel[h], cross[h * ck:(h + 1) * ck, :], 0.0)
                                        for h in range(GDN_HEADS)])
                state[it["d"]] = state[it["d"]] * jnp.exp(it["g_last"]) + upd
            yield

    per = GDN_GROUP_CHUNKS
    groups = [steps[i:i + per] for i in range(0, cpt, per)]
    for gi in range(len(groups) + 1):
        running = []
        if gi < len(groups):
            running.append(chunk_local([it for st in groups[gi] for it in st]))
        if gi >= 1:
            running.append(recurrence(groups[gi - 1]))
        while running:
            for gen in list(running):
                if next(gen, "done") == "done":
                    running.remove(gen)
    for d in range(N_DIR):
        s_scr[d] = state[d]

    @pl.when(n == nt - 1)
    def _():
        sfin_ref[0] = s_scr[...]


def _gdn_mix(qkv, gab, gabt, prow, pcol, s0, *, tt):
    bg, lg, w = qkv.shape
    nt = lg // tt
    fwd = lambda b, n: (b, n, 0)
    bwd = lambda b, n: (b, nt - 1 - n, 0)
    gate_col = jnp.arange(LANES)[None, :, None]
    want = (jnp.arange(N_DIR)[:, None, None] * GDN_HEADS + jnp.arange(GDN_V_W)[None, None, :] // GDN_DV)
    spread = jnp.concatenate([gate_col == want, gate_col == want + GATE_W // 2], axis=2).astype(BF16)
    st_spec = pl.BlockSpec((1, N_DIR, GDN_DK, GDN_V_W), lambda b, n: (b, 0, 0, 0))
    o_f, o_b, s_fin = pl.pallas_call(
        functools.partial(_gdn_kernel, nt=nt, tt=tt),
        grid=(bg, nt),
        in_specs=[pl.BlockSpec((1, tt, w), fwd), pl.BlockSpec((1, tt, w), bwd),
                  pl.BlockSpec((1, tt, LANES), fwd), pl.BlockSpec((1, tt, LANES), bwd),
                  pl.BlockSpec((1, GATE_W, tt), lambda b, n: (b, 0, n)),
                  pl.BlockSpec((1, GATE_W, tt), lambda b, n: (b, 0, nt - 1 - n)),
                  pl.BlockSpec((2, LANES), lambda b, n: (0, 0)),
                  pl.BlockSpec((GATE_W, 2), lambda b, n: (0, 0)),
                  pl.BlockSpec((N_DIR, LANES, 2 * GDN_V_W), lambda b, n: (0, 0, 0)),
                  st_spec],
        out_specs=[pl.BlockSpec((1, tt, GDN_V_W), fwd), pl.BlockSpec((1, tt, GDN_V_W), bwd), st_spec],
        out_shape=[jax.ShapeDtypeStruct((bg, lg, GDN_V_W), F32), jax.ShapeDtypeStruct((bg, lg, GDN_V_W), F32),
                   jax.ShapeDtypeStruct((bg, N_DIR, GDN_DK, GDN_V_W), F32)],
        scratch_shapes=[pltpu.VMEM((N_DIR, GDN_DK, GDN_V_W), F32)],
        compiler_params=_params("arbitrary", "arbitrary"),
        name="gdn_mix",
    )(qkv, qkv, gab, gab, gabt, gabt, prow, pcol, spread, s0)
    return o_f, o_b, s_fin


def _gelu_tanh(x):
    return 0.5 * x * (1.0 + jnp.tanh(math.sqrt(2.0 / math.pi) * (x + 0.044715 * (x * x * x))))


def _out_kernel(x_ref, ga_ref, shm_ref, scm_ref, gm_ref, attn_ref, yt_ref, of_ref, ob_ref, gz_ref,
                g2_ref, gng_ref, bd_ref, wglut_ref, bglu_ref, wo_ref, w1_ref, w2_ref, o_ref):
    x = x_ref[0]
    z = _gelu_tanh(yt_ref[...])
    gate = jax.nn.sigmoid(jnp.dot(wglut_ref[...], z.astype(BF16), preferred_element_type=F32) + bglu_ref[...])
    ssm_t = (z * gate).astype(BF16)
    o = of_ref[0] + ob_ref[0]
    gdn = o * lax.rsqrt(_seg_mean_sq(o, bd_ref) + EPS) * gng_ref[...] * _silu(gz_ref[0])
    mixed = (jnp.dot(attn_ref[0], wo_ref[:ATT_Q_W, :], preferred_element_type=F32)
             + lax.dot_general(ssm_t, wo_ref[ATT_Q_W:ATT_Q_W + SSM_WIDTH, :], _TN, preferred_element_type=F32)
             + jnp.dot(gdn.astype(BF16), wo_ref[ATT_Q_W + SSM_WIDTH:, :], preferred_element_type=F32))
    x1 = x + ga_ref[0] * mixed
    ms = jnp.mean(x1 * x1, axis=-1, keepdims=True)
    h2 = ((x1 * lax.rsqrt(ms + EPS) * g2_ref[...]) * (1.0 + scm_ref[0]) + shm_ref[0]).astype(BF16)
    acc = None
    fc = D_MODEL
    for j in range(D_FF // fc):
        a = jnp.maximum(jnp.dot(h2, w1_ref[:, j * fc:(j + 1) * fc], preferred_element_type=F32), 0.0)
        part = jnp.dot((a * a).astype(BF16), w2_ref[j * fc:(j + 1) * fc, :], preferred_element_type=F32)
        acc = part if acc is None else acc + part
    o_ref[0] = x1 + gm_ref[0] * acc


def _out_projection(x, mod3, mod_row, attn, yt, o_f, o_b, gz, p, consts, *, tm):
    bg, lg, _ = x.shape
    nt = lg // tm
    row = mod_row
    full = lambda shape: pl.BlockSpec(shape, lambda t, b: (0,) * len(shape), pipeline_mode=pl.Buffered(1))
    tok = lambda w: pl.BlockSpec((1, tm, w), lambda t, b: (b, t, 0))
    modc = lambda j: pl.BlockSpec((1, 1, D_MODEL), lambda t, b: (row(b), 0, j))
    mix_w = ATT_Q_W + SSM_WIDTH + GDN_V_W
    return pl.pallas_call(
        _out_kernel,
        grid=(nt, bg),
        in_specs=[tok(D_MODEL), modc(2), modc(3), modc(4), modc(5),
                  tok(ATT_Q_W), pl.BlockSpec((SSM_WIDTH, tm), lambda t, b: (0, b * nt + t)),
                  tok(GDN_V_W), tok(GDN_V_W), tok(GDN_V_W),
                  full((1, D_MODEL)), full((1, GDN_V_W)), full((GDN_V_W, GDN_V_W)),
                  full((SSM_WIDTH, SSM_WIDTH)), full((SSM_WIDTH, 1)),
                  full((mix_w, D_MODEL)), full((D_MODEL, D_FF)), full((D_FF, D_MODEL))],
        out_specs=tok(D_MODEL),
        out_shape=jax.ShapeDtypeStruct((bg, lg, D_MODEL), F32),
        compiler_params=_params("arbitrary", "arbitrary"),
        name="out_projection",
    )(x, mod3, mod3, mod3, mod3, attn, yt, o_f, o_b, gz,
      p["norm2_g"], p["gdn_norm_g"], consts["bd_mean"][:GDN_V_W, :GDN_V_W],
      p["w_glu_t"], p["b_glu"], p["w_out"], p["w_ff1"], p["w_ff2"])


def _constants(max_len):
    n_freq = HEAD_DIM // 4
    rows = jnp.repeat(jnp.arange(max_len // GRID_W, dtype=F32), GRID_W)
    cols = jnp.tile(jnp.arange(GRID_W, dtype=F32), max_len // GRID_W)
    inv_freq = jnp.power(ROPE_BASE, -jnp.arange(n_freq, dtype=F32) / n_freq)
    ang = jnp.concatenate([rows[:, None] * inv_freq, cols[:, None] * inv_freq], axis=-1)
    cos = jnp.repeat(jnp.cos(ang), 2, axis=-1)
    sin = jnp.repeat(jnp.sin(ang), 2, axis=-1) * jnp.tile(jnp.array([-1.0, 1.0], F32), HEAD_DIM // 2)
    seg = jnp.arange(ATT_Q_W) // HEAD_DIM
    same = seg[:, None] == seg[None, :]
    return {
        "cos": jnp.tile(cos, (1, LANES // HEAD_DIM)),
        "sin": jnp.tile(sin, (1, LANES // HEAD_DIM)),
        "bd_mean": jnp.where(same, 1.0 / HEAD_DIM, 0.0).astype(BF16),
    }


def _layer_params(w):
    w_in = w["w_in"]
    off = [0]
    for size in (ATT_Q_W, ATT_KV_W, ATT_KV_W, SSM_WIDTH, GDN_K_W, GDN_K_W, GDN_V_W, GDN_V_W, GATE_W):
        off.append(off[-1] + size)
    wab = w_in[:, off[8]:off[9]]
    col = lambda a: a.reshape(-1, 1)
    pad_lanes = lambda a: jnp.pad(a, ((0, 0), (0, LANES - a.shape[1])))
    alog = w["gdn_a_log"].reshape(1, -1)
    dtb = w["gdn_dt_bias"].reshape(1, -1)
    zeros = jnp.zeros_like(alog)
    ls = jnp.broadcast_to(w["ssm_log_step"][..., None], (N_DIR, SSM_GROUPS, SSM_STATE))
    pair_order = jnp.arange(N_HEADS).reshape(N_KV_HEADS, GQA).T.reshape(-1)
    head_cols = (pair_order[:, None] * HEAD_DIM + jnp.arange(HEAD_DIM)[None, :]).reshape(-1)
    w_out = w["w_out"]
    w_out = jnp.concatenate([w_out[:ATT_Q_W][head_cols], w_out[ATT_Q_W:]], axis=0)
    return {
        "norm1_g": w["norm1_g"].reshape(1, -1), "norm2_g": w["norm2_g"].reshape(1, -1),
        "wq": w_in[:, off[0]:off[1]][:, head_cols].astype(BF16), "wkv": w_in[:, off[1]:off[3]].astype(BF16),
        "wut": w_in[:, off[3]:off[4]].T.astype(BF16), "wg": w_in[:, off[4]:off[7]].astype(BF16),
        "wz": w_in[:, off[7]:off[8]].astype(BF16),
        "wab": pad_lanes(wab).astype(BF16), "wabt": wab.T.astype(BF16),
        "q_norm_g": jnp.tile(w["q_norm_g"], N_HEADS).reshape(1, -1),
        "k_norm_g": jnp.tile(w["k_norm_g"], N_KV_HEADS).reshape(1, -1),
        "attn_sink": w["attn_sink"],
        "ssm_pc": jnp.stack([w["ssm_lam_re"], w["ssm_lam_im"], ls], axis=-1),
        "ssm_pr": jnp.stack([w["ssm_lam_re"], w["ssm_lam_im"], ls], axis=-2),
        "ssm_b_re": w["ssm_b_re"], "ssm_b_im": w["ssm_b_im"],
        "ssm_c_re": w["ssm_c_re"], "ssm_c_im": w["ssm_c_im"],
        "ssm_c_re_t": jnp.swapaxes(w["ssm_c_re"], -1, -2), "ssm_c_im_t": jnp.swapaxes(w["ssm_c_im"], -1, -2),
        "ssm_d": w["ssm_d"],
        "w_glu_t": w["ssm_w_glu"].T.astype(BF16), "b_glu": col(w["ssm_b_glu"]),
        "gdn_conv_w": w["gdn_conv_w"],
        "gdn_prow": pad_lanes(jnp.concatenate([jnp.concatenate([alog, zeros], 1), jnp.concatenate([dtb, zeros], 1)], 0)),
        "gdn_pcol": jnp.concatenate([jnp.concatenate([alog, zeros], 1), jnp.concatenate([dtb, zeros], 1)], 0).T,
        "gdn_norm_g": jnp.tile(w["gdn_norm_g"], GDN_HEADS).reshape(1, -1),
        "w_out": w_out.astype(BF16), "w_ff1": w["w_ff1"].astype(BF16), "w_ff2": w["w_ff2"].astype(BF16),
    }


def _ssm_states_to_lanes(h0):
    bg, depth = h0.shape[:2]
    return h0.transpose(1, 4, 0, 3, 2, 5).reshape(depth, SSM_GROUPS, bg, 2 * N_DIR * SSM_STATE)


def _gdn_states_to_lanes(s0):
    bg, depth = s0.shape[:2]
    return s0.transpose(1, 0, 2, 4, 3, 5).reshape(depth, bg, N_DIR, GDN_DK, GDN_V_W)


def _ssm_lanes_to_states(h):
    depth, _, bg, _ = h.shape
    return h.reshape(depth, SSM_GROUPS, bg, 2, N_DIR, SSM_STATE).transpose(2, 0, 4, 3, 1, 5)


def _gdn_lanes_to_states(s):
    depth, bg = s.shape[:2]
    return s.reshape(depth, bg, N_DIR, GDN_DK, GDN_HEADS, GDN_DV).transpose(1, 0, 2, 4, 3, 5)


def _layer(groups, mod3, p, consts, layer):
    tables = _ssm_prep(p)
    proj = []
    for gr in groups:
        lg = gr["x"].shape[1]
        proj.append(_in_projection(gr["x"], mod3, gr["mod_row"], p, consts, rope=gr["ctx_kv"] is not None,
                                   tm=min(lg, 512)))
    shapes = [gr["x"].shape[:2] for gr in groups]
    yts, h_fins = _ssm_mix([pr[3] for pr in proj], tables, [gr["ssm_h0"] for gr in groups], p["ssm_d"], shapes)
    xs, aux = [], []
    for gr, pr, yt, h_fin in zip(groups, proj, yts, h_fins):
        bg, lg, _ = gr["x"].shape
        q, k, v, _, qkv, gz, gab, gabt = pr
        if gr["ctx_kv"] is not None:
            attn = _latent_attention(q, k, v, gr["ctx_kv"][0], gr["ctx_kv"][1], layer, p["attn_sink"])
        else:
            attn = _context_attention(q, k, v, p["attn_sink"])
        o_f, o_b, s_fin = _gdn_mix(qkv, gab, gabt, p["gdn_prow"], p["gdn_pcol"], gr["gdn_s0"], tt=min(lg, GDN_TILE))
        xs.append(_out_projection(gr["x"], mod3, gr["mod_row"], attn, yt, o_f, o_b, gz, p, consts, tm=min(lg, 512)))
        aux.append((k, v, h_fin, s_fin))
    return xs, aux


def kernel(x_prompt, x_sample, c, cache_k, cache_v, state_ssm, state_gdn, c_ctx, norm1_g, norm2_g, w_mod, b_mod, w_in, q_norm_g, k_norm_g, attn_sink, ssm_lam_re, ssm_lam_im, ssm_log_step, ssm_b_re, ssm_b_im, ssm_c_re, ssm_c_im, ssm_d, ssm_w_glu, ssm_b_glu, gdn_conv_w, gdn_a_log, gdn_dt_bias, gdn_norm_g, w_out, w_ff1, w_ff2):
    w = dict(norm1_g=norm1_g, norm2_g=norm2_g, w_in=w_in, q_norm_g=q_norm_g, k_norm_g=k_norm_g, attn_sink=attn_sink,
             ssm_lam_re=ssm_lam_re, ssm_lam_im=ssm_lam_im, ssm_log_step=ssm_log_step, ssm_b_re=ssm_b_re,
             ssm_b_im=ssm_b_im, ssm_c_re=ssm_c_re, ssm_c_im=ssm_c_im, ssm_d=ssm_d, ssm_w_glu=ssm_w_glu,
             ssm_b_glu=ssm_b_glu, gdn_conv_w=gdn_conv_w, gdn_a_log=gdn_a_log, gdn_dt_bias=gdn_dt_bias,
             gdn_norm_g=gdn_norm_g, w_out=w_out, w_ff1=w_ff1, w_ff2=w_ff2)
    n_ctx, seq, _ = x_prompt.shape
    n_dec, dec_seq, _ = x_sample.shape
    depth = w_in.shape[0]
    past = cache_k.shape[2]

    n_rows = -(-(n_dec + 1) // SUBLANES) * SUBLANES
    cond = jnp.zeros((n_rows, D_MODEL), F32).at[:n_dec].set(c).at[n_dec].set(c_ctx)
    mod = _modulation(cond, w_mod, b_mod)

    consts = _constants(max(seq, dec_seq))
    cache_k4 = cache_k.reshape(n_dec, depth, past, ATT_KV_W)
    cache_v4 = cache_v.reshape(n_dec, depth, past, ATT_KV_W)
    ssm_zero = jnp.zeros((SSM_GROUPS, n_ctx, 2 * N_DIR * SSM_STATE), F32)
    gdn_zero = jnp.zeros((n_ctx, N_DIR, GDN_DK, GDN_V_W), F32)
    ssm_h0 = _ssm_states_to_lanes(state_ssm)
    gdn_s0 = _gdn_states_to_lanes(state_gdn)

    params = jax.vmap(_layer_params)(w)
    xp, xs = x_prompt, x_sample
    ks, vs, ss, gs = [], [], [], []
    for l in range(depth):
        p = {name: value[l] for name, value in params.items()}
        mod3 = mod[l].reshape(n_rows, 1, N_MOD * D_MODEL)
        groups = [dict(x=xp, mod_row=lambda b: n_dec, ctx_kv=None, ssm_h0=ssm_zero, gdn_s0=gdn_zero),
                  dict(x=xs, mod_row=lambda b: b, ctx_kv=(cache_k4, cache_v4), ssm_h0=ssm_h0[l], gdn_s0=gdn_s0[l])]
        (xp, xs), ((k_l, v_l, s_l, g_l), _) = _layer(groups, mod3, p, consts, l)
        ks.append(k_l.reshape(n_ctx, seq, N_KV_HEADS, HEAD_DIM))
        vs.append(v_l.reshape(n_ctx, seq, N_KV_HEADS, HEAD_DIM))
        ss.append(s_l)
        gs.append(g_l)
    return (xp, xs, jnp.stack(ks, axis=1), jnp.stack(vs, axis=1),
            _ssm_lanes_to_states(jnp.stack(ss)), _gdn_lanes_to_states(jnp.stack(gs)))
```

```python
import functools
import math

import jax
import jax.numpy as jnp
from jax import lax
from jax.experimental import pallas as pl
from jax.experimental.pallas import tpu as pltpu

F32 = jnp.float32
BF16 = jnp.bfloat16
EPS = 1e-6
NEG_INF = -1e30
LOG2_E = math.log2(math.e)

D_MODEL = 1024
DEPTH = 4
GRID_W = 64
N_DIR = 2
N_HEADS = 8
N_KV_HEADS = 2
GQA = N_HEADS // N_KV_HEADS
HEAD_DIM = 64
ATT_BLOCK = 128
ROPE_BASE = 10000.0
SSM_GROUP_CH = 16
SSM_GROUPS = 16
SSM_WIDTH = SSM_GROUPS * SSM_GROUP_CH
SSM_STATE = 64
GDN_HEADS = 4
GDN_DK = 64
GDN_DV = 64
GDN_K_W = GDN_HEADS * GDN_DK
GDN_V_W = GDN_HEADS * GDN_DV
GDN_QKV_W = 2 * GDN_K_W + GDN_V_W
GDN_CHUNK = 64
ATT_Q_W = N_HEADS * HEAD_DIM
ATT_KV_W = N_KV_HEADS * HEAD_DIM
D_FF = 4 * D_MODEL
N_MOD = 6
GATE_W = 2 * N_DIR * GDN_HEADS

LANES = 128
SUBLANES = 8
SSM_T = LANES
ATT_STEP_BLOCKS = 4
GDN_TILE = 1024
GDN_GROUP_CHUNKS = 4
VMEM_LIMIT = 56 * 1024 * 1024

_NT = (((1,), (1,)), ((), ()))
_TN = (((0,), (0,)), ((), ()))


def _mm(a, b):
    return jnp.dot(a.astype(BF16), b.astype(BF16), preferred_element_type=F32)


def _mm_nt(a, b):
    return lax.dot_general(a.astype(BF16), b.astype(BF16), _NT, preferred_element_type=F32)


def _mm_tn(a, b):
    return lax.dot_general(a.astype(BF16), b.astype(BF16), _TN, preferred_element_type=F32)


def _split3(a):
    hi = a.astype(BF16)
    r1 = a - hi.astype(F32)
    mid = r1.astype(BF16)
    lo = (r1 - mid.astype(F32)).astype(BF16)
    return hi, mid, lo


def _mm_exact_rhs(a, b_bf16):
    hi, mid, lo = _split3(a)
    return (jnp.dot(hi, b_bf16, preferred_element_type=F32)
            + jnp.dot(mid, b_bf16, preferred_element_type=F32)
            + jnp.dot(lo, b_bf16, preferred_element_type=F32))


def _mm_3pass(a, b):
    ah, am, _ = _split3(a)
    bh, bm, _ = _split3(b)
    d = lambda x, y: jnp.dot(x, y, preferred_element_type=F32)
    return d(ah, bh) + (d(ah, bm) + d(am, bh))


def _silu(x):
    return x * jax.nn.sigmoid(x)


def _softplus(x):
    return jnp.maximum(x, 0.0) + jnp.log1p(jnp.exp(-jnp.abs(x)))


def _params(*sem):
    return pltpu.CompilerParams(dimension_semantics=sem, vmem_limit_bytes=VMEM_LIMIT)


def _mod_kernel(cond_ref, w_ref, b_ref, o_ref):
    c = cond_ref[...]
    o_ref[0] = _mm(_silu(c), w_ref[0]) + b_ref[0]


def _modulation(cond, w_mod, b_mod):
    rows = cond.shape[0]
    cn = 1536
    return pl.pallas_call(
        _mod_kernel,
        grid=(DEPTH, N_MOD * D_MODEL // cn),
        in_specs=[pl.BlockSpec((rows, D_MODEL), lambda l, j: (0, 0)),
                  pl.BlockSpec((1, D_MODEL, cn), lambda l, j: (l, 0, j)),
                  pl.BlockSpec((1, 1, cn), lambda l, j: (l, 0, j))],
        out_specs=pl.BlockSpec((1, rows, cn), lambda l, j: (l, 0, j)),
        out_shape=jax.ShapeDtypeStruct((DEPTH, rows, N_MOD * D_MODEL), F32),
        compiler_params=_params("arbitrary", "arbitrary"),
        name="modulation",
    )(cond, w_mod, b_mod.reshape(DEPTH, 1, N_MOD * D_MODEL))


def _seg_mean_sq(x, bd_ref):
    n = x.shape[-1]
    return jnp.dot((x * x).astype(BF16), bd_ref[:n, :n], preferred_element_type=F32)


def _rope(x, cos, sin_signed):
    n = x.shape[-1]
    nxt = pltpu.roll(x, n - 1, 1)
    prv = pltpu.roll(x, 1, 1)
    lane = lax.broadcasted_iota(jnp.int32, x.shape, 1)
    swapped = jnp.where(lane % 2 == 0, nxt, prv)
    return x * cos + swapped * sin_signed


def _in_kernel(x_ref, xp_ref, xn_ref, sh_ref, sc_ref, g1_ref, wq_ref, wkv_ref, wut_ref, wg_ref, wz_ref, wab_ref,
               wabt_ref, qg_ref, kg_ref, bd_ref, cos_ref, sin_ref, cw_ref,
               q_out, k_out, v_out, ut_out, g_out, z_out, ab_out, abt_out, *, rope, nt):
    i = pl.program_id(0)
    tm = x_ref.shape[1]
    x = jnp.concatenate([x_ref[0], xp_ref[0], xn_ref[0]], axis=0)
    ms = jnp.mean(x * x, axis=-1, keepdims=True)
    h = (x * lax.rsqrt(ms + EPS) * g1_ref[...]) * (1.0 + sc_ref[0]) + sh_ref[0]
    hb_all = h.astype(BF16)
    hb = hb_all[:tm]

    zg = jnp.dot(hb_all, wg_ref[...], preferred_element_type=F32)
    z = zg[:tm]
    prev_row = jnp.where(i > 0, zg[tm + SUBLANES - 1:tm + SUBLANES], 0.0)
    next_row = jnp.where(i < nt - 1, zg[tm + SUBLANES:tm + SUBLANES + 1], 0.0)
    row = lax.broadcasted_iota(jnp.int32, z.shape, 0)
    z_m1 = jnp.where(row == 0, prev_row, pltpu.roll(z, 1, 0))
    z_p1 = jnp.where(row == tm - 1, next_row, pltpu.roll(z, tm - 1, 0))
    y = _silu(z_m1 * cw_ref[0:1, :] + z * cw_ref[1:2, :] + z_p1 * cw_ref[2:3, :])
    gq = y[:, :GDN_K_W]
    gk = y[:, GDN_K_W:2 * GDN_K_W]
    g_out[0, :, :GDN_K_W] = gq * lax.rsqrt(_seg_mean_sq(gq, bd_ref) * GDN_DK + EPS) * (GDN_DK ** -0.5)
    g_out[0, :, GDN_K_W:2 * GDN_K_W] = gk * lax.rsqrt(_seg_mean_sq(gk, bd_ref) * GDN_DK + EPS)
    g_out[0, :, 2 * GDN_K_W:] = y[:, 2 * GDN_K_W:]

    q = jnp.dot(hb, wq_ref[...], preferred_element_type=F32)
    q = q * lax.rsqrt(_seg_mean_sq(q, bd_ref) + EPS) * qg_ref[...]
    kv = jnp.dot(hb, wkv_ref[...], preferred_element_type=F32)
    k = kv[:, :ATT_KV_W]
    k = k * lax.rsqrt(_seg_mean_sq(k, bd_ref) + EPS) * kg_ref[...]
    if rope:
        cos = cos_ref[...]
        sin = sin_ref[...]
        q = _rope(q, jnp.concatenate([cos] * (ATT_Q_W // LANES), axis=1),
                  jnp.concatenate([sin] * (ATT_Q_W // LANES), axis=1))
        k = _rope(k, cos, sin)
    q_out[0] = (q * (HEAD_DIM ** -0.5 * LOG2_E)).astype(BF16)
    k_out[0] = k
    v_out[0] = kv[:, ATT_KV_W:]
    ut_out[...] = lax.dot_general(wut_ref[...], hb, _NT, preferred_element_type=F32)
    z_out[0] = jnp.dot(hb, wz_ref[...], preferred_element_type=F32)
    ab_out[0] = jnp.dot(hb, wab_ref[...], preferred_element_type=F32)
    abt_out[0] = lax.dot_general(wabt_ref[...], hb, _NT, preferred_element_type=F32)


def _in_projection(x, mod3, mod_row, p, consts, *, rope, tm):
    bg, lg, _ = x.shape
    nt = lg // tm
    row = mod_row
    full = lambda shape: pl.BlockSpec(shape, lambda t, b: (0,) * len(shape))
    tok = lambda w: pl.BlockSpec((1, tm, w), lambda t, b: (b, t, 0))
    rb = tm // SUBLANES
    in_specs = [
        tok(D_MODEL),
        pl.BlockSpec((1, SUBLANES, D_MODEL), lambda t, b: (b, jnp.maximum(t * rb - 1, 0), 0)),
        pl.BlockSpec((1, SUBLANES, D_MODEL), lambda t, b: (b, jnp.minimum((t + 1) * rb, lg // SUBLANES - 1), 0)),
        pl.BlockSpec((1, 1, D_MODEL), lambda t, b: (row(b), 0, 0)),
        pl.BlockSpec((1, 1, D_MODEL), lambda t, b: (row(b), 0, 1)),
        full((1, D_MODEL)),
        full((D_MODEL, ATT_Q_W)), full((D_MODEL, 2 * ATT_KV_W)), full((SSM_WIDTH, D_MODEL)),
        full((D_MODEL, GDN_QKV_W)), full((D_MODEL, GDN_V_W)), full((D_MODEL, LANES)), full((GATE_W, D_MODEL)),
        full((1, ATT_Q_W)), full((1, ATT_KV_W)), full((ATT_Q_W, ATT_Q_W)),
        pl.BlockSpec((tm, LANES), lambda t, b: (t, 0)),
        pl.BlockSpec((tm, LANES), lambda t, b: (t, 0)),
        full((3, GDN_QKV_W)),
    ]
    out_specs = [
        tok(ATT_Q_W), tok(ATT_KV_W), tok(ATT_KV_W),
        pl.BlockSpec((SSM_WIDTH, tm), lambda t, b: (0, b * nt + t)),
        tok(GDN_QKV_W), tok(GDN_V_W), tok(LANES),
        pl.BlockSpec((1, GATE_W, tm), lambda t, b: (b, 0, t)),
    ]
    out_shape = [
        jax.ShapeDtypeStruct((bg, lg, ATT_Q_W), BF16),
        jax.ShapeDtypeStruct((bg, lg, ATT_KV_W), F32),
        jax.ShapeDtypeStruct((bg, lg, ATT_KV_W), F32),
        jax.ShapeDtypeStruct((SSM_WIDTH, bg * lg), F32),
        jax.ShapeDtypeStruct((bg, lg, GDN_QKV_W), F32),
        jax.ShapeDtypeStruct((bg, lg, GDN_V_W), F32),
        jax.ShapeDtypeStruct((bg, lg, LANES), F32),
        jax.ShapeDtypeStruct((bg, GATE_W, lg), F32),
    ]
    return pl.pallas_call(
        functools.partial(_in_kernel, rope=rope, nt=nt),
        grid=(nt, bg), in_specs=in_specs, out_specs=out_specs, out_shape=out_shape,
        compiler_params=_params("arbitrary", "arbitrary"),
        name="in_projection",
    )(x, x, x, mod3, mod3, p["norm1_g"], p["wq"], p["wkv"], p["wut"], p["wg"], p["wz"], p["wab"], p["wabt"],
      p["q_norm_g"], p["k_norm_g"], consts["bd_mean"], consts["cos"][:lg], consts["sin"][:lg], p["gdn_conv_w"])


def _attend(q_ref, r0, k_all, v_all, masks, sink_ref, o_ref):
    nq = ATT_BLOCK
    lane = lax.broadcasted_iota(jnp.int32, (1, LANES), 1)
    low = lane < HEAD_DIM
    keep_low = low.astype(BF16)
    keep_high = 1 - keep_low
    tiles = [q_ref[0, r0:r0 + nq, i * LANES:(i + 1) * LANES] for i in range(GQA)]
    q_rows = jnp.concatenate([t * keep_low for t in tiles] + [t * keep_high for t in tiles], axis=0)
    s = lax.dot_general(q_rows, k_all, _NT, preferred_element_type=F32)
    n_keys = s.shape[1]
    probs, sink_terms = [], []
    for h in range(N_HEADS):
        sh = s[h * nq:(h + 1) * nq, :]
        if masks:
            cols = []
            for c0 in range(0, n_keys, ATT_BLOCK):
                piece = sh[:, c0:c0 + ATT_BLOCK]
                cols.append(jnp.where(masks[c0], piece, NEG_INF) if c0 in masks else piece)
            sh = jnp.concatenate(cols, axis=1)
        sink = sink_ref[h] * LOG2_E
        m = jnp.maximum(jnp.max(sh, axis=-1, keepdims=True), sink)
        sink_terms.append(jnp.exp2(sink - m))
        probs.append(jnp.exp2(sh - m).astype(BF16))
    v_ext = jnp.concatenate([v_all, jnp.ones((n_keys, LANES), BF16)], axis=1)
    o = jnp.dot(jnp.concatenate(probs, axis=0), v_ext, preferred_element_type=F32)

    def normalised(h):
        rows = o[h * nq:(h + 1) * nq]
        return rows[:, :LANES] * (1.0 / (rows[:, LANES:] + sink_terms[h]))
    for i in range(GQA):
        o_ref[0, r0:r0 + nq, i * LANES:(i + 1) * LANES] = jnp.where(
            low, normalised(i), normalised(GQA + i)).astype(o_ref.dtype)


def _latent_attn_kernel(*refs, nb):
    nk = ATT_STEP_BLOCKS + 2
    q_ref = refs[0]
    k_refs = refs[1:1 + nk]
    v_refs = refs[1 + nk:1 + 2 * nk]
    ck_ref, cv_ref, sink_ref, o_ref = refs[1 + 2 * nk:]
    i = pl.program_id(1)
    r = lax.broadcasted_iota(jnp.int32, (ATT_BLOCK, ATT_BLOCK), 0)
    c = lax.broadcasted_iota(jnp.int32, (ATT_BLOCK, ATT_BLOCK), 1)
    kb = [ref[0].astype(BF16) for ref in k_refs]
    vb = [ref[0].astype(BF16) for ref in v_refs]
    ck = ck_ref[0, 0].astype(BF16)
    cv = cv_ref[0, 0].astype(BF16)
    for j in range(ATT_STEP_BLOCKS):
        blk = i * ATT_STEP_BLOCKS + j
        masks = {0: jnp.logical_and(c >= r, blk > 0), 2 * ATT_BLOCK: jnp.logical_and(c <= r, blk < nb - 1)}
        k_all = jnp.concatenate(kb[j:j + 3] + [ck], axis=0)
        v_all = jnp.concatenate(vb[j:j + 3] + [cv], axis=0)
        _attend(q_ref, j * ATT_BLOCK, k_all, v_all, masks, sink_ref, o_ref)


def _latent_attention(q, k, v, cache_k4, cache_v4, layer, sink):
    bg, lg, _ = q.shape
    nb = lg // ATT_BLOCK
    sb = ATT_STEP_BLOCKS
    past = cache_k4.shape[2]
    tile = pl.BlockSpec((1, sb * ATT_BLOCK, ATT_Q_W), lambda b, i: (b, i, 0))
    kv_blocks = [pl.BlockSpec((1, ATT_BLOCK, ATT_KV_W),
                              lambda b, i, m=m: (b, jnp.clip(i * sb - 1 + m, 0, nb - 1), 0)) for m in range(sb + 2)]
    ctx = pl.BlockSpec((1, 1, past, ATT_KV_W), lambda b, i: (b, layer, 0, 0))
    return pl.pallas_call(
        functools.partial(_latent_attn_kernel, nb=nb),
        grid=(bg, nb // sb),
        in_specs=[tile] + kv_blocks + kv_blocks + [ctx, ctx, pl.BlockSpec(memory_space=pltpu.SMEM)],
        out_specs=tile,
        out_shape=jax.ShapeDtypeStruct((bg, lg, ATT_Q_W), BF16),
        compiler_params=_params("arbitrary", "arbitrary"),
        name="latent_attention",
    )(q, *([k] * (sb + 2)), *([v] * (sb + 2)), cache_k4, cache_v4, sink)


def _context_attn_kernel(q_ref, k_ref, v_ref, sink_ref, o_ref):
    k_all = k_ref[0].astype(BF16)
    v_all = v_ref[0].astype(BF16)
    for j in range(q_ref.shape[1] // ATT_BLOCK):
        _attend(q_ref, j * ATT_BLOCK, k_all, v_all, {}, sink_ref, o_ref)


def _context_attention(q, k, v, sink):
    bg, lg, _ = q.shape
    return pl.pallas_call(
        _context_attn_kernel,
        grid=(bg,),
        in_specs=[pl.BlockSpec((1, lg, ATT_Q_W), lambda b: (b, 0, 0)),
                  pl.BlockSpec((1, lg, ATT_KV_W), lambda b: (b, 0, 0)),
                  pl.BlockSpec((1, lg, ATT_KV_W), lambda b: (b, 0, 0)),
                  pl.BlockSpec(memory_space=pltpu.SMEM)],
        out_specs=pl.BlockSpec((1, lg, ATT_Q_W), lambda b: (b, 0, 0)),
        out_shape=jax.ShapeDtypeStruct((bg, lg, ATT_Q_W), BF16),
        compiler_params=_params("arbitrary"),
        name="context_attention",
    )(q, k, v, sink)


def _powers(er, th, e):
    mag = jnp.exp(er * e)
    return mag * jnp.cos(th * e), mag * jnp.sin(th * e)


def _ssm_prep_kernel(pc_ref, pr_ref, br_ref, bi_ref, cr_ref, ci_ref, crt_ref, cit_ref,
                     kv_out, rs_out, f_out, at_out):
    t = SSM_T
    p = SSM_STATE
    lane2 = lax.broadcasted_iota(jnp.int32, (1, 2 * t), 1)
    lane1 = lax.broadcasted_iota(jnp.int32, (1, t), 1)
    e_kv = [jnp.maximum(lane2 - t, 0).astype(F32), jnp.maximum(t - lane2, 0).astype(F32)]
    m_kv = [lane2 >= t, jnp.logical_and(lane2 >= 1, lane2 <= t)]
    e_rs = [(t - 1 - lane1).astype(F32), lane1.astype(F32)]
    e_f = [(lane1 + 1).astype(F32), (t - lane1).astype(F32)]

    bbr, bbi, pk, prs, pf = [], [], [], [], []
    for d in range(N_DIR):
        lr = pc_ref[d, 0, :, 0:1]
        li = pc_ref[d, 0, :, 1:2]
        step = jnp.exp(pc_ref[d, 0, :, 2:3])
        er = lr * step
        th = li * step
        mag = jnp.exp(er)
        ar = mag * jnp.cos(th)
        ai = mag * jnp.sin(th)
        den = lr * lr + li * li
        nr = ar - 1.0
        fr = (nr * lr + ai * li) / den
        fi = (ai * lr - nr * li) / den
        b_re = br_ref[d, 0]
        b_im = bi_ref[d, 0]
        bbr.append(fr * b_re - fi * b_im)
        bbi.append(fr * b_im + fi * b_re)
        kr, ki = _powers(er, th, e_kv[d])
        pk.append((jnp.where(m_kv[d], kr, 0.0), jnp.where(m_kv[d], ki, 0.0)))
        prs.append(_powers(er, th, e_rs[d]))
        pf.append(_powers(er, th, e_f[d]))
        lr_r = pr_ref[d, 0, 0:1, :]
        li_r = pr_ref[d, 0, 1:2, :]
        step_r = jnp.exp(pr_ref[d, 0, 2:3, :])
        mag_t = jnp.exp(lr_r * step_r * float(t))
        at_out[0, :, d * p:(d + 1) * p] = mag_t * jnp.cos(li_r * step_r * float(t))
        at_out[0, :, (N_DIR + d) * p:(N_DIR + d + 1) * p] = mag_t * jnp.sin(li_r * step_r * float(t))

    cmat = jnp.concatenate([cr_ref[0, 0], -ci_ref[0, 0], cr_ref[1, 0], -ci_ref[1, 0]], axis=1)
    for ci in range(SSM_GROUP_CH):
        rows = []
        for d in range(N_DIR):
            cbr = bbr[d][:, ci:ci + 1]
            cbi = bbi[d][:, ci:ci + 1]
            kr, ki = pk[d]
            rows += [kr * cbr - ki * cbi, kr * cbi + ki * cbr]
        kv_out[0, ci * SSM_GROUP_CH:(ci + 1) * SSM_GROUP_CH, :] = _mm_3pass(cmat, jnp.concatenate(rows, axis=0))
        for d in range(N_DIR):
            cbr = bbr[d][:, ci:ci + 1]
            cbi = bbi[d][:, ci:ci + 1]
            sr, si = prs[d]
            rs_out[0, d * p:(d + 1) * p, ci * t:(ci + 1) * t] = (sr * cbr - si * cbi).astype(BF16)
            rs_out[0, (N_DIR + d) * p:(N_DIR + d + 1) * p, ci * t:(ci + 1) * t] = (sr * cbi + si * cbr).astype(BF16)
    for co in range(SSM_GROUP_CH):
        for d in range(N_DIR):
            ccr = crt_ref[d, 0, :, co:co + 1]
            cci = cit_ref[d, 0, :, co:co + 1]
            fr_, fi_ = pf[d]
            f_out[0, d * p:(d + 1) * p, co * t:(co + 1) * t] = (ccr * fr_ - cci * fi_).astype(BF16)
            f_out[0, (N_DIR + d) * p:(N_DIR + d + 1) * p, co * t:(co + 1) * t] = (-(ccr * fi_ + cci * fr_)).astype(BF16)


def _ssm_prep(p):
    t = SSM_T
    g = SSM_GROUPS
    ns = N_DIR * 2 * SSM_STATE
    spec4 = lambda a, b: pl.BlockSpec((N_DIR, 1, a, b), lambda i: (0, i, 0, 0))
    return pl.pallas_call(
        _ssm_prep_kernel,
        grid=(g,),
        in_specs=[spec4(SSM_STATE, 3), spec4(3, SSM_STATE),
                  spec4(SSM_STATE, SSM_GROUP_CH), spec4(SSM_STATE, SSM_GROUP_CH),
                  spec4(SSM_GROUP_CH, SSM_STATE), spec4(SSM_GROUP_CH, SSM_STATE),
                  spec4(SSM_STATE, SSM_GROUP_CH), spec4(SSM_STATE, SSM_GROUP_CH)],
        out_specs=[pl.BlockSpec((1, SSM_GROUP_CH * SSM_GROUP_CH, 2 * t), lambda i: (i, 0, 0)),
                   pl.BlockSpec((1, ns, SSM_GROUP_CH * t), lambda i: (i, 0, 0)),
                   pl.BlockSpec((1, ns, SSM_GROUP_CH * t), lambda i: (i, 0, 0)),
                   pl.BlockSpec((1, 1, ns), lambda i: (i, 0, 0))],
        out_shape=[jax.ShapeDtypeStruct((g, SSM_GROUP_CH * SSM_GROUP_CH, 2 * t), F32),
                   jax.ShapeDtypeStruct((g, ns, SSM_GROUP_CH * t), BF16),
                   jax.ShapeDtypeStruct((g, ns, SSM_GROUP_CH * t), BF16),
                   jax.ShapeDtypeStruct((g, 1, ns), F32)],
        compiler_params=_params("arbitrary"),
        name="ssm_prep",
    )(p["ssm_pc"], p["ssm_pr"], p["ssm_b_re"], p["ssm_b_im"], p["ssm_c_re"], p["ssm_c_im"],
      p["ssm_c_re_t"], p["ssm_c_im_t"])


def _ssm_kernel(*refs, dims):
    ng = len(dims)
    u_refs = refs[:ng]
    kv_ref, rs_ref, f_ref, at_ref = refs[ng:ng + 4]
    h0_refs = refs[ng + 4:2 * ng + 4]
    d_ref = refs[2 * ng + 4]
    y_outs = refs[2 * ng + 5:3 * ng + 5]
    hfin_outs = refs[3 * ng + 5:4 * ng + 5]
    m_scr = refs[4 * ng + 5]
    scratch = refs[4 * ng + 6:]
    t = SSM_T
    nch = SSM_GROUP_CH
    g = pl.program_id(0)
    half = N_DIR * SSM_STATE

    def gen(ci, carry):
        r0 = pl.multiple_of(ci * t, t)
        for co in range(nch):
            row = kv_ref[0, pl.ds(ci * nch + co, 1), :]
            rolled = pltpu.roll(jnp.broadcast_to(row, (t, 2 * t)), t, 1, stride=1, stride_axis=0)
            m_scr[pl.ds(r0, t), co * t:(co + 1) * t] = rolled[:, :t].astype(BF16)
        return carry
    lax.fori_loop(0, nch, gen, 0)

    a_re = at_ref[0, :, :half]
    a_im = at_ref[0, :, half:]
    is_fwd = lax.broadcasted_iota(jnp.int32, (1, half), 1) < SSM_STATE
    for gi, (bg, nc) in enumerate(dims):
        u_ref, h0_ref, y_out, hfin_out = u_refs[gi], h0_refs[gi], y_outs[gi], hfin_outs[gi]
        s_scr, hpf_scr, hpb_scr = scratch[3 * gi:3 * gi + 3]
        u_flat = jnp.concatenate([u_ref[0, ci].astype(BF16) for ci in range(nch)], axis=1)

        s = lax.dot_general(u_flat, rs_ref[0], _NT, preferred_element_type=F32)
        s_scr[0] = s[:, :half]
        s_scr[1] = s[:, half:]

        def advance(re, im, rows, s_scr=s_scr):
            return (a_re * re - a_im * im + s_scr[0, rows, :], a_re * im + a_im * re + s_scr[1, rows, :])

        def scan(i, carry, bg=bg, nc=nc, hpf_scr=hpf_scr, hpb_scr=hpb_scr, advance=advance):
            f_re, f_im, b_re, b_im = carry
            rows_f = pl.ds(i, bg, stride=nc)
            rows_b = pl.ds(nc - 1 - i, bg, stride=nc)
            hpf_scr[0, rows_f, :] = f_re
            hpf_scr[1, rows_f, :] = f_im
            hpb_scr[0, rows_b, :] = b_re
            hpb_scr[1, rows_b, :] = b_im
            return advance(f_re, f_im, rows_f) + advance(b_re, b_im, rows_b)
        h0_re = h0_ref[0, :, :half]
        h0_im = h0_ref[0, :, half:]
        f_re, f_im, b_re, b_im = lax.fori_loop(0, nc, scan, (h0_re, h0_im, h0_re, h0_im))
        hfin_out[0, :, :half] = jnp.where(is_fwd, f_re, b_re)
        hfin_out[0, :, half:] = jnp.where(is_fwd, f_im, b_im)
        hprev = jnp.concatenate([jnp.where(is_fwd, hpf_scr[0], hpb_scr[0]),
                                 jnp.where(is_fwd, hpf_scr[1], hpb_scr[1])], axis=1)

        y = (jnp.dot(hprev.astype(BF16), f_ref[0], preferred_element_type=F32)
             + jnp.dot(u_flat, m_scr[...], preferred_element_type=F32))
        for co in range(nch):
            y_out[0, co] = y[:, co * t:(co + 1) * t] + d_ref[g * nch + co] * u_ref[0, co]


def _ssm_mix(uts, tables, h0s, d_skip, shapes):
    t = SSM_T
    g = SSM_GROUPS
    nch = SSM_GROUP_CH
    ns = N_DIR * 2 * SSM_STATE
    kv, rs, f, at = tables
    dims = tuple((bg, lg // t) for bg, lg in shapes)
    u_specs = [pl.BlockSpec((1, nch, bg * nc, t), lambda i: (i, 0, 0, 0)) for bg, nc in dims]
    h_specs = [pl.BlockSpec((1, bg, ns), lambda i: (i, 0, 0)) for bg, _ in dims]
    scratch = [pltpu.VMEM((nch * t, nch * t), BF16)]
    for bg, nc in dims:
        scratch += [pltpu.VMEM((2, bg * nc, ns // 2), F32)] * 3
    outs = pl.pallas_call(
        functools.partial(_ssm_kernel, dims=dims),
        grid=(g,),
        in_specs=u_specs + [pl.BlockSpec((1, nch * nch, 2 * t), lambda i: (i, 0, 0)),
                            pl.BlockSpec((1, ns, nch * t), lambda i: (i, 0, 0)),
                            pl.BlockSpec((1, ns, nch * t), lambda i: (i, 0, 0)),
                            pl.BlockSpec((1, 1, ns), lambda i: (i, 0, 0))]
                 + h_specs + [pl.BlockSpec(memory_space=pltpu.SMEM)],
        out_specs=u_specs + h_specs,
        out_shape=[jax.ShapeDtypeStruct((g, nch, bg * nc, t), F32) for bg, nc in dims]
                  + [jax.ShapeDtypeStruct((g, bg, ns), F32) for bg, _ in dims],
        scratch_shapes=scratch,
        compiler_params=_params("arbitrary"),
        name="ssm_mix",
    )(*[ut.reshape(g, nch, bg * nc, t) for ut, (bg, nc) in zip(uts, dims)], kv, rs, f, at, *h0s, d_skip)
    n = len(dims)
    return ([y.reshape(g * nch, bg * nc * t) for y, (bg, nc) in zip(outs[:n], dims)], list(outs[n:]))


def _head_blocks(x, head_masks):
    xb = x.astype(BF16)
    return jnp.concatenate([xb * m for m in head_masks], axis=0)


def _hmm(a, b_blocks):
    return jnp.dot(a.astype(BF16), b_blocks, preferred_element_type=F32)


def _tri_inverse_heads(a_list, same16, same32, eye, head_masks):
    n = a_list[0].shape[0]
    pw = [jnp.where(same16, -a, 0.0) for a in a_list]
    x = [eye + p for p in pw]
    pw = [_hmm(p, _head_blocks(p, head_masks)) for p in pw]
    yield
    for _ in range(2):
        both = [_hmm(jnp.concatenate([xi, p], axis=0), _head_blocks(p, head_masks)) for xi, p in zip(x, pw)]
        x = [xi + b[:n] for xi, b in zip(x, both)]
        pw = [b[n:] for b in both]
        yield
    x = [xi + _hmm(xi, _head_blocks(p, head_masks)) for xi, p in zip(x, pw)]
    yield
    for mask in (jnp.logical_and(same32, jnp.logical_not(same16)), jnp.logical_not(same32)):
        t = [_hmm(jnp.where(mask, a, 0.0), _head_blocks(xi, head_masks)) for a, xi in zip(a_list, x)]
        yield
        x = [xi - _hmm(xi, _head_blocks(ti, head_masks)) for xi, ti in zip(x, t)]
        yield
    return x


def _gdn_kernel(xf_ref, xb_ref, abf_ref, abb_ref, abtf_ref, abtb_ref, prow_ref, pcol_ref, ex_ref, s0_ref,
                of_ref, ob_ref, sfin_ref, s_scr, *, nt, tt):
    n = pl.program_id(1)
    ck = GDN_CHUNK
    cpt = tt // ck
    hw = GDN_HEADS * ck

    @pl.when(n == 0)
    def _():
        s_scr[...] = s0_ref[0]

    r = lax.broadcasted_iota(jnp.int32, (ck, hw), 0)
    lane = lax.broadcasted_iota(jnp.int32, (ck, hw), 1)
    c = lane % ck
    same16 = (r // 16) == (c // 16)
    same32 = (r // 32) == (c // 32)
    eye = (r == c).astype(F32)
    head_sel = [(lane // ck) == h for h in range(GDN_HEADS)]
    lane1 = lax.broadcasted_iota(jnp.int32, (1, hw), 1)
    head_masks = [((lane1 // ck) == h).astype(BF16) for h in range(GDN_HEADS)]
    row_in = lax.broadcasted_iota(jnp.int32, (tt, LANES), 0) % ck
    lane_in = lax.broadcasted_iota(jnp.int32, (GATE_W, tt), 1) % ck
    is_decay_lane = lax.broadcasted_iota(jnp.int32, (1, LANES), 1) < GATE_W // 2

    steps = [[] for _ in range(cpt)]
    for d, (x_ref, ab_ref, abt_ref, o_ref) in enumerate(((xf_ref, abf_ref, abtf_ref, of_ref),
                                                          (xb_ref, abb_ref, abtb_ref, ob_ref))):
        rev = d == 1
        ab = ab_ref[0]
        g_col = -jnp.exp(prow_ref[0:1, :]) * _softplus(ab + prow_ref[1:2, :])
        abt = abt_ref[0]
        g_row = -jnp.exp(pcol_ref[:, 0:1]) * _softplus(abt + pcol_ref[:, 1:2])
        sh = 1
        while sh < ck:
            if rev:
                g_col = g_col + jnp.where(row_in < ck - sh, pltpu.roll(g_col, tt - sh, 0), 0.0)
                g_row = g_row + jnp.where(lane_in < ck - sh, pltpu.roll(g_row, tt - sh, 1), 0.0)
            else:
                g_col = g_col + jnp.where(row_in >= sh, pltpu.roll(g_col, sh, 0), 0.0)
                g_row = g_row + jnp.where(lane_in >= sh, pltpu.roll(g_row, sh, 1), 0.0)
            sh *= 2
        spread = _mm_exact_rhs(jnp.where(is_decay_lane, g_col, jax.nn.sigmoid(ab)), ex_ref[d])
        incl = (r <= c) if rev else (r >= c)
        strict = (r < c) if rev else (r > c)
        last = 0 if rev else ck - 1
        for i, cc in enumerate(range(cpt - 1, -1, -1) if rev else range(cpt)):
            c0 = cc * ck
            gc = spread[c0:c0 + ck, :hw]
            gr = jnp.concatenate([g_row[d * GDN_HEADS + h:d * GDN_HEADS + h + 1, c0:c0 + ck]
                                  for h in range(GDN_HEADS)], axis=1)
            steps[i].append(dict(
                d=d, o_ref=o_ref, rows=slice(c0, c0 + ck), strict=strict,
                q=x_ref[0, c0:c0 + ck, :GDN_K_W], k=x_ref[0, c0:c0 + ck, GDN_K_W:2 * GDN_K_W],
                v=x_ref[0, c0:c0 + ck, 2 * GDN_K_W:],
                gc=gc, beta=spread[c0:c0 + ck, hw:], g_last=gc[last:last + 1, :],
                decay=jnp.exp(jnp.where(incl, gc - gr, -jnp.inf))))
    def chunk_local(items):
        for it in items:
            k_blocks = _head_blocks(it["k"], head_masks)
            kq = lax.dot_general(jnp.concatenate([it["k"], it["q"]], axis=0).astype(BF16), k_blocks, _NT,
                                 preferred_element_type=F32)
            it["kk"] = kq[:ck]
            it["attn"] = kq[ck:] * it["decay"]
        yield
        t_inv = yield from _tri_inverse_heads(
            [jnp.where(it["strict"], it["kk"] * it["decay"] * it["beta"], 0.0) for it in items],
            same16, same32, eye, head_masks)
        for it, ti in zip(items, t_inv):
            egc = jnp.exp(it["gc"])
            it["u"] = _hmm(ti, _head_blocks(it["v"] * it["beta"], head_masks))
            w = _hmm(ti, _head_blocks(it["k"] * (it["beta"] * egc), head_masks))
            it["w_qe"] = jnp.concatenate([w, it["q"] * egc], axis=0).astype(BF16)
            it["kd"] = (it["k"] * jnp.exp(it["g_last"] - it["gc"])).astype(BF16)
        yield

    state = [s_scr[d] for d in range(N_DIR)]

    def recurrence(some_steps):
        for st in some_steps:
            from_state = [_hmm(it["w_qe"], _head_blocks(state[it["d"]], head_masks)) for it in st]
            v_new = [it["u"] - fs[:ck] for it, fs in zip(st, from_state)]
            yield
            for it, fs, vn in zip(st, from_state, v_new):
                it["o_ref"][0, it["rows"], :] = fs[ck:] + _hmm(it["attn"], _head_blocks(vn, head_masks))
            for it, vn in zip(st, v_new):
                cross = lax.dot_general(it["kd"], vn.astype(BF16), _TN, preferred_element_type=F32)
                upd = functools.reduce(lambda a, b: a + b,
                                       [jnp.where(head_sel[h], cross[h * ck:(h + 1) * ck, :], 0.0)
                                        for h in range(GDN_HEADS)])
                state[it["d"]] = state[it["d"]] * jnp.exp(it["g_last"]) + upd
            yield

    per = GDN_GROUP_CHUNKS
    groups = [steps[i:i + per] for i in range(0, cpt, per)]
    for gi in range(len(groups) + 1):
        running = []
        if gi < len(groups):
            running.append(chunk_local([it for st in groups[gi] for it in st]))
        if gi >= 1:
            running.append(recurrence(groups[gi - 1]))
        while running:
            for gen in list(running):
                if next(gen, "done") == "done":
                    running.remove(gen)
    for d in range(N_DIR):
        s_scr[d] = state[d]

    @pl.when(n == nt - 1)
    def _():
        sfin_ref[0] = s_scr[...]


def _gdn_mix(qkv, gab, gabt, prow, pcol, s0, *, tt):
    bg, lg, w = qkv.shape
    nt = lg // tt
    fwd = lambda b, n: (b, n, 0)
    bwd = lambda b, n: (b, nt - 1 - n, 0)
    gate_col = jnp.arange(LANES)[None, :, None]
    want = (jnp.arange(N_DIR)[:, None, None] * GDN_HEADS + jnp.arange(GDN_V_W)[None, None, :] // GDN_DV)
    spread = jnp.concatenate([gate_col == want, gate_col == want + GATE_W // 2], axis=2).astype(BF16)
    st_spec = pl.BlockSpec((1, N_DIR, GDN_DK, GDN_V_W), lambda b, n: (b, 0, 0, 0))
    o_f, o_b, s_fin = pl.pallas_call(
        functools.partial(_gdn_kernel, nt=nt, tt=tt),
        grid=(bg, nt),
        in_specs=[pl.BlockSpec((1, tt, w), fwd), pl.BlockSpec((1, tt, w), bwd),
                  pl.BlockSpec((1, tt, LANES), fwd), pl.BlockSpec((1, tt, LANES), bwd),
                  pl.BlockSpec((1, GATE_W, tt), lambda b, n: (b, 0, n)),
                  pl.BlockSpec((1, GATE_W, tt), lambda b, n: (b, 0, nt - 1 - n)),
                  pl.BlockSpec((2, LANES), lambda b, n: (0, 0)),
                  pl.BlockSpec((GATE_W, 2), lambda b, n: (0, 0)),
                  pl.BlockSpec((N_DIR, LANES, 2 * GDN_V_W), lambda b, n: (0, 0, 0)),
                  st_spec],
        out_specs=[pl.BlockSpec((1, tt, GDN_V_W), fwd), pl.BlockSpec((1, tt, GDN_V_W), bwd), st_spec],
        out_shape=[jax.ShapeDtypeStruct((bg, lg, GDN_V_W), F32), jax.ShapeDtypeStruct((bg, lg, GDN_V_W), F32),
                   jax.ShapeDtypeStruct((bg, N_DIR, GDN_DK, GDN_V_W), F32)],
        scratch_shapes=[pltpu.VMEM((N_DIR, GDN_DK, GDN_V_W), F32)],
        compiler_params=_params("arbitrary", "arbitrary"),
        name="gdn_mix",
    )(qkv, qkv, gab, gab, gabt, gabt, prow, pcol, spread, s0)
    return o_f, o_b, s_fin


def _gelu_tanh(x):
    return 0.5 * x * (1.0 + jnp.tanh(math.sqrt(2.0 / math.pi) * (x + 0.044715 * (x * x * x))))


def _out_kernel(x_ref, ga_ref, shm_ref, scm_ref, gm_ref, attn_ref, yt_ref, of_ref, ob_ref, gz_ref,
                g2_ref, gng_ref, bd_ref, wglut_ref, bglu_ref, wo_ref, w1_ref, w2_ref, o_ref):
    x = x_ref[0]
    z = _gelu_tanh(yt_ref[...])
    gate = jax.nn.sigmoid(jnp.dot(wglut_ref[...], z.astype(BF16), preferred_element_type=F32) + bglu_ref[...])
    ssm_t = (z * gate).astype(BF16)
    o = of_ref[0] + ob_ref[0]
    gdn = o * lax.rsqrt(_seg_mean_sq(o, bd_ref) + EPS) * gng_ref[...] * _silu(gz_ref[0])
    mixed = (jnp.dot(attn_ref[0], wo_ref[:ATT_Q_W, :], preferred_element_type=F32)
             + lax.dot_general(ssm_t, wo_ref[ATT_Q_W:ATT_Q_W + SSM_WIDTH, :], _TN, preferred_element_type=F32)
             + jnp.dot(gdn.astype(BF16), wo_ref[ATT_Q_W + SSM_WIDTH:, :], preferred_element_type=F32))
    x1 = x + ga_ref[0] * mixed
    ms = jnp.mean(x1 * x1, axis=-1, keepdims=True)
    h2 = ((x1 * lax.rsqrt(ms + EPS) * g2_ref[...]) * (1.0 + scm_ref[0]) + shm_ref[0]).astype(BF16)
    acc = None
    fc = D_MODEL
    for j in range(D_FF // fc):
        a = jnp.maximum(jnp.dot(h2, w1_ref[:, j * fc:(j + 1) * fc], preferred_element_type=F32), 0.0)
        part = jnp.dot((a * a).astype(BF16), w2_ref[j * fc:(j + 1) * fc, :], preferred_element_type=F32)
        acc = part if acc is None else acc + part
    o_ref[0] = x1 + gm_ref[0] * acc


def _out_projection(x, mod3, mod_row, attn, yt, o_f, o_b, gz, p, consts, *, tm):
    bg, lg, _ = x.shape
    nt = lg // tm
    row = mod_row
    full = lambda shape: pl.BlockSpec(shape, lambda t, b: (0,) * len(shape), pipeline_mode=pl.Buffered(1))
    tok = lambda w: pl.BlockSpec((1, tm, w), lambda t, b: (b, t, 0))
    modc = lambda j: pl.BlockSpec((1, 1, D_MODEL), lambda t, b: (row(b), 0, j))
    mix_w = ATT_Q_W + SSM_WIDTH + GDN_V_W
    return pl.pallas_call(
        _out_kernel,
        grid=(nt, bg),
        in_specs=[tok(D_MODEL), modc(2), modc(3), modc(4), modc(5),
                  tok(ATT_Q_W), pl.BlockSpec((SSM_WIDTH, tm), lambda t, b: (0, b * nt + t)),
                  tok(GDN_V_W), tok(GDN_V_W), tok(GDN_V_W),
                  full((1, D_MODEL)), full((1, GDN_V_W)), full((GDN_V_W, GDN_V_W)),
                  full((SSM_WIDTH, SSM_WIDTH)), full((SSM_WIDTH, 1)),
                  full((mix_w, D_MODEL)), full((D_MODEL, D_FF)), full((D_FF, D_MODEL))],
        out_specs=tok(D_MODEL),
        out_shape=jax.ShapeDtypeStruct((bg, lg, D_MODEL), F32),
        compiler_params=_params("arbitrary", "arbitrary"),
        name="out_projection",
    )(x, mod3, mod3, mod3, mod3, attn, yt, o_f, o_b, gz,
      p["norm2_g"], p["gdn_norm_g"], consts["bd_mean"][:GDN_V_W, :GDN_V_W],
      p["w_glu_t"], p["b_glu"], p["w_out"], p["w_ff1"], p["w_ff2"])


def _constants(max_len):
    n_freq = HEAD_DIM // 4
    rows = jnp.repeat(jnp.arange(max_len // GRID_W, dtype=F32), GRID_W)
    cols = jnp.tile(jnp.arange(GRID_W, dtype=F32), max_len // GRID_W)
    inv_freq = jnp.power(ROPE_BASE, -jnp.arange(n_freq, dtype=F32) / n_freq)
    ang = jnp.concatenate([rows[:, None] * inv_freq, cols[:, None] * inv_freq], axis=-1)
    cos = jnp.repeat(jnp.cos(ang), 2, axis=-1)
    sin = jnp.repeat(jnp.sin(ang), 2, axis=-1) * jnp.tile(jnp.array([-1.0, 1.0], F32), HEAD_DIM // 2)
    seg = jnp.arange(ATT_Q_W) // HEAD_DIM
    same = seg[:, None] == seg[None, :]
    return {
        "cos": jnp.tile(cos, (1, LANES // HEAD_DIM)),
        "sin": jnp.tile(sin, (1, LANES // HEAD_DIM)),
        "bd_mean": jnp.where(same, 1.0 / HEAD_DIM, 0.0).astype(BF16),
    }


def _layer_params(w):
    w_in = w["w_in"]
    off = [0]
    for size in (ATT_Q_W, ATT_KV_W, ATT_KV_W, SSM_WIDTH, GDN_K_W, GDN_K_W, GDN_V_W, GDN_V_W, GATE_W):
        off.append(off[-1] + size)
    wab = w_in[:, off[8]:off[9]]
    col = lambda a: a.reshape(-1, 1)
    pad_lanes = lambda a: jnp.pad(a, ((0, 0), (0, LANES - a.shape[1])))
    alog = w["gdn_a_log"].reshape(1, -1)
    dtb = w["gdn_dt_bias"].reshape(1, -1)
    zeros = jnp.zeros_like(alog)
    ls = jnp.broadcast_to(w["ssm_log_step"][..., None], (N_DIR, SSM_GROUPS, SSM_STATE))
    pair_order = jnp.arange(N_HEADS).reshape(N_KV_HEADS, GQA).T.reshape(-1)
    head_cols = (pair_order[:, None] * HEAD_DIM + jnp.arange(HEAD_DIM)[None, :]).reshape(-1)
    w_out = w["w_out"]
    w_out = jnp.concatenate([w_out[:ATT_Q_W][head_cols], w_out[ATT_Q_W:]], axis=0)
    return {
        "norm1_g": w["norm1_g"].reshape(1, -1), "norm2_g": w["norm2_g"].reshape(1, -1),
        "wq": w_in[:, off[0]:off[1]][:, head_cols].astype(BF16), "wkv": w_in[:, off[1]:off[3]].astype(BF16),
        "wut": w_in[:, off[3]:off[4]].T.astype(BF16), "wg": w_in[:, off[4]:off[7]].astype(BF16),
        "wz": w_in[:, off[7]:off[8]].astype(BF16),
        "wab": pad_lanes(wab).astype(BF16), "wabt": wab.T.astype(BF16),
        "q_norm_g": jnp.tile(w["q_norm_g"], N_HEADS).reshape(1, -1),
        "k_norm_g": jnp.tile(w["k_norm_g"], N_KV_HEADS).reshape(1, -1),
        "attn_sink": w["attn_sink"],
        "ssm_pc": jnp.stack([w["ssm_lam_re"], w["ssm_lam_im"], ls], axis=-1),
        "ssm_pr": jnp.stack([w["ssm_lam_re"], w["ssm_lam_im"], ls], axis=-2),
        "ssm_b_re": w["ssm_b_re"], "ssm_b_im": w["ssm_b_im"],
        "ssm_c_re": w["ssm_c_re"], "ssm_c_im": w["ssm_c_im"],
        "ssm_c_re_t": jnp.swapaxes(w["ssm_c_re"], -1, -2), "ssm_c_im_t": jnp.swapaxes(w["ssm_c_im"], -1, -2),
        "ssm_d": w["ssm_d"],
        "w_glu_t": w["ssm_w_glu"].T.astype(BF16), "b_glu": col(w["ssm_b_glu"]),
        "gdn_conv_w": w["gdn_conv_w"],
        "gdn_prow": pad_lanes(jnp.concatenate([jnp.concatenate([alog, zeros], 1), jnp.concatenate([dtb, zeros], 1)], 0)),
        "gdn_pcol": jnp.concatenate([jnp.concatenate([alog, zeros], 1), jnp.concatenate([dtb, zeros], 1)], 0).T,
        "gdn_norm_g": jnp.tile(w["gdn_norm_g"], GDN_HEADS).reshape(1, -1),
        "w_out": w_out.astype(BF16), "w_ff1": w["w_ff1"].astype(BF16), "w_ff2": w["w_ff2"].astype(BF16),
    }


def _ssm_states_to_lanes(h0):
    bg, depth = h0.shape[:2]
    return h0.transpose(1, 4, 0, 3, 2, 5).reshape(depth, SSM_GROUPS, bg, 2 * N_DIR * SSM_STATE)


def _gdn_states_to_lanes(s0):
    bg, depth = s0.shape[:2]
    return s0.transpose(1, 0, 2, 4, 3, 5).reshape(depth, bg, N_DIR, GDN_DK, GDN_V_W)


def _ssm_lanes_to_states(h):
    depth, _, bg, _ = h.shape
    return h.reshape(depth, SSM_GROUPS, bg, 2, N_DIR, SSM_STATE).transpose(2, 0, 4, 3, 1, 5)


def _gdn_lanes_to_states(s):
    depth, bg = s.shape[:2]
    return s.reshape(depth, bg, N_DIR, GDN_DK, GDN_HEADS, GDN_DV).transpose(1, 0, 2, 4, 3, 5)


def _layer(groups, mod3, p, consts, layer):
    tables = _ssm_prep(p)
    proj = []
    for gr in groups:
        lg = gr["x"].shape[1]
        proj.append(_in_projection(gr["x"], mod3, gr["mod_row"], p, consts, rope=gr["ctx_kv"] is not None,
                                   tm=min(lg, 512)))
    shapes = [gr["x"].shape[:2] for gr in groups]
    yts, h_fins = _ssm_mix([pr[3] for pr in proj], tables, [gr["ssm_h0"] for gr in groups], p["ssm_d"], shapes)
    xs, aux = [], []
    for gr, pr, yt, h_fin in zip(groups, proj, yts, h_fins):
        bg, lg, _ = gr["x"].shape
        q, k, v, _, qkv, gz, gab, gabt = pr
        if gr["ctx_kv"] is not None:
            attn = _latent_attention(q, k, v, gr["ctx_kv"][0], gr["ctx_kv"][1], layer, p["attn_sink"])
        else:
            attn = _context_attention(q, k, v, p["attn_sink"])
        o_f, o_b, s_fin = _gdn_mix(qkv, gab, gabt, p["gdn_prow"], p["gdn_pcol"], gr["gdn_s0"], tt=min(lg, GDN_TILE))
        xs.append(_out_projection(gr["x"], mod3, gr["mod_row"], attn, yt, o_f, o_b, gz, p, consts, tm=min(lg, 512)))
        aux.append((k, v, h_fin, s_fin))
    return xs, aux


def kernel(x_prompt, x_sample, c, cache_k, cache_v, state_ssm, state_gdn, c_ctx, norm1_g, norm2_g, w_mod, b_mod, w_in, q_norm_g, k_norm_g, attn_sink, ssm_lam_re, ssm_lam_im, ssm_log_step, ssm_b_re, ssm_b_im, ssm_c_re, ssm_c_im, ssm_d, ssm_w_glu, ssm_b_glu, gdn_conv_w, gdn_a_log, gdn_dt_bias, gdn_norm_g, w_out, w_ff1, w_ff2):
    w = dict(norm1_g=norm1_g, norm2_g=norm2_g, w_in=w_in, q_norm_g=q_norm_g, k_norm_g=k_norm_g, attn_sink=attn_sink,
             ssm_lam_re=ssm_lam_re, ssm_lam_im=ssm_lam_im, ssm_log_step=ssm_log_step, ssm_b_re=ssm_b_re,
             ssm_b_im=ssm_b_im, ssm_c_re=ssm_c_re, ssm_c_im=ssm_c_im, ssm_d=ssm_d, ssm_w_glu=ssm_w_glu,
             ssm_b_glu=ssm_b_glu, gdn_conv_w=gdn_conv_w, gdn_a_log=gdn_a_log, gdn_dt_bias=gdn_dt_bias,
             gdn_norm_g=gdn_norm_g, w_out=w_out, w_ff1=w_ff1, w_ff2=w_ff2)
    n_ctx, seq, _ = x_prompt.shape
    n_dec, dec_seq, _ = x_sample.shape
    depth = w_in.shape[0]
    past = cache_k.shape[2]

    n_rows = -(-(n_dec + 1) // SUBLANES) * SUBLANES
    cond = jnp.zeros((n_rows, D_MODEL), F32).at[:n_dec].set(c).at[n_dec].set(c_ctx)
    mod = _modulation(cond, w_mod, b_mod)

    consts = _constants(max(seq, dec_seq))
    cache_k4 = cache_k.reshape(n_dec, depth, past, ATT_KV_W)
    cache_v4 = cache_v.reshape(n_dec, depth, past, ATT_KV_W)
    ssm_zero = jnp.zeros((SSM_GROUPS, n_ctx, 2 * N_DIR * SSM_STATE), F32)
    gdn_zero = jnp.zeros((n_ctx, N_DIR, GDN_DK, GDN_V_W), F32)
    ssm_h0 = _ssm_states_to_lanes(state_ssm)
    gdn_s0 = _gdn_states_to_lanes(state_gdn)

    params = jax.vmap(_layer_params)(w)
    xp, xs = x_prompt, x_sample
    ks, vs, ss, gs = [], [], [], []
    for l in range(depth):
        p = {name: value[l] for name, value in params.items()}
        mod3 = mod[l].reshape(n_rows, 1, N_MOD * D_MODEL)
        groups = [dict(x=xp, mod_row=lambda b: n_dec, ctx_kv=None, ssm_h0=ssm_zero, gdn_s0=gdn_zero),
                  dict(x=xs, mod_row=lambda b: b, ctx_kv=(cache_k4, cache_v4), ssm_h0=ssm_h0[l], gdn_s0=gdn_s0[l])]
        (xp, xs), ((k_l, v_l, s_l, g_l), _) = _layer(groups, mod3, p, consts, l)
        ks.append(k_l.reshape(n_ctx, seq, N_KV_HEADS, HEAD_DIM))
        vs.append(v_l.reshape(n_ctx, seq, N_KV_HEADS, HEAD_DIM))
        ss.append(s_l)
        gs.append(g_l)
    return (xp, xs, jnp.stack(ks, axis=1), jnp.stack(vs, axis=1),
            _ssm_lanes_to_states(jnp.stack(ss)), _gdn_lanes_to_states(jnp.stack(gs)))
```

```python
import functools
import math

import jax
import jax.numpy as jnp
from jax import lax
from jax.experimental import pallas as pl
from jax.experimental.pallas import tpu as pltpu

F32 = jnp.float32
BF16 = jnp.bfloat16
EPS = 1e-6
NEG_INF = -1e30
LOG2_E = math.log2(math.e)

D_MODEL = 1024
DEPTH = 4
GRID_W = 64
N_DIR = 2
N_HEADS = 8
N_KV_HEADS = 2
GQA = N_HEADS // N_KV_HEADS
HEAD_DIM = 64
ATT_BLOCK = 128
ROPE_BASE = 10000.0
SSM_GROUP_CH = 16
SSM_GROUPS = 16
SSM_WIDTH = SSM_GROUPS * SSM_GROUP_CH
SSM_STATE = 64
GDN_HEADS = 4
GDN_DK = 64
GDN_DV = 64
GDN_K_W = GDN_HEADS * GDN_DK
GDN_V_W = GDN_HEADS * GDN_DV
GDN_QKV_W = 2 * GDN_K_W + GDN_V_W
GDN_CHUNK = 64
ATT_Q_W = N_HEADS * HEAD_DIM
ATT_KV_W = N_KV_HEADS * HEAD_DIM
D_FF = 4 * D_MODEL
N_MOD = 6
GATE_W = 2 * N_DIR * GDN_HEADS

LANES = 128
SUBLANES = 8
SSM_T = LANES
ATT_STEP_BLOCKS = 8
GDN_TILE = 1024
GDN_GROUP_CHUNKS = 4
VMEM_LIMIT = 56 * 1024 * 1024

_NT = (((1,), (1,)), ((), ()))
_TN = (((0,), (0,)), ((), ()))


def _mm(a, b):
    return jnp.dot(a.astype(BF16), b.astype(BF16), preferred_element_type=F32)


def _mm_nt(a, b):
    return lax.dot_general(a.astype(BF16), b.astype(BF16), _NT, preferred_element_type=F32)


def _mm_tn(a, b):
    return lax.dot_general(a.astype(BF16), b.astype(BF16), _TN, preferred_element_type=F32)


def _split3(a):
    hi = a.astype(BF16)
    r1 = a - hi.astype(F32)
    mid = r1.astype(BF16)
    lo = (r1 - mid.astype(F32)).astype(BF16)
    return hi, mid, lo


def _mm_exact_rhs(a, b_bf16):
    hi, mid, lo = _split3(a)
    return (jnp.dot(hi, b_bf16, preferred_element_type=F32)
            + jnp.dot(mid, b_bf16, preferred_element_type=F32)
            + jnp.dot(lo, b_bf16, preferred_element_type=F32))


def _mm_3pass(a, b):
    ah, am, _ = _split3(a)
    bh, bm, _ = _split3(b)
    d = lambda x, y: jnp.dot(x, y, preferred_element_type=F32)
    return d(ah, bh) + (d(ah, bm) + d(am, bh))


def _silu(x):
    return x * jax.nn.sigmoid(x)


def _softplus(x):
    return jnp.maximum(x, 0.0) + jnp.log1p(jnp.exp(-jnp.abs(x)))


def _params(*sem):
    return pltpu.CompilerParams(dimension_semantics=sem, vmem_limit_bytes=VMEM_LIMIT)


def _mod_kernel(cond_ref, w_ref, b_ref, o_ref):
    c = cond_ref[...]
    o_ref[0] = _mm(_silu(c), w_ref[0]) + b_ref[0]


def _modulation(cond, w_mod, b_mod):
    rows = cond.shape[0]
    cn = 1536
    return pl.pallas_call(
        _mod_kernel,
        grid=(DEPTH, N_MOD * D_MODEL // cn),
        in_specs=[pl.BlockSpec((rows, D_MODEL), lambda l, j: (0, 0)),
                  pl.BlockSpec((1, D_MODEL, cn), lambda l, j: (l, 0, j)),
                  pl.BlockSpec((1, 1, cn), lambda l, j: (l, 0, j))],
        out_specs=pl.BlockSpec((1, rows, cn), lambda l, j: (l, 0, j)),
        out_shape=jax.ShapeDtypeStruct((DEPTH, rows, N_MOD * D_MODEL), F32),
        compiler_params=_params("arbitrary", "arbitrary"),
        name="modulation",
    )(cond, w_mod, b_mod.reshape(DEPTH, 1, N_MOD * D_MODEL))


def _seg_mean_sq(x, bd_ref):
    n = x.shape[-1]
    return jnp.dot((x * x).astype(BF16), bd_ref[:n, :n], preferred_element_type=F32)


def _rope(x, cos, sin_signed):
    n = x.shape[-1]
    nxt = pltpu.roll(x, n - 1, 1)
    prv = pltpu.roll(x, 1, 1)
    lane = lax.broadcasted_iota(jnp.int32, x.shape, 1)
    swapped = jnp.where(lane % 2 == 0, nxt, prv)
    return x * cos + swapped * sin_signed


def _in_kernel(x_ref, xp_ref, xn_ref, sh_ref, sc_ref, g1_ref, wq_ref, wkv_ref, wut_ref, wg_ref, wz_ref, wab_ref,
               wabt_ref, qg_ref, kg_ref, bd_ref, cos_ref, sin_ref, cw_ref,
               q_out, k_out, v_out, ut_out, g_out, z_out, ab_out, abt_out, *, rope, nt):
    i = pl.program_id(0)
    tm = x_ref.shape[1]
    x = jnp.concatenate([x_ref[0], xp_ref[0], xn_ref[0]], axis=0)
    ms = jnp.mean(x * x, axis=-1, keepdims=True)
    h = (x * lax.rsqrt(ms + EPS) * g1_ref[...]) * (1.0 + sc_ref[0]) + sh_ref[0]
    hb_all = h.astype(BF16)
    hb = hb_all[:tm]

    zg = jnp.dot(hb_all, wg_ref[...], preferred_element_type=F32)
    z = zg[:tm]
    prev_row = jnp.where(i > 0, zg[tm + SUBLANES - 1:tm + SUBLANES], 0.0)
    next_row = jnp.where(i < nt - 1, zg[tm + SUBLANES:tm + SUBLANES + 1], 0.0)
    row = lax.broadcasted_iota(jnp.int32, z.shape, 0)
    z_m1 = jnp.where(row == 0, prev_row, pltpu.roll(z, 1, 0))
    z_p1 = jnp.where(row == tm - 1, next_row, pltpu.roll(z, tm - 1, 0))
    y = _silu(z_m1 * cw_ref[0:1, :] + z * cw_ref[1:2, :] + z_p1 * cw_ref[2:3, :])
    gq = y[:, :GDN_K_W]
    gk = y[:, GDN_K_W:2 * GDN_K_W]
    g_out[0, :, :GDN_K_W] = gq * lax.rsqrt(_seg_mean_sq(gq, bd_ref) * GDN_DK + EPS) * (GDN_DK ** -0.5)
    g_out[0, :, GDN_K_W:2 * GDN_K_W] = gk * lax.rsqrt(_seg_mean_sq(gk, bd_ref) * GDN_DK + EPS)
    g_out[0, :, 2 * GDN_K_W:] = y[:, 2 * GDN_K_W:]

    q = jnp.dot(hb, wq_ref[...], preferred_element_type=F32)
    q = q * lax.rsqrt(_seg_mean_sq(q, bd_ref) + EPS) * qg_ref[...]
    kv = jnp.dot(hb, wkv_ref[...], preferred_element_type=F32)
    k = kv[:, :ATT_KV_W]
    k = k * lax.rsqrt(_seg_mean_sq(k, bd_ref) + EPS) * kg_ref[...]
    if rope:
        cos = cos_ref[...]
        sin = sin_ref[...]
        q = _rope(q, jnp.concatenate([cos] * (ATT_Q_W // LANES), axis=1),
                  jnp.concatenate([sin] * (ATT_Q_W // LANES), axis=1))
        k = _rope(k, cos, sin)
    q_out[0] = (q * (HEAD_DIM ** -0.5 * LOG2_E)).astype(BF16)
    k_out[0] = k
    v_out[0] = kv[:, ATT_KV_W:]
    ut_out[...] = lax.dot_general(wut_ref[...], hb, _NT, preferred_element_type=F32).astype(BF16)
    z_out[0] = jnp.dot(hb, wz_ref[...], preferred_element_type=F32)
    ab_out[0] = jnp.dot(hb, wab_ref[...], preferred_element_type=F32)
    abt_out[0] = lax.dot_general(wabt_ref[...], hb, _NT, preferred_element_type=F32)


def _in_projection(x, mod3, mod_row, p, consts, *, rope, tm):
    bg, lg, _ = x.shape
    nt = lg // tm
    row = mod_row
    full = lambda shape: pl.BlockSpec(shape, lambda t, b: (0,) * len(shape))
    tok = lambda w: pl.BlockSpec((1, tm, w), lambda t, b: (b, t, 0))
    rb = tm // SUBLANES
    in_specs = [
        tok(D_MODEL),
        pl.BlockSpec((1, SUBLANES, D_MODEL), lambda t, b: (b, jnp.maximum(t * rb - 1, 0), 0)),
        pl.BlockSpec((1, SUBLANES, D_MODEL), lambda t, b: (b, jnp.minimum((t + 1) * rb, lg // SUBLANES - 1), 0)),
        pl.BlockSpec((1, 1, D_MODEL), lambda t, b: (row(b), 0, 0)),
        pl.BlockSpec((1, 1, D_MODEL), lambda t, b: (row(b), 0, 1)),
        full((1, D_MODEL)),
        full((D_MODEL, ATT_Q_W)), full((D_MODEL, 2 * ATT_KV_W)), full((SSM_WIDTH, D_MODEL)),
        full((D_MODEL, GDN_QKV_W)), full((D_MODEL, GDN_V_W)), full((D_MODEL, LANES)), full((GATE_W, D_MODEL)),
        full((1, ATT_Q_W)), full((1, ATT_KV_W)), full((ATT_Q_W, ATT_Q_W)),
        pl.BlockSpec((tm, LANES), lambda t, b: (t, 0)),
        pl.BlockSpec((tm, LANES), lambda t, b: (t, 0)),
        full((3, GDN_QKV_W)),
    ]
    out_specs = [
        tok(ATT_Q_W), tok(ATT_KV_W), tok(ATT_KV_W),
        pl.BlockSpec((SSM_WIDTH, tm), lambda t, b: (0, b * nt + t)),
        tok(GDN_QKV_W), tok(GDN_V_W), tok(LANES),
        pl.BlockSpec((1, GATE_W, tm), lambda t, b: (b, 0, t)),
    ]
    out_shape = [
        jax.ShapeDtypeStruct((bg, lg, ATT_Q_W), BF16),
        jax.ShapeDtypeStruct((bg, lg, ATT_KV_W), F32),
        jax.ShapeDtypeStruct((bg, lg, ATT_KV_W), F32),
        jax.ShapeDtypeStruct((SSM_WIDTH, bg * lg), BF16),
        jax.ShapeDtypeStruct((bg, lg, GDN_QKV_W), F32),
        jax.ShapeDtypeStruct((bg, lg, GDN_V_W), F32),
        jax.ShapeDtypeStruct((bg, lg, LANES), F32),
        jax.ShapeDtypeStruct((bg, GATE_W, lg), F32),
    ]
    return pl.pallas_call(
        functools.partial(_in_kernel, rope=rope, nt=nt),
        grid=(nt, bg), in_specs=in_specs, out_specs=out_specs, out_shape=out_shape,
        compiler_params=_params("arbitrary", "arbitrary"),
        name="in_projection",
    )(x, x, x, mod3, mod3, p["norm1_g"], p["wq"], p["wkv"], p["wut"], p["wg"], p["wz"], p["wab"], p["wabt"],
      p["q_norm_g"], p["k_norm_g"], consts["bd_mean"], consts["cos"][:lg], consts["sin"][:lg], p["gdn_conv_w"])


def _attend(q_ref, r0, k_all, v_all, masks, sink_ref, o_ref):
    nq = ATT_BLOCK
    lane = lax.broadcasted_iota(jnp.int32, (1, LANES), 1)
    low = lane < HEAD_DIM
    keep_low = low.astype(BF16)
    keep_high = 1 - keep_low
    tiles = [q_ref[0, r0:r0 + nq, i * LANES:(i + 1) * LANES] for i in range(GQA)]
    q_rows = jnp.concatenate([t * keep_low for t in tiles] + [t * keep_high for t in tiles], axis=0)
    s = lax.dot_general(q_rows, k_all, _NT, preferred_element_type=F32)
    n_keys = s.shape[1]
    probs, sink_terms = [], []
    for h in range(N_HEADS):
        sh = s[h * nq:(h + 1) * nq, :]
        if masks:
            cols = []
            for c0 in range(0, n_keys, ATT_BLOCK):
                piece = sh[:, c0:c0 + ATT_BLOCK]
                cols.append(jnp.where(masks[c0], piece, NEG_INF) if c0 in masks else piece)
            sh = jnp.concatenate(cols, axis=1)
        sink = sink_ref[h] * LOG2_E
        m = jnp.maximum(jnp.max(sh, axis=-1, keepdims=True), sink)
        sink_terms.append(jnp.exp2(sink - m))
        probs.append(jnp.exp2(sh - m).astype(BF16))
    v_ext = jnp.concatenate([v_all, jnp.ones((n_keys, LANES), BF16)], axis=1)
    o = jnp.dot(jnp.concatenate(probs, axis=0), v_ext, preferred_element_type=F32)

    def normalised(h):
        rows = o[h * nq:(h + 1) * nq]
        return rows[:, :LANES] * (1.0 / (rows[:, LANES:] + sink_terms[h]))
    for i in range(GQA):
        o_ref[0, r0:r0 + nq, i * LANES:(i + 1) * LANES] = jnp.where(
            low, normalised(i), normalised(GQA + i)).astype(o_ref.dtype)


def _latent_attn_kernel(*refs, nb):
    nk = ATT_STEP_BLOCKS + 2
    q_ref = refs[0]
    k_refs = refs[1:1 + nk]
    v_refs = refs[1 + nk:1 + 2 * nk]
    ck_ref, cv_ref, sink_ref, o_ref = refs[1 + 2 * nk:]
    i = pl.program_id(1)
    r = lax.broadcasted_iota(jnp.int32, (ATT_BLOCK, ATT_BLOCK), 0)
    c = lax.broadcasted_iota(jnp.int32, (ATT_BLOCK, ATT_BLOCK), 1)
    kb = [ref[0].astype(BF16) for ref in k_refs]
    vb = [ref[0].astype(BF16) for ref in v_refs]
    ck = ck_ref[0, 0].astype(BF16)
    cv = cv_ref[0, 0].astype(BF16)
    for j in range(ATT_STEP_BLOCKS):
        blk = i * ATT_STEP_BLOCKS + j
        masks = {0: jnp.logical_and(c >= r, blk > 0), 2 * ATT_BLOCK: jnp.logical_and(c <= r, blk < nb - 1)}
        k_all = jnp.concatenate(kb[j:j + 3] + [ck], axis=0)
        v_all = jnp.concatenate(vb[j:j + 3] + [cv], axis=0)
        _attend(q_ref, j * ATT_BLOCK, k_all, v_all, masks, sink_ref, o_ref)


def _latent_attention(q, k, v, cache_k4, cache_v4, layer, sink):
    bg, lg, _ = q.shape
    nb = lg // ATT_BLOCK
    sb = ATT_STEP_BLOCKS
    past = cache_k4.shape[2]
    tile = pl.BlockSpec((1, sb * ATT_BLOCK, ATT_Q_W), lambda b, i: (b, i, 0))
    kv_blocks = [pl.BlockSpec((1, ATT_BLOCK, ATT_KV_W),
                              lambda b, i, m=m: (b, jnp.clip(i * sb - 1 + m, 0, nb - 1), 0)) for m in range(sb + 2)]
    ctx = pl.BlockSpec((1, 1, past, ATT_KV_W), lambda b, i: (b, layer, 0, 0))
    return pl.pallas_call(
        functools.partial(_latent_attn_kernel, nb=nb),
        grid=(bg, nb // sb),
        in_specs=[tile] + kv_blocks + kv_blocks + [ctx, ctx, pl.BlockSpec(memory_space=pltpu.SMEM)],
        out_specs=tile,
        out_shape=jax.ShapeDtypeStruct((bg, lg, ATT_Q_W), BF16),
        compiler_params=_params("arbitrary", "arbitrary"),
        name="latent_attention",
    )(q, *([k] * (sb + 2)), *([v] * (sb + 2)), cache_k4, cache_v4, sink)


def _context_attn_kernel(q_ref, k_ref, v_ref, sink_ref, o_ref):
    k_all = k_ref[0].astype(BF16)
    v_all = v_ref[0].astype(BF16)
    for j in range(q_ref.shape[1] // ATT_BLOCK):
        _attend(q_ref, j * ATT_BLOCK, k_all, v_all, {}, sink_ref, o_ref)


def _context_attention(q, k, v, sink):
    bg, lg, _ = q.shape
    return pl.pallas_call(
        _context_attn_kernel,
        grid=(bg,),
        in_specs=[pl.BlockSpec((1, lg, ATT_Q_W), lambda b: (b, 0, 0)),
                  pl.BlockSpec((1, lg, ATT_KV_W), lambda b: (b, 0, 0)),
                  pl.BlockSpec((1, lg, ATT_KV_W), lambda b: (b, 0, 0)),
                  pl.BlockSpec(memory_space=pltpu.SMEM)],
        out_specs=pl.BlockSpec((1, lg, ATT_Q_W), lambda b: (b, 0, 0)),
        out_shape=jax.ShapeDtypeStruct((bg, lg, ATT_Q_W), BF16),
        compiler_params=_params("arbitrary"),
        name="context_attention",
    )(q, k, v, sink)


def _powers(er, th, e):
    mag = jnp.exp(er * e)
    return mag * jnp.cos(th * e), mag * jnp.sin(th * e)


def _ssm_prep_kernel(pc_ref, pr_ref, br_ref, bi_ref, cr_ref, ci_ref, crt_ref, cit_ref,
                     kv_out, rs_out, f_out, at_out):
    t = SSM_T
    p = SSM_STATE
    lane2 = lax.broadcasted_iota(jnp.int32, (1, 2 * t), 1)
    lane1 = lax.broadcasted_iota(jnp.int32, (1, t), 1)
    e_kv = [jnp.maximum(lane2 - t, 0).astype(F32), jnp.maximum(t - lane2, 0).astype(F32)]
    m_kv = [lane2 >= t, jnp.logical_and(lane2 >= 1, lane2 <= t)]
    e_rs = [(t - 1 - lane1).astype(F32), lane1.astype(F32)]
    e_f = [(lane1 + 1).astype(F32), (t - lane1).astype(F32)]

    bbr, bbi, pk, prs, pf = [], [], [], [], []
    for d in range(N_DIR):
        lr = pc_ref[d, 0, :, 0:1]
        li = pc_ref[d, 0, :, 1:2]
        step = jnp.exp(pc_ref[d, 0, :, 2:3])
        er = lr * step
        th = li * step
        mag = jnp.exp(er)
        ar = mag * jnp.cos(th)
        ai = mag * jnp.sin(th)
        den = lr * lr + li * li
        nr = ar - 1.0
        fr = (nr * lr + ai * li) / den
        fi = (ai * lr - nr * li) / den
        b_re = br_ref[d, 0]
        b_im = bi_ref[d, 0]
        bbr.append(fr * b_re - fi * b_im)
        bbi.append(fr * b_im + fi * b_re)
        kr, ki = _powers(er, th, e_kv[d])
        pk.append((jnp.where(m_kv[d], kr, 0.0), jnp.where(m_kv[d], ki, 0.0)))
        prs.append(_powers(er, th, e_rs[d]))
        pf.append(_powers(er, th, e_f[d]))
        lr_r = pr_ref[d, 0, 0:1, :]
        li_r = pr_ref[d, 0, 1:2, :]
        step_r = jnp.exp(pr_ref[d, 0, 2:3, :])
        mag_t = jnp.exp(lr_r * step_r * float(t))
        at_out[0, :, d * p:(d + 1) * p] = mag_t * jnp.cos(li_r * step_r * float(t))
        at_out[0, :, (N_DIR + d) * p:(N_DIR + d + 1) * p] = mag_t * jnp.sin(li_r * step_r * float(t))

    cmat = jnp.concatenate([cr_ref[0, 0], -ci_ref[0, 0], cr_ref[1, 0], -ci_ref[1, 0]], axis=1)
    for ci in range(SSM_GROUP_CH):
        rows = []
        for d in range(N_DIR):
            cbr = bbr[d][:, ci:ci + 1]
            cbi = bbi[d][:, ci:ci + 1]
            kr, ki = pk[d]
            rows += [kr * cbr - ki * cbi, kr * cbi + ki * cbr]
        kv_out[0, ci * SSM_GROUP_CH:(ci + 1) * SSM_GROUP_CH, :] = _mm_3pass(cmat, jnp.concatenate(rows, axis=0))
        for d in range(N_DIR):
            cbr = bbr[d][:, ci:ci + 1]
            cbi = bbi[d][:, ci:ci + 1]
            sr, si = prs[d]
            rs_out[0, d * p:(d + 1) * p, ci * t:(ci + 1) * t] = (sr * cbr - si * cbi).astype(BF16)
            rs_out[0, (N_DIR + d) * p:(N_DIR + d + 1) * p, ci * t:(ci + 1) * t] = (sr * cbi + si * cbr).astype(BF16)
    for co in range(SSM_GROUP_CH):
        for d in range(N_DIR):
            ccr = crt_ref[d, 0, :, co:co + 1]
            cci = cit_ref[d, 0, :, co:co + 1]
            fr_, fi_ = pf[d]
            f_out[0, d * p:(d + 1) * p, co * t:(co + 1) * t] = (ccr * fr_ - cci * fi_).astype(BF16)
            f_out[0, (N_DIR + d) * p:(N_DIR + d + 1) * p, co * t:(co + 1) * t] = (-(ccr * fi_ + cci * fr_)).astype(BF16)


def _ssm_prep(p):
    t = SSM_T
    g = SSM_GROUPS
    ns = N_DIR * 2 * SSM_STATE
    spec4 = lambda a, b: pl.BlockSpec((N_DIR, 1, a, b), lambda i: (0, i, 0, 0))
    return pl.pallas_call(
        _ssm_prep_kernel,
        grid=(g,),
        in_specs=[spec4(SSM_STATE, 3), spec4(3, SSM_STATE),
                  spec4(SSM_STATE, SSM_GROUP_CH), spec4(SSM_STATE, SSM_GROUP_CH),
                  spec4(SSM_GROUP_CH, SSM_STATE), spec4(SSM_GROUP_CH, SSM_STATE),
                  spec4(SSM_STATE, SSM_GROUP_CH), spec4(SSM_STATE, SSM_GROUP_CH)],
        out_specs=[pl.BlockSpec((1, SSM_GROUP_CH * SSM_GROUP_CH, 2 * t), lambda i: (i, 0, 0)),
                   pl.BlockSpec((1, ns, SSM_GROUP_CH * t), lambda i: (i, 0, 0)),
                   pl.BlockSpec((1, ns, SSM_GROUP_CH * t), lambda i: (i, 0, 0)),
                   pl.BlockSpec((1, 1, ns), lambda i: (i, 0, 0))],
        out_shape=[jax.ShapeDtypeStruct((g, SSM_GROUP_CH * SSM_GROUP_CH, 2 * t), F32),
                   jax.ShapeDtypeStruct((g, ns, SSM_GROUP_CH * t), BF16),
                   jax.ShapeDtypeStruct((g, ns, SSM_GROUP_CH * t), BF16),
                   jax.ShapeDtypeStruct((g, 1, ns), F32)],
        compiler_params=_params("arbitrary"),
        name="ssm_prep",
    )(p["ssm_pc"], p["ssm_pr"], p["ssm_b_re"], p["ssm_b_im"], p["ssm_c_re"], p["ssm_c_im"],
      p["ssm_c_re_t"], p["ssm_c_im_t"])


def _ssm_kernel(*refs, dims):
    ng = len(dims)
    u_refs = refs[:ng]
    kv_ref, rs_ref, f_ref, at_ref = refs[ng:ng + 4]
    h0_refs = refs[ng + 4:2 * ng + 4]
    d_ref = refs[2 * ng + 4]
    y_outs = refs[2 * ng + 5:3 * ng + 5]
    hfin_outs = refs[3 * ng + 5:4 * ng + 5]
    m_scr = refs[4 * ng + 5]
    scratch = refs[4 * ng + 6:]
    t = SSM_T
    nch = SSM_GROUP_CH
    g = pl.program_id(0)
    half = N_DIR * SSM_STATE

    def gen(ci, carry):
        r0 = pl.multiple_of(ci * t, t)
        for co in range(nch):
            row = kv_ref[0, pl.ds(ci * nch + co, 1), :]
            rolled = pltpu.roll(jnp.broadcast_to(row, (t, 2 * t)), t, 1, stride=1, stride_axis=0)
            m_scr[pl.ds(r0, t), co * t:(co + 1) * t] = rolled[:, :t].astype(BF16)
        return carry
    lax.fori_loop(0, nch, gen, 0)

    a_re = at_ref[0, :, :half]
    a_im = at_ref[0, :, half:]
    is_fwd = lax.broadcasted_iota(jnp.int32, (1, half), 1) < SSM_STATE
    for gi, (bg, nc) in enumerate(dims):
        u_ref, h0_ref, y_out, hfin_out = u_refs[gi], h0_refs[gi], y_outs[gi], hfin_outs[gi]
        s_scr, hpf_scr, hpb_scr = scratch[3 * gi:3 * gi + 3]
        u_flat = jnp.concatenate([u_ref[0, ci].astype(BF16) for ci in range(nch)], axis=1)

        s = lax.dot_general(u_flat, rs_ref[0], _NT, preferred_element_type=F32)
        s_scr[0] = s[:, :half]
        s_scr[1] = s[:, half:]

        def advance(re, im, rows, s_scr=s_scr):
            return (a_re * re - a_im * im + s_scr[0, rows, :], a_re * im + a_im * re + s_scr[1, rows, :])

        def scan(i, carry, bg=bg, nc=nc, hpf_scr=hpf_scr, hpb_scr=hpb_scr, advance=advance):
            f_re, f_im, b_re, b_im = carry
            rows_f = pl.ds(i, bg, stride=nc)
            rows_b = pl.ds(nc - 1 - i, bg, stride=nc)
            hpf_scr[0, rows_f, :] = f_re
            hpf_scr[1, rows_f, :] = f_im
            hpb_scr[0, rows_b, :] = b_re
            hpb_scr[1, rows_b, :] = b_im
            return advance(f_re, f_im, rows_f) + advance(b_re, b_im, rows_b)
        h0_re = h0_ref[0, :, :half]
        h0_im = h0_ref[0, :, half:]
        f_re, f_im, b_re, b_im = lax.fori_loop(0, nc, scan, (h0_re, h0_im, h0_re, h0_im))
        hfin_out[0, :, :half] = jnp.where(is_fwd, f_re, b_re)
        hfin_out[0, :, half:] = jnp.where(is_fwd, f_im, b_im)
        hprev = jnp.concatenate([jnp.where(is_fwd, hpf_scr[0], hpb_scr[0]),
                                 jnp.where(is_fwd, hpf_scr[1], hpb_scr[1])], axis=1)

        y = (jnp.dot(hprev.astype(BF16), f_ref[0], preferred_element_type=F32)
             + jnp.dot(u_flat, m_scr[...], preferred_element_type=F32))
        for co in range(nch):
            y_out[0, co] = y[:, co * t:(co + 1) * t] + d_ref[g * nch + co] * u_ref[0, co].astype(F32)


def _ssm_mix(uts, tables, h0s, d_skip, shapes):
    t = SSM_T
    g = SSM_GROUPS
    nch = SSM_GROUP_CH
    ns = N_DIR * 2 * SSM_STATE
    kv, rs, f, at = tables
    dims = tuple((bg, lg // t) for bg, lg in shapes)
    u_specs = [pl.BlockSpec((1, nch, bg * nc, t), lambda i: (i, 0, 0, 0)) for bg, nc in dims]
    h_specs = [pl.BlockSpec((1, bg, ns), lambda i: (i, 0, 0)) for bg, _ in dims]
    scratch = [pltpu.VMEM((nch * t, nch * t), BF16)]
    for bg, nc in dims:
        scratch += [pltpu.VMEM((2, bg * nc, ns // 2), F32)] * 3
    outs = pl.pallas_call(
        functools.partial(_ssm_kernel, dims=dims),
        grid=(g,),
        in_specs=u_specs + [pl.BlockSpec((1, nch * nch, 2 * t), lambda i: (i, 0, 0)),
                            pl.BlockSpec((1, ns, nch * t), lambda i: (i, 0, 0)),
                            pl.BlockSpec((1, ns, nch * t), lambda i: (i, 0, 0)),
                            pl.BlockSpec((1, 1, ns), lambda i: (i, 0, 0))]
                 + h_specs + [pl.BlockSpec(memory_space=pltpu.SMEM)],
        out_specs=u_specs + h_specs,
        out_shape=[jax.ShapeDtypeStruct((g, nch, bg * nc, t), F32) for bg, nc in dims]
                  + [jax.ShapeDtypeStruct((g, bg, ns), F32) for bg, _ in dims],
        scratch_shapes=scratch,
        compiler_params=_params("arbitrary"),
        name="ssm_mix",
    )(*[ut.reshape(g, nch, bg * nc, t) for ut, (bg, nc) in zip(uts, dims)], kv, rs, f, at, *h0s, d_skip)
    n = len(dims)
    return ([y.reshape(g * nch, bg * nc * t) for y, (bg, nc) in zip(outs[:n], dims)], list(outs[n:]))


def _head_blocks(x, head_masks):
    xb = x.astype(BF16)
    return jnp.concatenate([xb * m for m in head_masks], axis=0)


def _hmm(a, b_blocks):
    return jnp.dot(a.astype(BF16), b_blocks, preferred_element_type=F32)


def _tri_inverse_heads(a_list, same16, same32, eye, head_masks):
    n = a_list[0].shape[0]
    pw = [jnp.where(same16, -a, 0.0) for a in a_list]
    x = [eye + p for p in pw]
    pw = [_hmm(p, _head_blocks(p, head_masks)) for p in pw]
    yield
    for _ in range(2):
        both = [_hmm(jnp.concatenate([xi, p], axis=0), _head_blocks(p, head_masks)) for xi, p in zip(x, pw)]
        x = [xi + b[:n] for xi, b in zip(x, both)]
        pw = [b[n:] for b in both]
        yield
    x = [xi + _hmm(xi, _head_blocks(p, head_masks)) for xi, p in zip(x, pw)]
    yield
    for mask in (jnp.logical_and(same32, jnp.logical_not(same16)), jnp.logical_not(same32)):
        t = [_hmm(jnp.where(mask, a, 0.0), _head_blocks(xi, head_masks)) for a, xi in zip(a_list, x)]
        yield
        x = [xi - _hmm(xi, _head_blocks(ti, head_masks)) for xi, ti in zip(x, t)]
        yield
    return x


def _gdn_kernel(xf_ref, xb_ref, abf_ref, abb_ref, abtf_ref, abtb_ref, prow_ref, pcol_ref, ex_ref, s0_ref,
                of_ref, ob_ref, sfin_ref, s_scr, *, nt, tt):
    n = pl.program_id(1)
    ck = GDN_CHUNK
    cpt = tt // ck
    hw = GDN_HEADS * ck

    @pl.when(n == 0)
    def _():
        s_scr[...] = s0_ref[0]

    r = lax.broadcasted_iota(jnp.int32, (ck, hw), 0)
    lane = lax.broadcasted_iota(jnp.int32, (ck, hw), 1)
    c = lane % ck
    same16 = (r // 16) == (c // 16)
    same32 = (r // 32) == (c // 32)
    eye = (r == c).astype(F32)
    head_sel = [(lane // ck) == h for h in range(GDN_HEADS)]
    lane1 = lax.broadcasted_iota(jnp.int32, (1, hw), 1)
    head_masks = [((lane1 // ck) == h).astype(BF16) for h in range(GDN_HEADS)]
    row_in = lax.broadcasted_iota(jnp.int32, (tt, LANES), 0) % ck
    lane_in = lax.broadcasted_iota(jnp.int32, (GATE_W, tt), 1) % ck
    is_decay_lane = lax.broadcasted_iota(jnp.int32, (1, LANES), 1) < GATE_W // 2

    steps = [[] for _ in range(cpt)]
    for d, (x_ref, ab_ref, abt_ref, o_ref) in enumerate(((xf_ref, abf_ref, abtf_ref, of_ref),
                                                          (xb_ref, abb_ref, abtb_ref, ob_ref))):
        rev = d == 1
        ab = ab_ref[0]
        g_col = -jnp.exp(prow_ref[0:1, :]) * _softplus(ab + prow_ref[1:2, :])
        abt = abt_ref[0]
        g_row = -jnp.exp(pcol_ref[:, 0:1]) * _softplus(abt + pcol_ref[:, 1:2])
        sh = 1
        while sh < ck:
            if rev:
                g_col = g_col + jnp.where(row_in < ck - sh, pltpu.roll(g_col, tt - sh, 0), 0.0)
                g_row = g_row + jnp.where(lane_in < ck - sh, pltpu.roll(g_row, tt - sh, 1), 0.0)
            else:
                g_col = g_col + jnp.where(row_in >= sh, pltpu.roll(g_col, sh, 0), 0.0)
                g_row = g_row + jnp.where(lane_in >= sh, pltpu.roll(g_row, sh, 1), 0.0)
            sh *= 2
        spread = _mm_exact_rhs(jnp.where(is_decay_lane, g_col, jax.nn.sigmoid(ab)), ex_ref[d])
        incl = (r <= c) if rev else (r >= c)
        strict = (r < c) if rev else (r > c)
        last = 0 if rev else ck - 1
        for i, cc in enumerate(range(cpt - 1, -1, -1) if rev else range(cpt)):
            c0 = cc * ck
            gc = spread[c0:c0 + ck, :hw]
            gr = jnp.concatenate([g_row[d * GDN_HEADS + h:d * GDN_HEADS + h + 1, c0:c0 + ck]
                                  for h in range(GDN_HEADS)], axis=1)
            steps[i].append(dict(
                d=d, o_ref=o_ref, rows=slice(c0, c0 + ck), strict=strict,
                q=x_ref[0, c0:c0 + ck, :GDN_K_W], k=x_ref[0, c0:c0 + ck, GDN_K_W:2 * GDN_K_W],
                v=x_ref[0, c0:c0 + ck, 2 * GDN_K_W:],
                gc=gc, beta=spread[c0:c0 + ck, hw:], g_last=gc[last:last + 1, :],
                decay=jnp.exp(jnp.where(incl, gc - gr, -jnp.inf))))
    def chunk_local(items):
        for it in items:
            k_blocks = _head_blocks(it["k"], head_masks)
            kq = lax.dot_general(jnp.concatenate([it["k"], it["q"]], axis=0).astype(BF16), k_blocks, _NT,
                                 preferred_element_type=F32)
            it["kk"] = kq[:ck]
            it["attn"] = kq[ck:] * it["decay"]
        yield
        t_inv = yield from _tri_inverse_heads(
            [jnp.where(it["strict"], it["kk"] * it["decay"] * it["beta"], 0.0) for it in items],
            same16, same32, eye, head_masks)
        for it, ti in zip(items, t_inv):
            egc = jnp.exp(it["gc"])
            it["u"] = _hmm(ti, _head_blocks(it["v"] * it["beta"], head_masks))
            w = _hmm(ti, _head_blocks(it["k"] * (it["beta"] * egc), head_masks))
            it["w_qe"] = jnp.concatenate([w, it["q"] * egc], axis=0).astype(BF16)
            it["kd"] = (it["k"] * jnp.exp(it["g_last"] - it["gc"])).astype(BF16)
        yield

    state = [s_scr[d] for d in range(N_DIR)]

    def recurrence(some_steps):
        for st in some_steps:
            from_state = [_hmm(it["w_qe"], _head_blocks(state[it["d"]], head_masks)) for it in st]
            v_new = [it["u"] - fs[:ck] for it, fs in zip(st, from_state)]
            yield
            for it, fs, vn in zip(st, from_state, v_new):
                it["o_ref"][0, it["rows"], :] = fs[ck:] + _hmm(it["attn"], _head_blocks(vn, head_masks))
            for it, vn in zip(st, v_new):
                cross = lax.dot_general(it["kd"], vn.astype(BF16), _TN, preferred_element_type=F32)
                upd = functools.reduce(lambda a, b: a + b,
                                       [jnp.where(head_sel[h], cross[h * ck:(h + 1) * ck, :], 0.0)
                                        for h in range(GDN_HEADS)])
                state[it["d"]] = state[it["d"]] * jnp.exp(it["g_last"]) + upd
            yield

    per = GDN_GROUP_CHUNKS
    groups = [steps[i:i + per] for i in range(0, cpt, per)]
    for gi in range(len(groups) + 1):
        running = []
        if gi < len(groups):
            running.append(chunk_local([it for st in groups[gi] for it in st]))
        if gi >= 1:
            running.append(recurrence(groups[gi - 1]))
        while running:
            for gen in list(running):
                if next(gen, "done") == "done":
                    running.remove(gen)
    for d in range(N_DIR):
        s_scr[d] = state[d]

    @pl.when(n == nt - 1)
    def _():
        sfin_ref[0] = s_scr[...]


def _gdn_mix(qkv, gab, gabt, prow, pcol, s0, *, tt):
    bg, lg, w = qkv.shape
    nt = lg // tt
    fwd = lambda b, n: (b, n, 0)
    bwd = lambda b, n: (b, nt - 1 - n, 0)
    gate_col = jnp.arange(LANES)[None, :, None]
    want = (jnp.arange(N_DIR)[:, None, None] * GDN_HEADS + jnp.arange(GDN_V_W)[None, None, :] // GDN_DV)
    spread = jnp.concatenate([gate_col == want, gate_col == want + GATE_W // 2], axis=2).astype(BF16)
    st_spec = pl.BlockSpec((1, N_DIR, GDN_DK, GDN_V_W), lambda b, n: (b, 0, 0, 0))
    o_f, o_b, s_fin = pl.pallas_call(
        functools.partial(_gdn_kernel, nt=nt, tt=tt),
        grid=(bg, nt),
        in_specs=[pl.BlockSpec((1, tt, w), fwd), pl.BlockSpec((1, tt, w), bwd),
                  pl.BlockSpec((1, tt, LANES), fwd), pl.BlockSpec((1, tt, LANES), bwd),
                  pl.BlockSpec((1, GATE_W, tt), lambda b, n: (b, 0, n)),
                  pl.BlockSpec((1, GATE_W, tt), lambda b, n: (b, 0, nt - 1 - n)),
                  pl.BlockSpec((2, LANES), lambda b, n: (0, 0)),
                  pl.BlockSpec((GATE_W, 2), lambda b, n: (0, 0)),
                  pl.BlockSpec((N_DIR, LANES, 2 * GDN_V_W), lambda b, n: (0, 0, 0)),
                  st_spec],
        out_specs=[pl.BlockSpec((1, tt, GDN_V_W), fwd), pl.BlockSpec((1, tt, GDN_V_W), bwd), st_spec],
        out_shape=[jax.ShapeDtypeStruct((bg, lg, GDN_V_W), F32), jax.ShapeDtypeStruct((bg, lg, GDN_V_W), F32),
                   jax.ShapeDtypeStruct((bg, N_DIR, GDN_DK, GDN_V_W), F32)],
        scratch_shapes=[pltpu.VMEM((N_DIR, GDN_DK, GDN_V_W), F32)],
        compiler_params=_params("arbitrary", "arbitrary"),
        name="gdn_mix",
    )(qkv, qkv, gab, gab, gabt, gabt, prow, pcol, spread, s0)
    return o_f, o_b, s_fin


def _gelu_tanh(x):
    return 0.5 * x * (1.0 + jnp.tanh(math.sqrt(2.0 / math.pi) * (x + 0.044715 * (x * x * x))))


def _out_kernel(x_ref, ga_ref, shm_ref, scm_ref, gm_ref, attn_ref, yt_ref, of_ref, ob_ref, gz_ref,
                g2_ref, gng_ref, bd_ref, wglut_ref, bglu_ref, wo_ref, w1_ref, w2_ref, o_ref):
    x = x_ref[0]
    z = _gelu_tanh(yt_ref[...])
    gate = jax.nn.sigmoid(jnp.dot(wglut_ref[...], z.astype(BF16), preferred_element_type=F32) + bglu_ref[...])
    ssm_t = (z * gate).astype(BF16)
    o = of_ref[0] + ob_ref[0]
    gdn = o * lax.rsqrt(_seg_mean_sq(o, bd_ref) + EPS) * gng_ref[...] * _silu(gz_ref[0])
    mixed = (jnp.dot(attn_ref[0], wo_ref[:ATT_Q_W, :], preferred_element_type=F32)
             + lax.dot_general(ssm_t, wo_ref[ATT_Q_W:ATT_Q_W + SSM_WIDTH, :], _TN, preferred_element_type=F32)
             + jnp.dot(gdn.astype(BF16), wo_ref[ATT_Q_W + SSM_WIDTH:, :], preferred_element_type=F32))
    x1 = x + ga_ref[0] * mixed
    ms = jnp.mean(x1 * x1, axis=-1, keepdims=True)
    h2 = ((x1 * lax.rsqrt(ms + EPS) * g2_ref[...]) * (1.0 + scm_ref[0]) + shm_ref[0]).astype(BF16)
    acc = None
    fc = D_MODEL
    for j in range(D_FF // fc):
        a = jnp.maximum(jnp.dot(h2, w1_ref[:, j * fc:(j + 1) * fc], preferred_element_type=F32), 0.0)
        part = jnp.dot((a * a).astype(BF16), w2_ref[j * fc:(j + 1) * fc, :], preferred_element_type=F32)
        acc = part if acc is None else acc + part
    o_ref[0] = x1 + gm_ref[0] * acc


def _out_projection(x, mod3, mod_row, attn, yt, o_f, o_b, gz, p, consts, *, tm):
    bg, lg, _ = x.shape
    nt = lg // tm
    row = mod_row
    full = lambda shape: pl.BlockSpec(shape, lambda t, b: (0,) * len(shape), pipeline_mode=pl.Buffered(1))
    tok = lambda w: pl.BlockSpec((1, tm, w), lambda t, b: (b, t, 0))
    modc = lambda j: pl.BlockSpec((1, 1, D_MODEL), lambda t, b: (row(b), 0, j))
    mix_w = ATT_Q_W + SSM_WIDTH + GDN_V_W
    return pl.pallas_call(
        _out_kernel,
        grid=(nt, bg),
        in_specs=[tok(D_MODEL), modc(2), modc(3), modc(4), modc(5),
                  tok(ATT_Q_W), pl.BlockSpec((SSM_WIDTH, tm), lambda t, b: (0, b * nt + t)),
                  tok(GDN_V_W), tok(GDN_V_W), tok(GDN_V_W),
                  full((1, D_MODEL)), full((1, GDN_V_W)), full((GDN_V_W, GDN_V_W)),
                  full((SSM_WIDTH, SSM_WIDTH)), full((SSM_WIDTH, 1)),
                  full((mix_w, D_MODEL)), full((D_MODEL, D_FF)), full((D_FF, D_MODEL))],
        out_specs=tok(D_MODEL),
        out_shape=jax.ShapeDtypeStruct((bg, lg, D_MODEL), F32),
        compiler_params=_params("arbitrary", "arbitrary"),
        name="out_projection",
    )(x, mod3, mod3, mod3, mod3, attn, yt, o_f, o_b, gz,
      p["norm2_g"], p["gdn_norm_g"], consts["bd_mean"][:GDN_V_W, :GDN_V_W],
      p["w_glu_t"], p["b_glu"], p["w_out"], p["w_ff1"], p["w_ff2"])


def _constants(max_len):
    n_freq = HEAD_DIM // 4
    rows = jnp.repeat(jnp.arange(max_len // GRID_W, dtype=F32), GRID_W)
    cols = jnp.tile(jnp.arange(GRID_W, dtype=F32), max_len // GRID_W)
    inv_freq = jnp.power(ROPE_BASE, -jnp.arange(n_freq, dtype=F32) / n_freq)
    ang = jnp.concatenate([rows[:, None] * inv_freq, cols[:, None] * inv_freq], axis=-1)
    cos = jnp.repeat(jnp.cos(ang), 2, axis=-1)
    sin = jnp.repeat(jnp.sin(ang), 2, axis=-1) * jnp.tile(jnp.array([-1.0, 1.0], F32), HEAD_DIM // 2)
    seg = jnp.arange(ATT_Q_W) // HEAD_DIM
    same = seg[:, None] == seg[None, :]
    return {
        "cos": jnp.tile(cos, (1, LANES // HEAD_DIM)),
        "sin": jnp.tile(sin, (1, LANES // HEAD_DIM)),
        "bd_mean": jnp.where(same, 1.0 / HEAD_DIM, 0.0).astype(BF16),
    }


def _layer_params(w):
    w_in = w["w_in"]
    off = [0]
    for size in (ATT_Q_W, ATT_KV_W, ATT_KV_W, SSM_WIDTH, GDN_K_W, GDN_K_W, GDN_V_W, GDN_V_W, GATE_W):
        off.append(off[-1] + size)
    wab = w_in[:, off[8]:off[9]]
    col = lambda a: a.reshape(-1, 1)
    pad_lanes = lambda a: jnp.pad(a, ((0, 0), (0, LANES - a.shape[1])))
    alog = w["gdn_a_log"].reshape(1, -1)
    dtb = w["gdn_dt_bias"].reshape(1, -1)
    zeros = jnp.zeros_like(alog)
    ls = jnp.broadcast_to(w["ssm_log_step"][..., None], (N_DIR, SSM_GROUPS, SSM_STATE))
    pair_order = jnp.arange(N_HEADS).reshape(N_KV_HEADS, GQA).T.reshape(-1)
    head_cols = (pair_order[:, None] * HEAD_DIM + jnp.arange(HEAD_DIM)[None, :]).reshape(-1)
    w_out = w["w_out"]
    w_out = jnp.concatenate([w_out[:ATT_Q_W][head_cols], w_out[ATT_Q_W:]], axis=0)
    return {
        "norm1_g": w["norm1_g"].reshape(1, -1), "norm2_g": w["norm2_g"].reshape(1, -1),
        "wq": w_in[:, off[0]:off[1]][:, head_cols].astype(BF16), "wkv": w_in[:, off[1]:off[3]].astype(BF16),
        "wut": w_in[:, off[3]:off[4]].T.astype(BF16), "wg": w_in[:, off[4]:off[7]].astype(BF16),
        "wz": w_in[:, off[7]:off[8]].astype(BF16),
        "wab": pad_lanes(wab).astype(BF16), "wabt": wab.T.astype(BF16),
        "q_norm_g": jnp.tile(w["q_norm_g"], N_HEADS).reshape(1, -1),
        "k_norm_g": jnp.tile(w["k_norm_g"], N_KV_HEADS).reshape(1, -1),
        "attn_sink": w["attn_sink"],
        "ssm_pc": jnp.stack([w["ssm_lam_re"], w["ssm_lam_im"], ls], axis=-1),
        "ssm_pr": jnp.stack([w["ssm_lam_re"], w["ssm_lam_im"], ls], axis=-2),
        "ssm_b_re": w["ssm_b_re"], "ssm_b_im": w["ssm_b_im"],
        "ssm_c_re": w["ssm_c_re"], "ssm_c_im": w["ssm_c_im"],
        "ssm_c_re_t": jnp.swapaxes(w["ssm_c_re"], -1, -2), "ssm_c_im_t": jnp.swapaxes(w["ssm_c_im"], -1, -2),
        "ssm_d": w["ssm_d"],
        "w_glu_t": w["ssm_w_glu"].T.astype(BF16), "b_glu": col(w["ssm_b_glu"]),
        "gdn_conv_w": w["gdn_conv_w"],
        "gdn_prow": pad_lanes(jnp.concatenate([jnp.concatenate([alog, zeros], 1), jnp.concatenate([dtb, zeros], 1)], 0)),
        "gdn_pcol": jnp.concatenate([jnp.concatenate([alog, zeros], 1), jnp.concatenate([dtb, zeros], 1)], 0).T,
        "gdn_norm_g": jnp.tile(w["gdn_norm_g"], GDN_HEADS).reshape(1, -1),
        "w_out": w_out.astype(BF16), "w_ff1": w["w_ff1"].astype(BF16), "w_ff2": w["w_ff2"].astype(BF16),
    }


def _ssm_states_to_lanes(h0):
    bg, depth = h0.shape[:2]
    return h0.transpose(1, 4, 0, 3, 2, 5).reshape(depth, SSM_GROUPS, bg, 2 * N_DIR * SSM_STATE)


def _gdn_states_to_lanes(s0):
    bg, depth = s0.shape[:2]
    return s0.transpose(1, 0, 2, 4, 3, 5).reshape(depth, bg, N_DIR, GDN_DK, GDN_V_W)


def _ssm_lanes_to_states(h):
    depth, _, bg, _ = h.shape
    return h.reshape(depth, SSM_GROUPS, bg, 2, N_DIR, SSM_STATE).transpose(2, 0, 4, 3, 1, 5)


def _gdn_lanes_to_states(s):
    depth, bg = s.shape[:2]
    return s.reshape(depth, bg, N_DIR, GDN_DK, GDN_HEADS, GDN_DV).transpose(1, 0, 2, 4, 3, 5)


def _layer(groups, mod3, p, consts, layer):
    tables = _ssm_prep(p)
    proj = []
    for gr in groups:
        lg = gr["x"].shape[1]
        proj.append(_in_projection(gr["x"], mod3, gr["mod_row"], p, consts, rope=gr["ctx_kv"] is not None,
                                   tm=min(lg, 512)))
    shapes = [gr["x"].shape[:2] for gr in groups]
    yts, h_fins = _ssm_mix([pr[3] for pr in proj], tables, [gr["ssm_h0"] for gr in groups], p["ssm_d"], shapes)
    xs, aux = [], []
    for gr, pr, yt, h_fin in zip(groups, proj, yts, h_fins):
        bg, lg, _ = gr["x"].shape
        q, k, v, _, qkv, gz, gab, gabt = pr
        if gr["ctx_kv"] is not None:
            attn = _latent_attention(q, k, v, gr["ctx_kv"][0], gr["ctx_kv"][1], layer, p["attn_sink"])
        else:
            attn = _context_attention(q, k, v, p["attn_sink"])
        o_f, o_b, s_fin = _gdn_mix(qkv, gab, gabt, p["gdn_prow"], p["gdn_pcol"], gr["gdn_s0"], tt=min(lg, GDN_TILE))
        if gr["ctx_kv"] is None:
            flat = lambda a: a.reshape(1, bg * lg, a.shape[-1])
            x_new = _out_projection(flat(gr["x"]), mod3, gr["mod_row"], flat(attn), yt, flat(o_f), flat(o_b),
                                    flat(gz), p, consts, tm=min(bg * lg, 512)).reshape(bg, lg, D_MODEL)
        else:
            x_new = _out_projection(gr["x"], mod3, gr["mod_row"], attn, yt, o_f, o_b, gz, p, consts, tm=min(lg, 512))
        xs.append(x_new)
        aux.append((k, v, h_fin, s_fin))
    return xs, aux


def kernel(x_prompt, x_sample, c, cache_k, cache_v, state_ssm, state_gdn, c_ctx, norm1_g, norm2_g, w_mod, b_mod, w_in, q_norm_g, k_norm_g, attn_sink, ssm_lam_re, ssm_lam_im, ssm_log_step, ssm_b_re, ssm_b_im, ssm_c_re, ssm_c_im, ssm_d, ssm_w_glu, ssm_b_glu, gdn_conv_w, gdn_a_log, gdn_dt_bias, gdn_norm_g, w_out, w_ff1, w_ff2):
    w = dict(norm1_g=norm1_g, norm2_g=norm2_g, w_in=w_in, q_norm_g=q_norm_g, k_norm_g=k_norm_g, attn_sink=attn_sink,
             ssm_lam_re=ssm_lam_re, ssm_lam_im=ssm_lam_im, ssm_log_step=ssm_log_step, ssm_b_re=ssm_b_re,
             ssm_b_im=ssm_b_im, ssm_c_re=ssm_c_re, ssm_c_im=ssm_c_im, ssm_d=ssm_d, ssm_w_glu=ssm_w_glu,
             ssm_b_glu=ssm_b_glu, gdn_conv_w=gdn_conv_w, gdn_a_log=gdn_a_log, gdn_dt_bias=gdn_dt_bias,
             gdn_norm_g=gdn_norm_g, w_out=w_out, w_ff1=w_ff1, w_ff2=w_ff2)
    n_ctx, seq, _ = x_prompt.shape
    n_dec, dec_seq, _ = x_sample.shape
    depth = w_in.shape[0]
    past = cache_k.shape[2]

    n_rows = -(-(n_dec + 1) // SUBLANES) * SUBLANES
    cond = jnp.zeros((n_rows, D_MODEL), F32).at[:n_dec].set(c).at[n_dec].set(c_ctx)
    mod = _modulation(cond, w_mod, b_mod)

    consts = _constants(max(seq, dec_seq))
    cache_k4 = cache_k.reshape(n_dec, depth, past, ATT_KV_W)
    cache_v4 = cache_v.reshape(n_dec, depth, past, ATT_KV_W)
    ssm_zero = jnp.zeros((SSM_GROUPS, n_ctx, 2 * N_DIR * SSM_STATE), F32)
    gdn_zero = jnp.zeros((n_ctx, N_DIR, GDN_DK, GDN_V_W), F32)
    ssm_h0 = _ssm_states_to_lanes(state_ssm)
    gdn_s0 = _gdn_states_to_lanes(state_gdn)

    params = jax.vmap(_layer_params)(w)
    xp, xs = x_prompt, x_sample
    ks, vs, ss, gs = [], [], [], []
    for l in range(depth):
        p = {name: value[l] for name, value in params.items()}
        mod3 = mod[l].reshape(n_rows, 1, N_MOD * D_MODEL)
        groups = [dict(x=xp, mod_row=lambda b: n_dec, ctx_kv=None, ssm_h0=ssm_zero, gdn_s0=gdn_zero),
                  dict(x=xs, mod_row=lambda b: b, ctx_kv=(cache_k4, cache_v4), ssm_h0=ssm_h0[l], gdn_s0=gdn_s0[l])]
        (xp, xs), ((k_l, v_l, s_l, g_l), _) = _layer(groups, mod3, p, consts, l)
        ks.append(k_l.reshape(n_ctx, seq, N_KV_HEADS, HEAD_DIM))
        vs.append(v_l.reshape(n_ctx, seq, N_KV_HEADS, HEAD_DIM))
        ss.append(s_l)
        gs.append(g_l)
    return (xp, xs, jnp.stack(ks, axis=1), jnp.stack(vs, axis=1),
            _ssm_lanes_to_states(jnp.stack(ss)), _gdn_lanes_to_states(jnp.stack(gs)))
```

```python
import functools
import math

import jax
import jax.numpy as jnp
from jax import lax
from jax.experimental import pallas as pl
from jax.experimental.pallas import tpu as pltpu

F32 = jnp.float32
BF16 = jnp.bfloat16
EPS = 1e-6
NEG_INF = -1e30
LOG2_E = math.log2(math.e)

D_MODEL = 1024
DEPTH = 4
GRID_W = 64
N_DIR = 2
N_HEADS = 8
N_KV_HEADS = 2
GQA = N_HEADS // N_KV_HEADS
HEAD_DIM = 64
ATT_BLOCK = 128
ROPE_BASE = 10000.0
SSM_GROUP_CH = 16
SSM_GROUPS = 16
SSM_WIDTH = SSM_GROUPS * SSM_GROUP_CH
SSM_STATE = 64
GDN_HEADS = 4
GDN_DK = 64
GDN_DV = 64
GDN_K_W = GDN_HEADS * GDN_DK
GDN_V_W = GDN_HEADS * GDN_DV
GDN_QKV_W = 2 * GDN_K_W + GDN_V_W
GDN_CHUNK = 64
ATT_Q_W = N_HEADS * HEAD_DIM
ATT_KV_W = N_KV_HEADS * HEAD_DIM
D_FF = 4 * D_MODEL
N_MOD = 6
GATE_W = 2 * N_DIR * GDN_HEADS

LANES = 128
SUBLANES = 8
SSM_T = LANES
ATT_STEP_BLOCKS = 8
GDN_TILE = 1024
GDN_GROUP_CHUNKS = 4
VMEM_LIMIT = 56 * 1024 * 1024

_NT = (((1,), (1,)), ((), ()))
_TN = (((0,), (0,)), ((), ()))


def _mm(a, b):
    return jnp.dot(a.astype(BF16), b.astype(BF16), preferred_element_type=F32)


def _mm_nt(a, b):
    return lax.dot_general(a.astype(BF16), b.astype(BF16), _NT, preferred_element_type=F32)


def _mm_tn(a, b):
    return lax.dot_general(a.astype(BF16), b.astype(BF16), _TN, preferred_element_type=F32)


def _split3(a):
    hi = a.astype(BF16)
    r1 = a - hi.astype(F32)
    mid = r1.astype(BF16)
    lo = (r1 - mid.astype(F32)).astype(BF16)
    return hi, mid, lo


def _mm_exact_rhs(a, b_bf16):
    hi, mid, lo = _split3(a)
    return (jnp.dot(hi, b_bf16, preferred_element_type=F32)
            + jnp.dot(mid, b_bf16, preferred_element_type=F32)
            + jnp.dot(lo, b_bf16, preferred_element_type=F32))


def _mm_3pass(a, b):
    ah, am, _ = _split3(a)
    bh, bm, _ = _split3(b)
    d = lambda x, y: jnp.dot(x, y, preferred_element_type=F32)
    return d(ah, bh) + (d(ah, bm) + d(am, bh))


def _silu(x):
    return x * jax.nn.sigmoid(x)


def _softplus(x):
    return jnp.maximum(x, 0.0) + jnp.log1p(jnp.exp(-jnp.abs(x)))


def _params(*sem):
    return pltpu.CompilerParams(dimension_semantics=sem, vmem_limit_bytes=VMEM_LIMIT)


def _mod_kernel(cond_ref, w_ref, b_ref, o_ref):
    c = cond_ref[...]
    o_ref[0] = _mm(_silu(c), w_ref[0]) + b_ref[0]


def _modulation(cond, w_mod, b_mod):
    rows = cond.shape[0]
    cn = 1536
    return pl.pallas_call(
        _mod_kernel,
        grid=(DEPTH, N_MOD * D_MODEL // cn),
        in_specs=[pl.BlockSpec((rows, D_MODEL), lambda l, j: (0, 0)),
                  pl.BlockSpec((1, D_MODEL, cn), lambda l, j: (l, 0, j)),
                  pl.BlockSpec((1, 1, cn), lambda l, j: (l, 0, j))],
        out_specs=pl.BlockSpec((1, rows, cn), lambda l, j: (l, 0, j)),
        out_shape=jax.ShapeDtypeStruct((DEPTH, rows, N_MOD * D_MODEL), F32),
        compiler_params=_params("arbitrary", "arbitrary"),
        name="modulation",
    )(cond, w_mod, b_mod.reshape(DEPTH, 1, N_MOD * D_MODEL))


def _seg_mean_sq(x, bd_ref):
    n = x.shape[-1]
    return jnp.dot((x * x).astype(BF16), bd_ref[:n, :n], preferred_element_type=F32)


def _rope(x, cos, sin_signed):
    n = x.shape[-1]
    nxt = pltpu.roll(x, n - 1, 1)
    prv = pltpu.roll(x, 1, 1)
    lane = lax.broadcasted_iota(jnp.int32, x.shape, 1)
    swapped = jnp.where(lane % 2 == 0, nxt, prv)
    return x * cos + swapped * sin_signed


def _in_kernel(x_ref, xp_ref, xn_ref, sh_ref, sc_ref, g1_ref, wq_ref, wkv_ref, wut_ref, wg_ref, wz_ref, wab_ref,
               wabt_ref, qg_ref, kg_ref, bd_ref, cos_ref, sin_ref, cw_ref,
               q_out, k_out, v_out, ut_out, g_out, z_out, ab_out, abt_out, *, rope, nt):
    i = pl.program_id(0)
    tm = x_ref.shape[1]
    x = jnp.concatenate([x_ref[0], xp_ref[0], xn_ref[0]], axis=0)
    ms = jnp.mean(x * x, axis=-1, keepdims=True)
    h = (x * lax.rsqrt(ms + EPS) * g1_ref[...]) * (1.0 + sc_ref[0]) + sh_ref[0]
    hb_all = h.astype(BF16)
    hb = hb_all[:tm]

    zg = jnp.dot(hb_all, wg_ref[...], preferred_element_type=F32)
    z = zg[:tm]
    prev_row = jnp.where(i > 0, zg[tm + SUBLANES - 1:tm + SUBLANES], 0.0)
    next_row = jnp.where(i < nt - 1, zg[tm + SUBLANES:tm + SUBLANES + 1], 0.0)
    row = lax.broadcasted_iota(jnp.int32, z.shape, 0)
    z_m1 = jnp.where(row == 0, prev_row, pltpu.roll(z, 1, 0))
    z_p1 = jnp.where(row == tm - 1, next_row, pltpu.roll(z, tm - 1, 0))
    y = _silu(z_m1 * cw_ref[0:1, :] + z * cw_ref[1:2, :] + z_p1 * cw_ref[2:3, :])
    gq = y[:, :GDN_K_W]
    gk = y[:, GDN_K_W:2 * GDN_K_W]
    g_out[0, :, :GDN_K_W] = gq * lax.rsqrt(_seg_mean_sq(gq, bd_ref) * GDN_DK + EPS) * (GDN_DK ** -0.5)
    g_out[0, :, GDN_K_W:2 * GDN_K_W] = gk * lax.rsqrt(_seg_mean_sq(gk, bd_ref) * GDN_DK + EPS)
    g_out[0, :, 2 * GDN_K_W:] = y[:, 2 * GDN_K_W:]

    q = jnp.dot(hb, wq_ref[...], preferred_element_type=F32)
    q = q * lax.rsqrt(_seg_mean_sq(q, bd_ref) + EPS) * qg_ref[...]
    kv = jnp.dot(hb, wkv_ref[...], preferred_element_type=F32)
    k = kv[:, :ATT_KV_W]
    k = k * lax.rsqrt(_seg_mean_sq(k, bd_ref) + EPS) * kg_ref[...]
    if rope:
        cos = cos_ref[...]
        sin = sin_ref[...]
        q = _rope(q, jnp.concatenate([cos] * (ATT_Q_W // LANES), axis=1),
                  jnp.concatenate([sin] * (ATT_Q_W // LANES), axis=1))
        k = _rope(k, cos, sin)
    q_out[0] = (q * (HEAD_DIM ** -0.5 * LOG2_E)).astype(BF16)
    k_out[0] = k
    v_out[0] = kv[:, ATT_KV_W:]
    ut_out[...] = lax.dot_general(wut_ref[...], hb, _NT, preferred_element_type=F32).astype(BF16)
    z_out[0] = jnp.dot(hb, wz_ref[...], preferred_element_type=F32)
    ab_out[0] = jnp.dot(hb, wab_ref[...], preferred_element_type=F32)
    abt_out[0] = lax.dot_general(wabt_ref[...], hb, _NT, preferred_element_type=F32)


def _in_projection(x, mod3, mod_row, p, consts, *, rope, tm):
    bg, lg, _ = x.shape
    nt = lg // tm
    row = mod_row
    full = lambda shape: pl.BlockSpec(shape, lambda t, b: (0,) * len(shape))
    tok = lambda w: pl.BlockSpec((1, tm, w), lambda t, b: (b, t, 0))
    rb = tm // SUBLANES
    in_specs = [
        tok(D_MODEL),
        pl.BlockSpec((1, SUBLANES, D_MODEL), lambda t, b: (b, jnp.maximum(t * rb - 1, 0), 0)),
        pl.BlockSpec((1, SUBLANES, D_MODEL), lambda t, b: (b, jnp.minimum((t + 1) * rb, lg // SUBLANES - 1), 0)),
        pl.BlockSpec((1, 1, D_MODEL), lambda t, b: (row(b), 0, 0)),
        pl.BlockSpec((1, 1, D_MODEL), lambda t, b: (row(b), 0, 1)),
        full((1, D_MODEL)),
        full((D_MODEL, ATT_Q_W)), full((D_MODEL, 2 * ATT_KV_W)), full((SSM_WIDTH, D_MODEL)),
        full((D_MODEL, GDN_QKV_W)), full((D_MODEL, GDN_V_W)), full((D_MODEL, LANES)), full((GATE_W, D_MODEL)),
        full((1, ATT_Q_W)), full((1, ATT_KV_W)), full((ATT_Q_W, ATT_Q_W)),
        pl.BlockSpec((tm, LANES), lambda t, b: (t, 0)),
        pl.BlockSpec((tm, LANES), lambda t, b: (t, 0)),
        full((3, GDN_QKV_W)),
    ]
    out_specs = [
        tok(ATT_Q_W), tok(ATT_KV_W), tok(ATT_KV_W),
        pl.BlockSpec((SSM_WIDTH, tm), lambda t, b: (0, b * nt + t)),
        tok(GDN_QKV_W), tok(GDN_V_W), tok(LANES),
        pl.BlockSpec((1, GATE_W, tm), lambda t, b: (b, 0, t)),
    ]
    out_shape = [
        jax.ShapeDtypeStruct((bg, lg, ATT_Q_W), BF16),
        jax.ShapeDtypeStruct((bg, lg, ATT_KV_W), F32),
        jax.ShapeDtypeStruct((bg, lg, ATT_KV_W), F32),
        jax.ShapeDtypeStruct((SSM_WIDTH, bg * lg), BF16),
        jax.ShapeDtypeStruct((bg, lg, GDN_QKV_W), F32),
        jax.ShapeDtypeStruct((bg, lg, GDN_V_W), F32),
        jax.ShapeDtypeStruct((bg, lg, LANES), F32),
        jax.ShapeDtypeStruct((bg, GATE_W, lg), F32),
    ]
    return pl.pallas_call(
        functools.partial(_in_kernel, rope=rope, nt=nt),
        grid=(nt, bg), in_specs=in_specs, out_specs=out_specs, out_shape=out_shape,
        compiler_params=_params("arbitrary", "arbitrary"),
        name="in_projection",
    )(x, x, x, mod3, mod3, p["norm1_g"], p["wq"], p["wkv"], p["wut"], p["wg"], p["wz"], p["wab"], p["wabt"],
      p["q_norm_g"], p["k_norm_g"], consts["bd_mean"], consts["cos"][:lg], consts["sin"][:lg], p["gdn_conv_w"])


def _attend(q_ref, r0, k_all, v_all, masks, sink_ref, o_ref):
    nq = ATT_BLOCK
    lane = lax.broadcasted_iota(jnp.int32, (1, LANES), 1)
    low = lane < HEAD_DIM
    keep_low = low.astype(BF16)
    keep_high = 1 - keep_low
    tiles = [q_ref[0, r0:r0 + nq, i * LANES:(i + 1) * LANES] for i in range(GQA)]
    q_rows = jnp.concatenate([t * keep_low for t in tiles] + [t * keep_high for t in tiles], axis=0)
    s = lax.dot_general(q_rows, k_all, _NT, preferred_element_type=F32)
    n_keys = s.shape[1]
    probs, sink_terms = [], []
    for h in range(N_HEADS):
        sh = s[h * nq:(h + 1) * nq, :]
        if masks:
            cols = []
            for c0 in range(0, n_keys, ATT_BLOCK):
                piece = sh[:, c0:c0 + ATT_BLOCK]
                cols.append(jnp.where(masks[c0], piece, NEG_INF) if c0 in masks else piece)
            sh = jnp.concatenate(cols, axis=1)
        sink = sink_ref[h] * LOG2_E
        m = jnp.maximum(jnp.max(sh, axis=-1, keepdims=True), sink)
        sink_terms.append(jnp.exp2(sink - m))
        probs.append(jnp.exp2(sh - m).astype(BF16))
    v_ext = jnp.concatenate([v_all, jnp.ones((n_keys, LANES), BF16)], axis=1)
    o = jnp.dot(jnp.concatenate(probs, axis=0), v_ext, preferred_element_type=F32)

    def normalised(h):
        rows = o[h * nq:(h + 1) * nq]
        return rows[:, :LANES] * (1.0 / (rows[:, LANES:] + sink_terms[h]))
    for i in range(GQA):
        o_ref[0, r0:r0 + nq, i * LANES:(i + 1) * LANES] = jnp.where(
            low, normalised(i), normalised(GQA + i)).astype(o_ref.dtype)


def _latent_attn_kernel(*refs, nb):
    nk = ATT_STEP_BLOCKS + 2
    q_ref = refs[0]
    k_refs = refs[1:1 + nk]
    v_refs = refs[1 + nk:1 + 2 * nk]
    ck_ref, cv_ref, sink_ref, o_ref = refs[1 + 2 * nk:]
    i = pl.program_id(1)
    r = lax.broadcasted_iota(jnp.int32, (ATT_BLOCK, ATT_BLOCK), 0)
    c = lax.broadcasted_iota(jnp.int32, (ATT_BLOCK, ATT_BLOCK), 1)
    kb = [ref[0].astype(BF16) for ref in k_refs]
    vb = [ref[0].astype(BF16) for ref in v_refs]
    ck = ck_ref[0, 0].astype(BF16)
    cv = cv_ref[0, 0].astype(BF16)
    for j in range(ATT_STEP_BLOCKS):
        blk = i * ATT_STEP_BLOCKS + j
        masks = {0: jnp.logical_and(c >= r, blk > 0), 2 * ATT_BLOCK: jnp.logical_and(c <= r, blk < nb - 1)}
        k_all = jnp.concatenate(kb[j:j + 3] + [ck], axis=0)
        v_all = jnp.concatenate(vb[j:j + 3] + [cv], axis=0)
        _attend(q_ref, j * ATT_BLOCK, k_all, v_all, masks, sink_ref, o_ref)


def _latent_attention(q, k, v, cache_k4, cache_v4, layer, sink):
    bg, lg, _ = q.shape
    nb = lg // ATT_BLOCK
    sb = ATT_STEP_BLOCKS
    past = cache_k4.shape[2]
    tile = pl.BlockSpec((1, sb * ATT_BLOCK, ATT_Q_W), lambda b, i: (b, i, 0))
    kv_blocks = [pl.BlockSpec((1, ATT_BLOCK, ATT_KV_W),
                              lambda b, i, m=m: (b, jnp.clip(i * sb - 1 + m, 0, nb - 1), 0)) for m in range(sb + 2)]
    ctx = pl.BlockSpec((1, 1, past, ATT_KV_W), lambda b, i: (b, layer, 0, 0))
    return pl.pallas_call(
        functools.partial(_latent_attn_kernel, nb=nb),
        grid=(bg, nb // sb),
        in_specs=[tile] + kv_blocks + kv_blocks + [ctx, ctx, pl.BlockSpec(memory_space=pltpu.SMEM)],
        out_specs=tile,
        out_shape=jax.ShapeDtypeStruct((bg, lg, ATT_Q_W), BF16),
        compiler_params=_params("arbitrary", "arbitrary"),
        name="latent_attention",
    )(q, *([k] * (sb + 2)), *([v] * (sb + 2)), cache_k4, cache_v4, sink)


def _context_attn_kernel(q_ref, k_ref, v_ref, sink_ref, o_ref):
    k_all = k_ref[0].astype(BF16)
    v_all = v_ref[0].astype(BF16)
    for j in range(q_ref.shape[1] // ATT_BLOCK):
        _attend(q_ref, j * ATT_BLOCK, k_all, v_all, {}, sink_ref, o_ref)


def _context_attention(q, k, v, sink):
    bg, lg, _ = q.shape
    return pl.pallas_call(
        _context_attn_kernel,
        grid=(bg,),
        in_specs=[pl.BlockSpec((1, lg, ATT_Q_W), lambda b: (b, 0, 0)),
                  pl.BlockSpec((1, lg, ATT_KV_W), lambda b: (b, 0, 0)),
                  pl.BlockSpec((1, lg, ATT_KV_W), lambda b: (b, 0, 0)),
                  pl.BlockSpec(memory_space=pltpu.SMEM)],
        out_specs=pl.BlockSpec((1, lg, ATT_Q_W), lambda b: (b, 0, 0)),
        out_shape=jax.ShapeDtypeStruct((bg, lg, ATT_Q_W), BF16),
        compiler_params=_params("arbitrary"),
        name="context_attention",
    )(q, k, v, sink)


def _powers(er, th, e):
    mag = jnp.exp(er * e)
    return mag * jnp.cos(th * e), mag * jnp.sin(th * e)


def _ssm_prep_kernel(pc_ref, pr_ref, br_ref, bi_ref, cr_ref, ci_ref, crt_ref, cit_ref,
                     kv_out, rs_out, f_out, at_out):
    t = SSM_T
    p = SSM_STATE
    lane2 = lax.broadcasted_iota(jnp.int32, (1, 2 * t), 1)
    lane1 = lax.broadcasted_iota(jnp.int32, (1, t), 1)
    e_kv = [jnp.maximum(lane2 - t, 0).astype(F32), jnp.maximum(t - lane2, 0).astype(F32)]
    m_kv = [lane2 >= t, jnp.logical_and(lane2 >= 1, lane2 <= t)]
    e_rs = [(t - 1 - lane1).astype(F32), lane1.astype(F32)]
    e_f = [(lane1 + 1).astype(F32), (t - lane1).astype(F32)]

    bbr, bbi, pk, prs, pf = [], [], [], [], []
    for d in range(N_DIR):
        lr = pc_ref[d, 0, :, 0:1]
        li = pc_ref[d, 0, :, 1:2]
        step = jnp.exp(pc_ref[d, 0, :, 2:3])
        er = lr * step
        th = li * step
        mag = jnp.exp(er)
        ar = mag * jnp.cos(th)
        ai = mag * jnp.sin(th)
        den = lr * lr + li * li
        nr = ar - 1.0
        fr = (nr * lr + ai * li) / den
        fi = (ai * lr - nr * li) / den
        b_re = br_ref[d, 0]
        b_im = bi_ref[d, 0]
        bbr.append(fr * b_re - fi * b_im)
        bbi.append(fr * b_im + fi * b_re)
        kr, ki = _powers(er, th, e_kv[d])
        pk.append((jnp.where(m_kv[d], kr, 0.0), jnp.where(m_kv[d], ki, 0.0)))
        prs.append(_powers(er, th, e_rs[d]))
        pf.append(_powers(er, th, e_f[d]))
        lr_r = pr_ref[d, 0, 0:1, :]
        li_r = pr_ref[d, 0, 1:2, :]
        step_r = jnp.exp(pr_ref[d, 0, 2:3, :])
        mag_t = jnp.exp(lr_r * step_r * float(t))
        at_out[0, :, d * p:(d + 1) * p] = mag_t * jnp.cos(li_r * step_r * float(t))
        at_out[0, :, (N_DIR + d) * p:(N_DIR + d + 1) * p] = mag_t * jnp.sin(li_r * step_r * float(t))

    cmat = jnp.concatenate([cr_ref[0, 0], -ci_ref[0, 0], cr_ref[1, 0], -ci_ref[1, 0]], axis=1)
    for ci in range(SSM_GROUP_CH):
        rows = []
        for d in range(N_DIR):
            cbr = bbr[d][:, ci:ci + 1]
            cbi = bbi[d][:, ci:ci + 1]
            kr, ki = pk[d]
            rows += [kr * cbr - ki * cbi, kr * cbi + ki * cbr]
        kv_out[0, ci * SSM_GROUP_CH:(ci + 1) * SSM_GROUP_CH, :] = _mm_3pass(cmat, jnp.concatenate(rows, axis=0))
        for d in range(N_DIR):
            cbr = bbr[d][:, ci:ci + 1]
            cbi = bbi[d][:, ci:ci + 1]
            sr, si = prs[d]
            rs_out[0, d * p:(d + 1) * p, ci * t:(ci + 1) * t] = (sr * cbr - si * cbi).astype(BF16)
            rs_out[0, (N_DIR + d) * p:(N_DIR + d + 1) * p, ci * t:(ci + 1) * t] = (sr * cbi + si * cbr).astype(BF16)
    for co in range(SSM_GROUP_CH):
        for d in range(N_DIR):
            ccr = crt_ref[d, 0, :, co:co + 1]
            cci = cit_ref[d, 0, :, co:co + 1]
            fr_, fi_ = pf[d]
            f_out[0, d * p:(d + 1) * p, co * t:(co + 1) * t] = (ccr * fr_ - cci * fi_).astype(BF16)
            f_out[0, (N_DIR + d) * p:(N_DIR + d + 1) * p, co * t:(co + 1) * t] = (-(ccr * fi_ + cci * fr_)).astype(BF16)


def _ssm_prep(p):
    t = SSM_T
    g = SSM_GROUPS
    ns = N_DIR * 2 * SSM_STATE
    spec4 = lambda a, b: pl.BlockSpec((N_DIR, 1, a, b), lambda i: (0, i, 0, 0))
    return pl.pallas_call(
        _ssm_prep_kernel,
        grid=(g,),
        in_specs=[spec4(SSM_STATE, 3), spec4(3, SSM_STATE),
                  spec4(SSM_STATE, SSM_GROUP_CH), spec4(SSM_STATE, SSM_GROUP_CH),
                  spec4(SSM_GROUP_CH, SSM_STATE), spec4(SSM_GROUP_CH, SSM_STATE),
                  spec4(SSM_STATE, SSM_GROUP_CH), spec4(SSM_STATE, SSM_GROUP_CH)],
        out_specs=[pl.BlockSpec((1, SSM_GROUP_CH * SSM_GROUP_CH, 2 * t), lambda i: (i, 0, 0)),
                   pl.BlockSpec((1, ns, SSM_GROUP_CH * t), lambda i: (i, 0, 0)),
                   pl.BlockSpec((1, ns, SSM_GROUP_CH * t), lambda i: (i, 0, 0)),
                   pl.BlockSpec((1, 1, ns), lambda i: (i, 0, 0))],
        out_shape=[jax.ShapeDtypeStruct((g, SSM_GROUP_CH * SSM_GROUP_CH, 2 * t), F32),
                   jax.ShapeDtypeStruct((g, ns, SSM_GROUP_CH * t), BF16),
                   jax.ShapeDtypeStruct((g, ns, SSM_GROUP_CH * t), BF16),
                   jax.ShapeDtypeStruct((g, 1, ns), F32)],
        compiler_params=_params("arbitrary"),
        name="ssm_prep",
    )(p["ssm_pc"], p["ssm_pr"], p["ssm_b_re"], p["ssm_b_im"], p["ssm_c_re"], p["ssm_c_im"],
      p["ssm_c_re_t"], p["ssm_c_im_t"])


def _ssm_kernel(*refs, dims):
    ng = len(dims)
    u_refs = refs[:ng]
    kv_ref, rs_ref, f_ref, at_ref = refs[ng:ng + 4]
    h0_refs = refs[ng + 4:2 * ng + 4]
    d_ref = refs[2 * ng + 4]
    y_outs = refs[2 * ng + 5:3 * ng + 5]
    hfin_outs = refs[3 * ng + 5:4 * ng + 5]
    m_scr = refs[4 * ng + 5]
    scratch = refs[4 * ng + 6:]
    t = SSM_T
    nch = SSM_GROUP_CH
    g = pl.program_id(0)
    half = N_DIR * SSM_STATE

    def gen(ci, carry):
        r0 = pl.multiple_of(ci * t, t)
        for co in range(nch):
            row = kv_ref[0, pl.ds(ci * nch + co, 1), :]
            rolled = pltpu.roll(jnp.broadcast_to(row, (t, 2 * t)), t, 1, stride=1, stride_axis=0)
            m_scr[pl.ds(r0, t), co * t:(co + 1) * t] = rolled[:, :t].astype(BF16)
        return carry
    lax.fori_loop(0, nch, gen, 0)

    a_re = at_ref[0, :, :half]
    a_im = at_ref[0, :, half:]
    is_fwd = lax.broadcasted_iota(jnp.int32, (1, half), 1) < SSM_STATE
    for gi, (bg, nc) in enumerate(dims):
        u_ref, h0_ref, y_out, hfin_out = u_refs[gi], h0_refs[gi], y_outs[gi], hfin_outs[gi]
        s_scr, hpf_scr, hpb_scr = scratch[3 * gi:3 * gi + 3]
        u_flat = jnp.concatenate([u_ref[0, ci].astype(BF16) for ci in range(nch)], axis=1)

        s = lax.dot_general(u_flat, rs_ref[0], _NT, preferred_element_type=F32)
        s_scr[0] = s[:, :half]
        s_scr[1] = s[:, half:]

        def advance(re, im, rows, s_scr=s_scr):
            return (a_re * re - a_im * im + s_scr[0, rows, :], a_re * im + a_im * re + s_scr[1, rows, :])

        def scan(i, carry, bg=bg, nc=nc, hpf_scr=hpf_scr, hpb_scr=hpb_scr, advance=advance):
            f_re, f_im, b_re, b_im = carry
            rows_f = pl.ds(i, bg, stride=nc)
            rows_b = pl.ds(nc - 1 - i, bg, stride=nc)
            hpf_scr[0, rows_f, :] = f_re
            hpf_scr[1, rows_f, :] = f_im
            hpb_scr[0, rows_b, :] = b_re
            hpb_scr[1, rows_b, :] = b_im
            return advance(f_re, f_im, rows_f) + advance(b_re, b_im, rows_b)
        h0_re = h0_ref[0, :, :half]
        h0_im = h0_ref[0, :, half:]
        f_re, f_im, b_re, b_im = lax.fori_loop(0, nc, scan, (h0_re, h0_im, h0_re, h0_im))
        hfin_out[0, :, :half] = jnp.where(is_fwd, f_re, b_re)
        hfin_out[0, :, half:] = jnp.where(is_fwd, f_im, b_im)
        hprev = jnp.concatenate([jnp.where(is_fwd, hpf_scr[0], hpb_scr[0]),
                                 jnp.where(is_fwd, hpf_scr[1], hpb_scr[1])], axis=1)

        y = (jnp.dot(hprev.astype(BF16), f_ref[0], preferred_element_type=F32)
             + jnp.dot(u_flat, m_scr[...], preferred_element_type=F32))
        for co in range(nch):
            y_out[0, co] = (y[:, co * t:(co + 1) * t]
                            + d_ref[g * nch + co] * u_ref[0, co].astype(F32)).astype(y_out.dtype)


def _ssm_mix(uts, tables, h0s, d_skip, shapes):
    t = SSM_T
    g = SSM_GROUPS
    nch = SSM_GROUP_CH
    ns = N_DIR * 2 * SSM_STATE
    kv, rs, f, at = tables
    dims = tuple((bg, lg // t) for bg, lg in shapes)
    u_specs = [pl.BlockSpec((1, nch, bg * nc, t), lambda i: (i, 0, 0, 0)) for bg, nc in dims]
    h_specs = [pl.BlockSpec((1, bg, ns), lambda i: (i, 0, 0)) for bg, _ in dims]
    scratch = [pltpu.VMEM((nch * t, nch * t), BF16)]
    for bg, nc in dims:
        scratch += [pltpu.VMEM((2, bg * nc, ns // 2), F32)] * 3
    outs = pl.pallas_call(
        functools.partial(_ssm_kernel, dims=dims),
        grid=(g,),
        in_specs=u_specs + [pl.BlockSpec((1, nch * nch, 2 * t), lambda i: (i, 0, 0)),
                            pl.BlockSpec((1, ns, nch * t), lambda i: (i, 0, 0)),
                            pl.BlockSpec((1, ns, nch * t), lambda i: (i, 0, 0)),
                            pl.BlockSpec((1, 1, ns), lambda i: (i, 0, 0))]
                 + h_specs + [pl.BlockSpec(memory_space=pltpu.SMEM)],
        out_specs=u_specs + h_specs,
        out_shape=[jax.ShapeDtypeStruct((g, nch, bg * nc, t), BF16) for bg, nc in dims]
                  + [jax.ShapeDtypeStruct((g, bg, ns), F32) for bg, _ in dims],
        scratch_shapes=scratch,
        compiler_params=_params("arbitrary"),
        name="ssm_mix",
    )(*[ut.reshape(g, nch, bg * nc, t) for ut, (bg, nc) in zip(uts, dims)], kv, rs, f, at, *h0s, d_skip)
    n = len(dims)
    return ([y.reshape(g * nch, bg * nc * t) for y, (bg, nc) in zip(outs[:n], dims)], list(outs[n:]))


def _head_blocks(x, head_masks):
    xb = x.astype(BF16)
    return jnp.concatenate([xb * m for m in head_masks], axis=0)


def _hmm(a, b_blocks):
    return jnp.dot(a.astype(BF16), b_blocks, preferred_element_type=F32)


def _tri_inverse_heads(a_list, same16, same32, eye, head_masks):
    n = a_list[0].shape[0]
    pw = [jnp.where(same16, -a, 0.0) for a in a_list]
    x = [eye + p for p in pw]
    pw = [_hmm(p, _head_blocks(p, head_masks)) for p in pw]
    yield
    for _ in range(2):
        both = [_hmm(jnp.concatenate([xi, p], axis=0), _head_blocks(p, head_masks)) for xi, p in zip(x, pw)]
        x = [xi + b[:n] for xi, b in zip(x, both)]
        pw = [b[n:] for b in both]
        yield
    x = [xi + _hmm(xi, _head_blocks(p, head_masks)) for xi, p in zip(x, pw)]
    yield
    for mask in (jnp.logical_and(same32, jnp.logical_not(same16)), jnp.logical_not(same32)):
        t = [_hmm(jnp.where(mask, a, 0.0), _head_blocks(xi, head_masks)) for a, xi in zip(a_list, x)]
        yield
        x = [xi - _hmm(xi, _head_blocks(ti, head_masks)) for xi, ti in zip(x, t)]
        yield
    return x


def _gdn_kernel(xf_ref, xb_ref, abf_ref, abb_ref, abtf_ref, abtb_ref, prow_ref, pcol_ref, ex_ref, s0_ref,
                of_ref, ob_ref, sfin_ref, s_scr, *, nt, tt):
    n = pl.program_id(1)
    ck = GDN_CHUNK
    cpt = tt // ck
    hw = GDN_HEADS * ck

    @pl.when(n == 0)
    def _():
        s_scr[...] = s0_ref[0]

    r = lax.broadcasted_iota(jnp.int32, (ck, hw), 0)
    lane = lax.broadcasted_iota(jnp.int32, (ck, hw), 1)
    c = lane % ck
    same16 = (r // 16) == (c // 16)
    same32 = (r // 32) == (c // 32)
    eye = (r == c).astype(F32)
    head_sel = [(lane // ck) == h for h in range(GDN_HEADS)]
    lane1 = lax.broadcasted_iota(jnp.int32, (1, hw), 1)
    head_masks = [((lane1 // ck) == h).astype(BF16) for h in range(GDN_HEADS)]
    row_in = lax.broadcasted_iota(jnp.int32, (tt, LANES), 0) % ck
    lane_in = lax.broadcasted_iota(jnp.int32, (GATE_W, tt), 1) % ck
    is_decay_lane = lax.broadcasted_iota(jnp.int32, (1, LANES), 1) < GATE_W // 2

    steps = [[] for _ in range(cpt)]
    for d, (x_ref, ab_ref, abt_ref, o_ref) in enumerate(((xf_ref, abf_ref, abtf_ref, of_ref),
                                                          (xb_ref, abb_ref, abtb_ref, ob_ref))):
        rev = d == 1
        ab = ab_ref[0]
        g_col = -jnp.exp(prow_ref[0:1, :]) * _softplus(ab + prow_ref[1:2, :])
        abt = abt_ref[0]
        g_row = -jnp.exp(pcol_ref[:, 0:1]) * _softplus(abt + pcol_ref[:, 1:2])
        sh = 1
        while sh < ck:
            if rev:
                g_col = g_col + jnp.where(row_in < ck - sh, pltpu.roll(g_col, tt - sh, 0), 0.0)
                g_row = g_row + jnp.where(lane_in < ck - sh, pltpu.roll(g_row, tt - sh, 1), 0.0)
            else:
                g_col = g_col + jnp.where(row_in >= sh, pltpu.roll(g_col, sh, 0), 0.0)
                g_row = g_row + jnp.where(lane_in >= sh, pltpu.roll(g_row, sh, 1), 0.0)
            sh *= 2
        spread = _mm_exact_rhs(jnp.where(is_decay_lane, g_col, jax.nn.sigmoid(ab)), ex_ref[d])
        incl = (r <= c) if rev else (r >= c)
        strict = (r < c) if rev else (r > c)
        last = 0 if rev else ck - 1
        for i, cc in enumerate(range(cpt - 1, -1, -1) if rev else range(cpt)):
            c0 = cc * ck
            gc = spread[c0:c0 + ck, :hw]
            gr = jnp.concatenate([g_row[d * GDN_HEADS + h:d * GDN_HEADS + h + 1, c0:c0 + ck]
                                  for h in range(GDN_HEADS)], axis=1)
            steps[i].append(dict(
                d=d, o_ref=o_ref, rows=slice(c0, c0 + ck), strict=strict,
                q=x_ref[0, c0:c0 + ck, :GDN_K_W], k=x_ref[0, c0:c0 + ck, GDN_K_W:2 * GDN_K_W],
                v=x_ref[0, c0:c0 + ck, 2 * GDN_K_W:],
                gc=gc, beta=spread[c0:c0 + ck, hw:], g_last=gc[last:last + 1, :],
                decay=jnp.exp(jnp.where(incl, gc - gr, -jnp.inf))))
    def chunk_local(items):
        for it in items:
            k_blocks = _head_blocks(it["k"], head_masks)
            kq = lax.dot_general(jnp.concatenate([it["k"], it["q"]], axis=0).astype(BF16), k_blocks, _NT,
                                 preferred_element_type=F32)
            it["kk"] = kq[:ck]
            it["attn"] = kq[ck:] * it["decay"]
        yield
        t_inv = yield from _tri_inverse_heads(
            [jnp.where(it["strict"], it["kk"] * it["decay"] * it["beta"], 0.0) for it in items],
            same16, same32, eye, head_masks)
        for it, ti in zip(items, t_inv):
            egc = jnp.exp(it["gc"])
            it["u"] = _hmm(ti, _head_blocks(it["v"] * it["beta"], head_masks))
            w = _hmm(ti, _head_blocks(it["k"] * (it["beta"] * egc), head_masks))
            it["w_qe"] = jnp.concatenate([w, it["q"] * egc], axis=0).astype(BF16)
            it["kd"] = (it["k"] * jnp.exp(it["g_last"] - it["gc"])).astype(BF16)
        yield

    state = [s_scr[d] for d in range(N_DIR)]

    def recurrence(some_steps):
        for st in some_steps:
            from_state = [_hmm(it["w_qe"], _head_blocks(state[it["d"]], head_masks)) for it in st]
            v_new = [it["u"] - fs[:ck] for it, fs in zip(st, from_state)]
            yield
            for it, fs, vn in zip(st, from_state, v_new):
                it["o_ref"][0, it["rows"], :] = fs[ck:] + _hmm(it["attn"], _head_blocks(vn, head_masks))
            for it, vn in zip(st, v_new):
                cross = lax.dot_general(it["kd"], vn.astype(BF16), _TN, preferred_element_type=F32)
                upd = functools.reduce(lambda a, b: a + b,
                                       [jnp.where(head_sel[h], cross[h * ck:(h + 1) * ck, :], 0.0)
                                        for h in range(GDN_HEADS)])
                state[it["d"]] = state[it["d"]] * jnp.exp(it["g_last"]) + upd
            yield

    per = GDN_GROUP_CHUNKS
    groups = [steps[i:i + per] for i in range(0, cpt, per)]
    for gi in range(len(groups) + 1):
        running = []
        if gi < len(groups):
            running.append(chunk_local([it for st in groups[gi] for it in st]))
        if gi >= 1:
            running.append(recurrence(groups[gi - 1]))
        while running:
            for gen in list(running):
                if next(gen, "done") == "done":
                    running.remove(gen)
    for d in range(N_DIR):
        s_scr[d] = state[d]

    @pl.when(n == nt - 1)
    def _():
        sfin_ref[0] = s_scr[...]


def _gdn_mix(qkv, gab, gabt, prow, pcol, s0, *, tt):
    bg, lg, w = qkv.shape
    nt = lg // tt
    fwd = lambda b, n: (b, n, 0)
    bwd = lambda b, n: (b, nt - 1 - n, 0)
    gate_col = jnp.arange(LANES)[None, :, None]
    want = (jnp.arange(N_DIR)[:, None, None] * GDN_HEADS + jnp.arange(GDN_V_W)[None, None, :] // GDN_DV)
    spread = jnp.concatenate([gate_col == want, gate_col == want + GATE_W // 2], axis=2).astype(BF16)
    st_spec = pl.BlockSpec((1, N_DIR, GDN_DK, GDN_V_W), lambda b, n: (b, 0, 0, 0))
    o_f, o_b, s_fin = pl.pallas_call(
        functools.partial(_gdn_kernel, nt=nt, tt=tt),
        grid=(bg, nt),
        in_specs=[pl.BlockSpec((1, tt, w), fwd), pl.BlockSpec((1, tt, w), bwd),
                  pl.BlockSpec((1, tt, LANES), fwd), pl.BlockSpec((1, tt, LANES), bwd),
                  pl.BlockSpec((1, GATE_W, tt), lambda b, n: (b, 0, n)),
                  pl.BlockSpec((1, GATE_W, tt), lambda b, n: (b, 0, nt - 1 - n)),
                  pl.BlockSpec((2, LANES), lambda b, n: (0, 0)),
                  pl.BlockSpec((GATE_W, 2), lambda b, n: (0, 0)),
                  pl.BlockSpec((N_DIR, LANES, 2 * GDN_V_W), lambda b, n: (0, 0, 0)),
                  st_spec],
        out_specs=[pl.BlockSpec((1, tt, GDN_V_W), fwd), pl.BlockSpec((1, tt, GDN_V_W), bwd), st_spec],
        out_shape=[jax.ShapeDtypeStruct((bg, lg, GDN_V_W), F32), jax.ShapeDtypeStruct((bg, lg, GDN_V_W), F32),
                   jax.ShapeDtypeStruct((bg, N_DIR, GDN_DK, GDN_V_W), F32)],
        scratch_shapes=[pltpu.VMEM((N_DIR, GDN_DK, GDN_V_W), F32)],
        compiler_params=_params("arbitrary", "arbitrary"),
        name="gdn_mix",
    )(qkv, qkv, gab, gab, gabt, gabt, prow, pcol, spread, s0)
    return o_f, o_b, s_fin


def _gelu_tanh(x):
    return 0.5 * x * (1.0 + jnp.tanh(math.sqrt(2.0 / math.pi) * (x + 0.044715 * (x * x * x))))


def _out_kernel(x_ref, ga_ref, shm_ref, scm_ref, gm_ref, attn_ref, yt_ref, of_ref, ob_ref, gz_ref,
                g2_ref, gng_ref, bd_ref, wglut_ref, bglu_ref, wo_ref, w1_ref, w2_ref, o_ref):
    x = x_ref[0]
    z = _gelu_tanh(yt_ref[...].astype(F32))
    gate = jax.nn.sigmoid(jnp.dot(wglut_ref[...], z.astype(BF16), preferred_element_type=F32) + bglu_ref[...])
    ssm_t = (z * gate).astype(BF16)
    o = of_ref[0] + ob_ref[0]
    gdn = o * lax.rsqrt(_seg_mean_sq(o, bd_ref) + EPS) * gng_ref[...] * _silu(gz_ref[0])
    mixed = (jnp.dot(attn_ref[0], wo_ref[:ATT_Q_W, :], preferred_element_type=F32)
             + lax.dot_general(ssm_t, wo_ref[ATT_Q_W:ATT_Q_W + SSM_WIDTH, :], _TN, preferred_element_type=F32)
             + jnp.dot(gdn.astype(BF16), wo_ref[ATT_Q_W + SSM_WIDTH:, :], preferred_element_type=F32))
    x1 = x + ga_ref[0] * mixed
    ms = jnp.mean(x1 * x1, axis=-1, keepdims=True)
    h2 = ((x1 * lax.rsqrt(ms + EPS) * g2_ref[...]) * (1.0 + scm_ref[0]) + shm_ref[0]).astype(BF16)
    acc = None
    fc = D_MODEL
    for j in range(D_FF // fc):
        a = jnp.maximum(jnp.dot(h2, w1_ref[:, j * fc:(j + 1) * fc], preferred_element_type=F32), 0.0)
        part = jnp.dot((a * a).astype(BF16), w2_ref[j * fc:(j + 1) * fc, :], preferred_element_type=F32)
        acc = part if acc is None else acc + part
    o_ref[0] = x1 + gm_ref[0] * acc


def _out_projection(x, mod3, mod_row, attn, yt, o_f, o_b, gz, p, consts, *, tm):
    bg, lg, _ = x.shape
    nt = lg // tm
    row = mod_row
    full = lambda shape: pl.BlockSpec(shape, lambda t, b: (0,) * len(shape), pipeline_mode=pl.Buffered(1))
    tok = lambda w: pl.BlockSpec((1, tm, w), lambda t, b: (b, t, 0))
    modc = lambda j: pl.BlockSpec((1, 1, D_MODEL), lambda t, b: (row(b), 0, j))
    mix_w = ATT_Q_W + SSM_WIDTH + GDN_V_W
    return pl.pallas_call(
        _out_kernel,
        grid=(nt, bg),
        in_specs=[tok(D_MODEL), modc(2), modc(3), modc(4), modc(5),
                  tok(ATT_Q_W), pl.BlockSpec((SSM_WIDTH, tm), lambda t, b: (0, b * nt + t)),
                  tok(GDN_V_W), tok(GDN_V_W), tok(GDN_V_W),
                  full((1, D_MODEL)), full((1, GDN_V_W)), full((GDN_V_W, GDN_V_W)),
                  full((SSM_WIDTH, SSM_WIDTH)), full((SSM_WIDTH, 1)),
                  full((mix_w, D_MODEL)), full((D_MODEL, D_FF)), full((D_FF, D_MODEL))],
        out_specs=tok(D_MODEL),
        out_shape=jax.ShapeDtypeStruct((bg, lg, D_MODEL), F32),
        compiler_params=_params("arbitrary", "arbitrary"),
        name="out_projection",
    )(x, mod3, mod3, mod3, mod3, attn, yt, o_f, o_b, gz,
      p["norm2_g"], p["gdn_norm_g"], consts["bd_mean"][:GDN_V_W, :GDN_V_W],
      p["w_glu_t"], p["b_glu"], p["w_out"], p["w_ff1"], p["w_ff2"])


def _constants(max_len):
    n_freq = HEAD_DIM // 4
    rows = jnp.repeat(jnp.arange(max_len // GRID_W, dtype=F32), GRID_W)
    cols = jnp.tile(jnp.arange(GRID_W, dtype=F32), max_len // GRID_W)
    inv_freq = jnp.power(ROPE_BASE, -jnp.arange(n_freq, dtype=F32) / n_freq)
    ang = jnp.concatenate([rows[:, None] * inv_freq, cols[:, None] * inv_freq], axis=-1)
    cos = jnp.repeat(jnp.cos(ang), 2, axis=-1)
    sin = jnp.repeat(jnp.sin(ang), 2, axis=-1) * jnp.tile(jnp.array([-1.0, 1.0], F32), HEAD_DIM // 2)
    seg = jnp.arange(ATT_Q_W) // HEAD_DIM
    same = seg[:, None] == seg[None, :]
    return {
        "cos": jnp.tile(cos, (1, LANES // HEAD_DIM)),
        "sin": jnp.tile(sin, (1, LANES // HEAD_DIM)),
        "bd_mean": jnp.where(same, 1.0 / HEAD_DIM, 0.0).astype(BF16),
    }


def _layer_params(w):
    w_in = w["w_in"]
    off = [0]
    for size in (ATT_Q_W, ATT_KV_W, ATT_KV_W, SSM_WIDTH, GDN_K_W, GDN_K_W, GDN_V_W, GDN_V_W, GATE_W):
        off.append(off[-1] + size)
    wab = w_in[:, off[8]:off[9]]
    col = lambda a: a.reshape(-1, 1)
    pad_lanes = lambda a: jnp.pad(a, ((0, 0), (0, LANES - a.shape[1])))
    alog = w["gdn_a_log"].reshape(1, -1)
    dtb = w["gdn_dt_bias"].reshape(1, -1)
    zeros = jnp.zeros_like(alog)
    ls = jnp.broadcast_to(w["ssm_log_step"][..., None], (N_DIR, SSM_GROUPS, SSM_STATE))
    pair_order = jnp.arange(N_HEADS).reshape(N_KV_HEADS, GQA).T.reshape(-1)
    head_cols = (pair_order[:, None] * HEAD_DIM + jnp.arange(HEAD_DIM)[None, :]).reshape(-1)
    w_out = w["w_out"]
    w_out = jnp.concatenate([w_out[:ATT_Q_W][head_cols], w_out[ATT_Q_W:]], axis=0)
    return {
        "norm1_g": w["norm1_g"].reshape(1, -1), "norm2_g": w["norm2_g"].reshape(1, -1),
        "wq": w_in[:, off[0]:off[1]][:, head_cols].astype(BF16), "wkv": w_in[:, off[1]:off[3]].astype(BF16),
        "wut": w_in[:, off[3]:off[4]].T.astype(BF16), "wg": w_in[:, off[4]:off[7]].astype(BF16),
        "wz": w_in[:, off[7]:off[8]].astype(BF16),
        "wab": pad_lanes(wab).astype(BF16), "wabt": wab.T.astype(BF16),
        "q_norm_g": jnp.tile(w["q_norm_g"], N_HEADS).reshape(1, -1),
        "k_norm_g": jnp.tile(w["k_norm_g"], N_KV_HEADS).reshape(1, -1),
        "attn_sink": w["attn_sink"],
        "ssm_pc": jnp.stack([w["ssm_lam_re"], w["ssm_lam_im"], ls], axis=-1),
        "ssm_pr": jnp.stack([w["ssm_lam_re"], w["ssm_lam_im"], ls], axis=-2),
        "ssm_b_re": w["ssm_b_re"], "ssm_b_im": w["ssm_b_im"],
        "ssm_c_re": w["ssm_c_re"], "ssm_c_im": w["ssm_c_im"],
        "ssm_c_re_t": jnp.swapaxes(w["ssm_c_re"], -1, -2), "ssm_c_im_t": jnp.swapaxes(w["ssm_c_im"], -1, -2),
        "ssm_d": w["ssm_d"],
        "w_glu_t": w["ssm_w_glu"].T.astype(BF16), "b_glu": col(w["ssm_b_glu"]),
        "gdn_conv_w": w["gdn_conv_w"],
        "gdn_prow": pad_lanes(jnp.concatenate([jnp.concatenate([alog, zeros], 1), jnp.concatenate([dtb, zeros], 1)], 0)),
        "gdn_pcol": jnp.concatenate([jnp.concatenate([alog, zeros], 1), jnp.concatenate([dtb, zeros], 1)], 0).T,
        "gdn_norm_g": jnp.tile(w["gdn_norm_g"], GDN_HEADS).reshape(1, -1),
        "w_out": w_out.astype(BF16), "w_ff1": w["w_ff1"].astype(BF16), "w_ff2": w["w_ff2"].astype(BF16),
    }


def _ssm_states_to_lanes(h0):
    bg, depth = h0.shape[:2]
    return h0.transpose(1, 4, 0, 3, 2, 5).reshape(depth, SSM_GROUPS, bg, 2 * N_DIR * SSM_STATE)


def _gdn_states_to_lanes(s0):
    bg, depth = s0.shape[:2]
    return s0.transpose(1, 0, 2, 4, 3, 5).reshape(depth, bg, N_DIR, GDN_DK, GDN_V_W)


def _ssm_lanes_to_states(h):
    depth, _, bg, _ = h.shape
    return h.reshape(depth, SSM_GROUPS, bg, 2, N_DIR, SSM_STATE).transpose(2, 0, 4, 3, 1, 5)


def _gdn_lanes_to_states(s):
    depth, bg = s.shape[:2]
    return s.reshape(depth, bg, N_DIR, GDN_DK, GDN_HEADS, GDN_DV).transpose(1, 0, 2, 4, 3, 5)


def _layer(groups, mod3, p, consts, layer):
    tables = _ssm_prep(p)
    proj = []
    for gr in groups:
        lg = gr["x"].shape[1]
        proj.append(_in_projection(gr["x"], mod3, gr["mod_row"], p, consts, rope=gr["ctx_kv"] is not None,
                                   tm=min(lg, 512)))
    shapes = [gr["x"].shape[:2] for gr in groups]
    yts, h_fins = _ssm_mix([pr[3] for pr in proj], tables, [gr["ssm_h0"] for gr in groups], p["ssm_d"], shapes)
    xs, aux = [], []
    for gr, pr, yt, h_fin in zip(groups, proj, yts, h_fins):
        bg, lg, _ = gr["x"].shape
        q, k, v, _, qkv, gz, gab, gabt = pr
        if gr["ctx_kv"] is not None:
            attn = _latent_attention(q, k, v, gr["ctx_kv"][0], gr["ctx_kv"][1], layer, p["attn_sink"])
        else:
            attn = _context_attention(q, k, v, p["attn_sink"])
        o_f, o_b, s_fin = _gdn_mix(qkv, gab, gabt, p["gdn_prow"], p["gdn_pcol"], gr["gdn_s0"], tt=min(lg, GDN_TILE))
        if gr["ctx_kv"] is None:
            flat = lambda a: a.reshape(1, bg * lg, a.shape[-1])
            x_new = _out_projection(flat(gr["x"]), mod3, gr["mod_row"], flat(attn), yt, flat(o_f), flat(o_b),
                                    flat(gz), p, consts, tm=min(bg * lg, 512)).reshape(bg, lg, D_MODEL)
        else:
            x_new = _out_projection(gr["x"], mod3, gr["mod_row"], attn, yt, o_f, o_b, gz, p, consts, tm=min(lg, 512))
        xs.append(x_new)
        aux.append((k, v, h_fin, s_fin))
    return xs, aux


def kernel(x_prompt, x_sample, c, cache_k, cache_v, state_ssm, state_gdn, c_ctx, norm1_g, norm2_g, w_mod, b_mod, w_in, q_norm_g, k_norm_g, attn_sink, ssm_lam_re, ssm_lam_im, ssm_log_step, ssm_b_re, ssm_b_im, ssm_c_re, ssm_c_im, ssm_d, ssm_w_glu, ssm_b_glu, gdn_conv_w, gdn_a_log, gdn_dt_bias, gdn_norm_g, w_out, w_ff1, w_ff2):
    w = dict(norm1_g=norm1_g, norm2_g=norm2_g, w_in=w_in, q_norm_g=q_norm_g, k_norm_g=k_norm_g, attn_sink=attn_sink,
             ssm_lam_re=ssm_lam_re, ssm_lam_im=ssm_lam_im, ssm_log_step=ssm_log_step, ssm_b_re=ssm_b_re,
             ssm_b_im=ssm_b_im, ssm_c_re=ssm_c_re, ssm_c_im=ssm_c_im, ssm_d=ssm_d, ssm_w_glu=ssm_w_glu,
             ssm_b_glu=ssm_b_glu, gdn_conv_w=gdn_conv_w, gdn_a_log=gdn_a_log, gdn_dt_bias=gdn_dt_bias,
             gdn_norm_g=gdn_norm_g, w_out=w_out, w_ff1=w_ff1, w_ff2=w_ff2)
    n_ctx, seq, _ = x_prompt.shape
    n_dec, dec_seq, _ = x_sample.shape
    depth = w_in.shape[0]
    past = cache_k.shape[2]

    n_rows = -(-(n_dec + 1) // SUBLANES) * SUBLANES
    cond = jnp.zeros((n_rows, D_MODEL), F32).at[:n_dec].set(c).at[n_dec].set(c_ctx)
    mod = _modulation(cond, w_mod, b_mod)

    consts = _constants(max(seq, dec_seq))
    cache_k4 = cache_k.reshape(n_dec, depth, past, ATT_KV_W)
    cache_v4 = cache_v.reshape(n_dec, depth, past, ATT_KV_W)
    ssm_zero = jnp.zeros((SSM_GROUPS, n_ctx, 2 * N_DIR * SSM_STATE), F32)
    gdn_zero = jnp.zeros((n_ctx, N_DIR, GDN_DK, GDN_V_W), F32)
    ssm_h0 = _ssm_states_to_lanes(state_ssm)
    gdn_s0 = _gdn_states_to_lanes(state_gdn)

    params = jax.vmap(_layer_params)(w)
    xp, xs = x_prompt, x_sample
    ks, vs, ss, gs = [], [], [], []
    for l in range(depth):
        p = {name: value[l] for name, value in params.items()}
        mod3 = mod[l].reshape(n_rows, 1, N_MOD * D_MODEL)
        groups = [dict(x=xp, mod_row=lambda b: n_dec, ctx_kv=None, ssm_h0=ssm_zero, gdn_s0=gdn_zero),
                  dict(x=xs, mod_row=lambda b: b, ctx_kv=(cache_k4, cache_v4), ssm_h0=ssm_h0[l], gdn_s0=gdn_s0[l])]
        (xp, xs), ((k_l, v_l, s_l, g_l), _) = _layer(groups, mod3, p, consts, l)
        ks.append(k_l.reshape(n_ctx, seq, N_KV_HEADS, HEAD_DIM))
        vs.append(v_l.reshape(n_ctx, seq, N_KV_HEADS, HEAD_DIM))
        ss.append(s_l)
        gs.append(g_l)
    return (xp, xs, jnp.stack(ks, axis=1), jnp.stack(vs, axis=1),
            _ssm_lanes_to_states(jnp.stack(ss)), _gdn_lanes_to_states(jnp.stack(gs)))
```
